```python
import jax, jax.numpy as jnp
from jax import lax
import numpy as np

D_MODEL = 1024
BATCH = 4
SEQ = 4096
DEPTH = 1
DEC_BATCH = 32
DEC_SEQ = 4
PAST_LEN = 16384
PAGE_SIZE = 128

HEAD_DIM = 64
ROT_DIM = HEAD_DIM // 4
ROPE_THETA = 500000.0
H_A = 8
KV_A = 2
H_IDX = 8
D_IDX = 64
TOPK_IDX = 256
QBLK_A = 128
H_B = 8
MOBA_BLOCK = 256
TOPK_BLK = 3
QBLK_B = 32
N_GROUPS = 4
EXPERTS_PER_GROUP = 8
N_EXPERTS = N_GROUPS * EXPERTS_PER_GROUP
TOPK_EXPERT = 2
D_EXPERT = 256
MOE_BLK = 128
RMS_EPS = 1e-6
MIX_SPLIT = (H_A * HEAD_DIM, KV_A * HEAD_DIM, KV_A * HEAD_DIM, H_IDX * D_IDX, D_IDX, H_IDX,
             H_B * HEAD_DIM, H_B * HEAD_DIM, H_B * HEAD_DIM, D_MODEL, D_MODEL)
D_IN = sum(MIX_SPLIT)

kernel_name = 'dsa_moba_gated_hier_moe_step'


def rmsnorm(x, g):
    xf = x.astype(jnp.float32)
    r = lax.rsqrt(jnp.mean(xf * xf, axis=-1, keepdims=True) + RMS_EPS)
    return (xf * r).astype(x.dtype) * g


def rope_partial(x, pos):
    half = ROT_DIM // 2
    inv = ROPE_THETA ** (-jnp.arange(half, dtype=jnp.float32) / half)
    ang = pos.astype(jnp.float32)[:, None] * inv[None, :]
    cos = jnp.cos(ang)[:, None, :]
    sin = jnp.sin(ang)[:, None, :]
    xr = x[..., :ROT_DIM].astype(jnp.float32)
    x1, x2 = xr[..., :half], xr[..., half:]
    rot = jnp.concatenate([x1 * cos - x2 * sin, x2 * cos + x1 * sin], axis=-1)
    return jnp.concatenate([rot.astype(x.dtype), x[..., ROT_DIM:]], axis=-1)


def mixer_project(h, w_in, pos):
    B, T, _ = h.shape
    cuts = np.cumsum(MIX_SPLIT)[:-1].tolist()
    qa, ka, va, qi, ki, wi, qb, kb, vb, ga, gb = jnp.split(h @ w_in, cuts, axis=-1)
    qa = rope_partial(qa.reshape(B, T, H_A, HEAD_DIM), pos)
    ka = rope_partial(ka.reshape(B, T, KV_A, HEAD_DIM), pos)
    va = va.reshape(B, T, KV_A, HEAD_DIM)
    qi = rope_partial(qi.reshape(B, T, H_IDX, D_IDX), pos)
    ki = rope_partial(ki[:, :, None, :], pos)[:, :, 0]
    qb = rope_partial(qb.reshape(B, T, H_B, HEAD_DIM), pos)
    kb = rope_partial(kb.reshape(B, T, H_B, HEAD_DIM), pos)
    vb = vb.reshape(B, T, H_B, HEAD_DIM)
    return qa, ka, va, qi, ki, wi, qb, kb, vb, ga, gb


def gather_paged(pool, page_table, pos, *extra):
    b = jnp.arange(page_table.shape[0]).reshape((-1,) + (1,) * (pos.ndim - 1))
    phys = page_table[b, pos // PAGE_SIZE]
    return pool[(phys, pos % PAGE_SIZE) + extra]


def index_scores(qi, wi, ki, qpos, kpos):
    s = jax.nn.relu(jnp.einsum('bqhd,bsd->bqhs', qi, ki).astype(jnp.float32))
    w = wi.astype(jnp.float32) * (H_IDX ** -0.5 * D_IDX ** -0.5)
    score = jnp.einsum('bqhs,bqh->bqs', s, w)
    return jnp.where(kpos[None, None, :] <= qpos[None, :, None], score, -jnp.inf)


def sparse_gqa(q, k_sel, v_sel, valid):
    B, Q = q.shape[:2]
    qg = q.reshape(B, Q, KV_A, H_A // KV_A, HEAD_DIM)
    logits = jnp.einsum('bqgrd,bqkgd->bqgrk', qg, k_sel).astype(jnp.float32) * HEAD_DIM ** -0.5
    logits = jnp.where(valid[:, :, None, None, :], logits, -jnp.inf)
    p = jax.nn.softmax(logits, axis=-1).astype(v_sel.dtype)
    o = jnp.einsum('bqgrk,bqkgd->bqgrd', p, v_sel)
    return o.reshape(B, Q, H_A * HEAD_DIM)


def dsa_prompt(qa, ka, va, qi, ki, wi):
    B, T = qa.shape[:2]
    k_top = min(TOPK_IDX, T // 4)
    kpos = jnp.arange(T)
    bidx = jnp.arange(B)[:, None, None]

    def block(i):
        t0 = i * QBLK_A
        qpos = t0 + jnp.arange(QBLK_A)
        q_blk = lax.dynamic_slice_in_dim(qa, t0, QBLK_A, axis=1)
        qi_blk = lax.dynamic_slice_in_dim(qi, t0, QBLK_A, axis=1)
        wi_blk = lax.dynamic_slice_in_dim(wi, t0, QBLK_A, axis=1)
        score = index_scores(qi_blk, wi_blk, ki, qpos, kpos)
        _, idx = lax.top_k(score, k_top)
        valid = idx <= qpos[None, :, None]
        return sparse_gqa(q_blk, ka[bidx, idx], va[bidx, idx], valid)

    out = lax.map(block, jnp.arange(T // QBLK_A))
    return jnp.moveaxis(out, 0, 1).reshape(B, T, H_A * HEAD_DIM)


def dsa_sample(qa, ka_new, va_new, qi, ki_new, wi, pool_k, pool_v, pool_kidx, page_table):
    Bd, S = qa.shape[:2]
    k_top = min(TOPK_IDX, (PAST_LEN + S) // 4)
    qpos = PAST_LEN + jnp.arange(S)
    past_kidx = pool_kidx[page_table].reshape(Bd, PAST_LEN, D_IDX)
    score = jnp.concatenate([
        index_scores(qi, wi, past_kidx, qpos, jnp.arange(PAST_LEN)),
        index_scores(qi, wi, ki_new, qpos, PAST_LEN + jnp.arange(S))], axis=-1)
    _, idx = lax.top_k(score, k_top)
    valid = idx <= qpos[None, :, None]
    is_past = (idx < PAST_LEN)[..., None, None]
    past_idx = jnp.minimum(idx, PAST_LEN - 1)
    new_idx = jnp.clip(idx - PAST_LEN, 0, S - 1)
    bidx = jnp.arange(Bd)[:, None, None]
    k_sel = jnp.where(is_past, gather_paged(pool_k, page_table, past_idx), ka_new[bidx, new_idx])
    v_sel = jnp.where(is_past, gather_paged(pool_v, page_table, past_idx), va_new[bidx, new_idx])
    return sparse_gqa(qa, k_sel, v_sel, valid)


def block_attend(q, k_own, v_own, own_mask, k_sel=None, v_sel=None, sel_valid=None):
    B, Q = q.shape[:2]
    scale = HEAD_DIM ** -0.5
    s_own = jnp.einsum('bqhd,bkhd->bqhk', q, k_own).astype(jnp.float32) * scale
    s_own = jnp.where(own_mask[None, :, None, :], s_own, -jnp.inf)
    if k_sel is None:
        p = jax.nn.softmax(s_own, axis=-1).astype(v_own.dtype)
        o = jnp.einsum('bqhk,bkhd->bqhd', p, v_own)
    else:
        n_sel = k_sel.shape[3]
        n_flat = n_sel * MOBA_BLOCK
        s_sel = jnp.einsum('bqhd,bqhnkd->bqhnk', q, k_sel).astype(jnp.float32) * scale
        s_sel = jnp.where(sel_valid[..., None], s_sel, -jnp.inf).reshape(B, Q, H_B, n_flat)
        p = jax.nn.softmax(jnp.concatenate([s_sel, s_own], axis=-1), axis=-1).astype(v_own.dtype)
        p_sel = p[..., :n_flat].reshape(B, Q, H_B, n_sel, MOBA_BLOCK)
        o = (jnp.einsum('bqhk,bkhd->bqhd', p[..., n_flat:], v_own)
             + jnp.einsum('bqhnk,bqhnkd->bqhd', p_sel, v_sel))
    return o.reshape(B, Q, H_B * HEAD_DIM)


def moba_prompt(q, k, v):
    B, T = q.shape[:2]
    nb = -(-T // MOBA_BLOCK)
    pad = nb * MOBA_BLOCK - T
    kp = jnp.pad(k, ((0, 0), (0, pad), (0, 0), (0, 0)))
    vp = jnp.pad(v, ((0, 0), (0, pad), (0, 0), (0, 0)))
    n_cand = nb - 1
    n_sel = min(TOPK_BLK, n_cand)
    kmean = kp.reshape(B, nb, MOBA_BLOCK, H_B, HEAD_DIM)[:, :n_cand].astype(jnp.float32).mean(axis=2)
    bi = jnp.arange(B).reshape(B, 1, 1, 1, 1)
    hi = jnp.arange(H_B).reshape(1, 1, H_B, 1, 1)
    offs = jnp.arange(MOBA_BLOCK)

    def block(i):
        t0 = i * QBLK_B
        j = t0 // MOBA_BLOCK
        qb = lax.dynamic_slice_in_dim(q, t0, QBLK_B, axis=1)
        qpos = t0 + jnp.arange(QBLK_B)
        k_own = lax.dynamic_slice_in_dim(kp, j * MOBA_BLOCK, MOBA_BLOCK, axis=1)
        v_own = lax.dynamic_slice_in_dim(vp, j * MOBA_BLOCK, MOBA_BLOCK, axis=1)
        own_mask = (j * MOBA_BLOCK + offs)[None, :] <= qpos[:, None]
        if n_sel == 0:
            return block_attend(qb, k_own, v_own, own_mask)
        gate = jnp.einsum('bqhd,bnhd->bqhn', qb.astype(jnp.float32), kmean)
        gate = jnp.where(jnp.arange(n_cand) < j, gate, -jnp.inf)
        _, sel = lax.top_k(gate, n_sel)
        valid = sel < j
        pos_sel = sel[..., None] * MOBA_BLOCK + offs
        return block_attend(qb, k_own, v_own, own_mask, kp[bi, pos_sel, hi], vp[bi, pos_sel, hi], valid)

    out = lax.map(block, jnp.arange(T // QBLK_B))
    return jnp.moveaxis(out, 0, 1).reshape(B, T, H_B * HEAD_DIM)


def moba_sample(q, k_new, v_new, pool_k, pool_v, page_table):
    Bd, S = q.shape[:2]
    nbp = PAST_LEN // MOBA_BLOCK
    r = PAST_LEN % MOBA_BLOCK
    n_sel = min(TOPK_BLK, nbp)
    qpos = PAST_LEN + jnp.arange(S)
    qblk = qpos // MOBA_BLOCK
    if r > 0:
        tail_pos = jnp.broadcast_to(PAST_LEN - r + jnp.arange(r), (Bd, r))
        k_own = jnp.concatenate([gather_paged(pool_k, page_table, tail_pos), k_new], axis=1)
        v_own = jnp.concatenate([gather_paged(pool_v, page_table, tail_pos), v_new], axis=1)
    else:
        k_own, v_own = k_new, v_new
    own_pos = PAST_LEN - r + jnp.arange(r + S)
    own_mask = (own_pos[None, :] <= qpos[:, None]) & (own_pos[None, :] // MOBA_BLOCK == qblk[:, None])
    if n_sel == 0:
        return block_attend(q, k_own, v_own, own_mask)
    past_k = pool_k[page_table].reshape(Bd, PAST_LEN, H_B, HEAD_DIM)
    kmean = past_k[:, :nbp * MOBA_BLOCK].reshape(Bd, nbp, MOBA_BLOCK, H_B, HEAD_DIM).astype(jnp.float32).mean(axis=2)
    gate = jnp.einsum('bqhd,bnhd->bqhn', q.astype(jnp.float32), kmean)
    gate = jnp.where(jnp.arange(nbp)[None, None, None, :] < qblk[None, :, None, None], gate, -jnp.inf)
    _, sel = lax.top_k(gate, n_sel)
    valid = sel < qblk[None, :, None, None]
    pos_sel = sel[..., None] * MOBA_BLOCK + jnp.arange(MOBA_BLOCK)
    hi = jnp.arange(H_B).reshape(1, 1, H_B, 1, 1)
    k_sel = gather_paged(pool_k, page_table, pos_sel, hi)
    v_sel = gather_paged(pool_v, page_table, pos_sel, hi)
    return block_attend(q, k_own, v_own, own_mask, k_sel, v_sel, valid)


def merge_branches(o_a, o_b, ga, gb, w_br_a, w_br_b, w_out):
    u = jax.nn.sigmoid(ga) * (o_a @ w_br_a) + jax.nn.sigmoid(gb) * (o_b @ w_br_b)
    return u @ w_out


def moe_dispatch(x, idx, wts, w_gate, w_up, w_down):
    N, D = x.shape
    K = idx.shape[1]
    A = N * K
    flat_e = idx.reshape(A)
    flat_t = jnp.repeat(jnp.arange(N), K)
    flat_w = wts.reshape(A)
    order = jnp.argsort(flat_e)
    se = flat_e[order]
    counts = jnp.bincount(flat_e, length=N_EXPERTS)
    start = jnp.cumsum(counts) - counts
    padded = (counts + MOE_BLK - 1) // MOE_BLK * MOE_BLK
    pend = jnp.cumsum(padded)
    pstart = pend - padded
    dest = pstart[se] + jnp.arange(A) - start[se]
    n_blk = -(-A // MOE_BLK) + N_EXPERTS
    P = n_blk * MOE_BLK
    slot_t = jnp.full((P,), N, dtype=jnp.int32).at[dest].set(flat_t[order])
    slot_w = jnp.zeros((P,), dtype=flat_w.dtype).at[dest].set(flat_w[order])
    blk_e = jnp.minimum(jnp.searchsorted(pend, jnp.arange(n_blk) * MOE_BLK, side='right'), N_EXPERTS - 1)
    x_pad = jnp.concatenate([x, jnp.zeros((1, D), x.dtype)], axis=0)

    def run(args):
        t, e = args
        xb = x_pad[t]
        hdn = jax.nn.silu(xb @ w_gate[e]) * (xb @ w_up[e])
        return hdn @ w_down[e]

    yb = lax.map(run, (slot_t.reshape(n_blk, MOE_BLK), blk_e))
    y = jax.ops.segment_sum(yb.reshape(P, D) * slot_w[:, None], slot_t, num_segments=N + 1)[:N]
    return y.astype(x.dtype)


def hier_moe(h, w_grp, b_grp, w_exp, b_exp, w_e_gate, w_e_up, w_e_down):
    B, T, D = h.shape
    hf = h.reshape(B * T, D)
    N = hf.shape[0]
    gp = jax.nn.softmax((hf @ w_grp).astype(jnp.float32) + b_grp, axis=-1)
    gprob, gsel = lax.top_k(gp, 1)
    elog = ((hf @ w_exp).astype(jnp.float32) + b_exp).reshape(N, N_GROUPS, EXPERTS_PER_GROUP)
    elog_g = jnp.take_along_axis(elog, jnp.broadcast_to(gsel[:, :, None], (N, 1, EXPERTS_PER_GROUP)), axis=1)[:, 0]
    ep = jax.nn.softmax(elog_g, axis=-1)
    eprob, esel = lax.top_k(ep, TOPK_EXPERT)
    wts = gprob * eprob / jnp.sum(eprob, axis=-1, keepdims=True)
    idx = gsel * EXPERTS_PER_GROUP + esel
    return moe_dispatch(hf, idx, wts, w_e_gate, w_e_up, w_e_down).reshape(B, T, D)


def setup_inputs(seed: int = 0) -> dict:
    key = jax.random.key(seed)
    ks = jax.random.split(key, 24)
    n_pages = PAST_LEN // PAGE_SIZE
    n_pool = (DEC_BATCH * n_pages * 5) // 4
    nrm = jax.random.normal
    f32 = jnp.float32
    page_table = jax.random.permutation(ks[7], n_pool)[:DEC_BATCH * n_pages].reshape(DEC_BATCH, n_pages).astype(jnp.int32)
    return {
        'x_prompt': nrm(ks[0], (BATCH, SEQ, D_MODEL), f32),
        'x_sample': nrm(ks[1], (DEC_BATCH, DEC_SEQ, D_MODEL), f32),
        'cache_k_a': nrm(ks[2], (DEPTH, n_pool, PAGE_SIZE, KV_A, HEAD_DIM), f32),
        'cache_v_a': nrm(ks[3], (DEPTH, n_pool, PAGE_SIZE, KV_A, HEAD_DIM), f32),
        'cache_k_idx': nrm(ks[4], (DEPTH, n_pool, PAGE_SIZE, D_IDX), f32),
        'cache_k_b': nrm(ks[5], (DEPTH, n_pool, PAGE_SIZE, H_B, HEAD_DIM), f32),
        'cache_v_b': nrm(ks[6], (DEPTH, n_pool, PAGE_SIZE, H_B, HEAD_DIM), f32),
        'page_table': page_table,
        'w_in': nrm(ks[8], (DEPTH, D_MODEL, D_IN), f32) * D_MODEL ** -0.5,
        'w_br_a': nrm(ks[9], (DEPTH, H_A * HEAD_DIM, D_MODEL), f32) * (H_A * HEAD_DIM) ** -0.5,
        'w_br_b': nrm(ks[10], (DEPTH, H_B * HEAD_DIM, D_MODEL), f32) * (H_B * HEAD_DIM) ** -0.5,
        'w_out': nrm(ks[11], (DEPTH, D_MODEL, D_MODEL), f32) * D_MODEL ** -0.5,
        'g_mix': 1.0 + 0.01 * nrm(ks[12], (DEPTH, D_MODEL), f32),
        'g_ffn': 1.0 + 0.01 * nrm(ks[13], (DEPTH, D_MODEL), f32),
        'w_grp': nrm(ks[14], (DEPTH, D_MODEL, N_GROUPS), f32) * D_MODEL ** -0.5,
        'b_grp': 0.01 * nrm(ks[15], (DEPTH, N_GROUPS), f32),
        'w_exp': nrm(ks[16], (DEPTH, D_MODEL, N_EXPERTS), f32) * D_MODEL ** -0.5,
        'b_exp': 0.01 * nrm(ks[17], (DEPTH, N_EXPERTS), f32),
        'w_e_gate': nrm(ks[18], (DEPTH, N_EXPERTS, D_MODEL, D_EXPERT), f32) * D_MODEL ** -0.5,
        'w_e_up': nrm(ks[19], (DEPTH, N_EXPERTS, D_MODEL, D_EXPERT), f32) * D_MODEL ** -0.5,
        'w_e_down': nrm(ks[20], (DEPTH, N_EXPERTS, D_EXPERT, D_MODEL), f32) * D_EXPERT ** -0.5,
        'g_final': 1.0 + 0.01 * nrm(ks[21], (D_MODEL,), f32),
    }


def reference(x_prompt, x_sample, cache_k_a, cache_v_a, cache_k_idx, cache_k_b, cache_v_b, page_table,
              w_in, w_br_a, w_br_b, w_out, g_mix, g_ffn, w_grp, b_grp, w_exp, b_exp,
              w_e_gate, w_e_up, w_e_down, g_final):
    xp, xs = x_prompt, x_sample
    pos_p = jnp.arange(xp.shape[1])
    pos_s = PAST_LEN + jnp.arange(xs.shape[1])
    rows_p, rows_s = [], []
    for l in range(DEPTH):
        ffn_w = (w_grp[l], b_grp[l], w_exp[l], b_exp[l], w_e_gate[l], w_e_up[l], w_e_down[l])
        qa, ka, va, qi, ki, wi, qb, kb, vb, ga, gb = mixer_project(rmsnorm(xp, g_mix[l]), w_in[l], pos_p)
        o_a = dsa_prompt(qa, ka, va, qi, ki, wi)
        o_b = moba_prompt(qb, kb, vb)
        xp = xp + merge_branches(o_a, o_b, ga, gb, w_br_a[l], w_br_b[l], w_out[l])
        xp = xp + hier_moe(rmsnorm(xp, g_ffn[l]), *ffn_w)
        rows_p.append((ka, va, ki, kb, vb))
        qa, ka, va, qi, ki, wi, qb, kb, vb, ga, gb = mixer_project(rmsnorm(xs, g_mix[l]), w_in[l], pos_s)
        o_a = dsa_sample(qa, ka, va, qi, ki, wi, cache_k_a[l], cache_v_a[l], cache_k_idx[l], page_table)
        o_b = moba_sample(qb, kb, vb, cache_k_b[l], cache_v_b[l], page_table)
        xs = xs + merge_branches(o_a, o_b, ga, gb, w_br_a[l], w_br_b[l], w_out[l])
        xs = xs + hier_moe(rmsnorm(xs, g_ffn[l]), *ffn_w)
        rows_s.append((ka, va, ki, kb, vb))
    y_prompt = rmsnorm(xp, g_final)
    y_sample = rmsnorm(xs, g_final)
    k_a_p, v_a_p, k_idx_p, k_b_p, v_b_p = [jnp.stack([r[i] for r in rows_p]) for i in range(5)]
    k_a_s, v_a_s, k_idx_s, k_b_s, v_b_s = [jnp.stack([r[i] for r in rows_s]) for i in range(5)]
    return (y_prompt, y_sample, k_a_p, v_a_p, k_idx_p, k_b_p, v_b_p, k_a_s, v_a_s, k_idx_s, k_b_s, v_b_s)
```

```python
import functools

import jax
import jax.numpy as jnp
from jax import lax
from jax.experimental import pallas as pl
from jax.experimental.pallas import tpu as pltpu

F32, BF16, I32 = jnp.float32, jnp.bfloat16, jnp.int32

HEAD_DIM = 64
ROT_DIM = HEAD_DIM // 4
ROPE_THETA = 500000.0
H_A = 8
KV_A = 2
H_IDX = 8
D_IDX = 64
TOPK_IDX = 256
H_B = 8
MOBA_BLOCK = 256
TOPK_BLK = 3
N_GROUPS = 4
EXPERTS_PER_GROUP = 8
N_EXPERTS = N_GROUPS * EXPERTS_PER_GROUP
D_EXPERT = 256
RMS_EPS = 1e-6
PAGE_SIZE = 128

LANE = 128
SUBLANE = 8
VMEM_LIMIT = 56 * 1024 * 1024

TM_PROJ = 512
DSA_Q = 128
DSA_KC = TM_PROJ
TM_MERGE = 512
TM_MOE = 1024
PAGES_SCORES = 32
PAGES_ATTN = 32
PAGES_MOBA = 16

NEG = -1e30
INT_MIN = -(2 ** 31)
KEY_NEG_INF = -2139095041
N_HEAD_SLABS = 8
Q_EXP = N_HEAD_SLABS * LANE
WKV = H_B * HEAD_DIM
SOFTMAX_SCALE = HEAD_DIM ** -0.5
assert SOFTMAX_SCALE == 0.125

_NT = (((1,), (1,)), ((), ()))
_HI = lax.Precision.HIGHEST


def _dot(a, b, precision=None):
    return jnp.dot(a, b, preferred_element_type=F32, precision=precision)


def _dot_nt(a, b, precision=None):
    return lax.dot_general(a, b, _NT, preferred_element_type=F32, precision=precision)


def _params(*sem):
    return pltpu.CompilerParams(dimension_semantics=sem, vmem_limit_bytes=VMEM_LIMIT)


def _rms_scale(x):
    return lax.rsqrt(jnp.mean(x * x, axis=-1, keepdims=True) + RMS_EPS)


def _rope(y, c, s1, s2):
    half = ROT_DIM // 2
    return y * c + pltpu.roll(y, LANE - half, 1) * s1 + pltpu.roll(y, half, 1) * s2


def _rope_t(y, cos_t, sin_t):
    half = ROT_DIM // 2
    parts = []
    for r in range(0, y.shape[0], HEAD_DIM):
        x1, x2 = y[r:r + half], y[r + half:r + ROT_DIM]
        parts += [x1 * cos_t - x2 * sin_t, x2 * cos_t + x1 * sin_t, y[r + ROT_DIM:r + HEAD_DIM]]
    return jnp.concatenate(parts, axis=0)


def _proj_prompt_kernel(x_ref, g_ref, wt_ref, wr_ref, cos_t_ref, sin_t_ref, c_ref, s1_ref, s2_ref,
                        qa_t_ref, qi_t_ref, qb_t_ref, ka_t_ref, va_t_ref, ki_t_ref, kb_t_ref, vb_t_ref, wi_t_ref,
                        kab_ref, kib_ref, kbb_ref, va_tc_ref, vb_tc_ref, kmean_ref, ga_ref, gb_ref):
    tm, d_model = x_ref.shape
    x = x_ref[...]
    h = ((x * _rms_scale(x)) * g_ref[...]).astype(BF16)
    cos_t, sin_t = cos_t_ref[...], sin_t_ref[...]
    c, s1, s2 = c_ref[...], s1_ref[...], s2_ref[...]

    off = [0]

    def proj_t(rows):
        y = _dot_nt(wt_ref[off[0]:off[0] + rows, :], h)
        off[0] += rows
        return y

    half_q = Q_EXP // 2
    for r0 in (0, half_q):
        qa_t_ref[0, r0:r0 + half_q, :] = _rope_t(proj_t(half_q), cos_t, sin_t).astype(BF16)
    qi_t_ref[0] = _rope_t(proj_t(H_IDX * D_IDX), cos_t, sin_t).astype(BF16)
    for r0 in (0, half_q):
        qb_t_ref[0, r0:r0 + half_q, :] = _rope_t(proj_t(half_q), cos_t, sin_t).astype(BF16)
    ka_t_ref[0] = _rope_t(proj_t(KV_A * HEAD_DIM), cos_t, sin_t)
    va_t = proj_t(KV_A * HEAD_DIM)
    va_t_ref[0] = va_t
    va_tc_ref[0] = va_t.astype(BF16)
    ki_t_ref[0] = _rope_t(proj_t(D_IDX), cos_t, sin_t)
    kb_t_ref[0] = _rope_t(proj_t(WKV), cos_t, sin_t)
    vb_t = proj_t(WKV)
    vb_t_ref[0] = vb_t
    for n in range(tm // MOBA_BLOCK):
        vb_tc_ref[n] = vb_t[:, n * MOBA_BLOCK:(n + 1) * MOBA_BLOCK].astype(BF16)
    wi_t_ref[0] = proj_t(2 * H_IDX)[:H_IDX]

    col = [0]

    def proj(width):
        y = _dot(h, wr_ref[:, col[0]:col[0] + width])
        col[0] += width
        return y

    y = proj(2 * LANE)
    kab_ref[...] = _rope(y[:, :LANE], c, s1, s2).astype(BF16)
    kib_ref[...] = _rope(y[:, LANE:], c, s1, s2)[:, :D_IDX].astype(BF16)
    y = proj(WKV)
    kb = jnp.concatenate([_rope(y[:, j * LANE:(j + 1) * LANE], c, s1, s2) for j in range(WKV // LANE)], axis=1)
    kbb_ref[...] = kb.astype(BF16)
    nblk = tm // MOBA_BLOCK
    kmean_ref[0] = jnp.sum(kb.reshape(nblk, MOBA_BLOCK, WKV), axis=1) * (1.0 / MOBA_BLOCK)
    ga_ref[...] = proj(d_model)
    gb_ref[...] = proj(d_model)


def _project_prompt(x2d, g, w_t, w_r, tabs_t, tabs, bsz, t):
    n, d = x2d.shape
    tm = TM_PROJ
    tpb = t // tm
    row = lambda i: (i, 0)
    const = lambda i: (0, 0)
    bt = lambda i: (i // tpb, 0, i % tpb)
    chunk = lambda i: (i, 0, 0)
    once = dict(pipeline_mode=pl.Buffered(1))
    in_specs = [pl.BlockSpec((tm, d), row), pl.BlockSpec((1, d), const),
                pl.BlockSpec(w_t.shape, const, **once), pl.BlockSpec(w_r.shape, const, **once)]
    in_specs += [pl.BlockSpec((ROT_DIM // 2, tm), lambda i: (0, i % tpb))] * 2
    in_specs += [pl.BlockSpec((tm, LANE), lambda i: (i % tpb, 0))] * 3
    t_feats = [(Q_EXP, BF16), (H_IDX * D_IDX, BF16), (Q_EXP, BF16), (KV_A * HEAD_DIM, F32), (KV_A * HEAD_DIM, F32),
               (D_IDX, F32), (WKV, F32), (WKV, F32), (H_IDX, F32)]
    out_shape = [jax.ShapeDtypeStruct((bsz, f, t), dt) for f, dt in t_feats]
    out_specs = [pl.BlockSpec((1, f, tm), bt) for f, _ in t_feats]
    r_feats = [(KV_A * HEAD_DIM, BF16), (D_IDX, BF16), (WKV, BF16)]
    out_shape += [jax.ShapeDtypeStruct((n, f), dt) for f, dt in r_feats]
    out_specs += [pl.BlockSpec((tm, f), row) for f, _ in r_feats]
    nblk = tm // MOBA_BLOCK
    out_shape += [jax.ShapeDtypeStruct((n // tm, KV_A * HEAD_DIM, tm), BF16),
                  jax.ShapeDtypeStruct((n // MOBA_BLOCK, WKV, MOBA_BLOCK), BF16),
                  jax.ShapeDtypeStruct((n // tm, nblk, WKV), F32),
                  jax.ShapeDtypeStruct((n, d), F32), jax.ShapeDtypeStruct((n, d), F32)]
    out_specs += [pl.BlockSpec((1, KV_A * HEAD_DIM, tm), chunk), pl.BlockSpec((nblk, WKV, MOBA_BLOCK), chunk),
                  pl.BlockSpec((1, nblk, WKV), chunk), pl.BlockSpec((tm, d), row), pl.BlockSpec((tm, d), row)]
    return pl.pallas_call(
        _proj_prompt_kernel, grid=(n // tm,), in_specs=in_specs, out_specs=out_specs, out_shape=out_shape,
        compiler_params=_params("parallel"), name="proj_prompt",
    )(x2d, g.reshape(1, d), w_t, w_r, *tabs_t, *tabs)


def _proj_sample_kernel(x_ref, g_ref, w_ref, c_ref, s1_ref, s2_ref,
                        qa_ref, qi_ref, qb_ref, ka_ref, va_ref, kiwi_ref, kb_ref, vb_ref, ga_ref, gb_ref):
    tm, d_model = x_ref.shape
    x = x_ref[...]
    h = ((x * _rms_scale(x)) * g_ref[...]).astype(BF16)
    c, s1, s2 = c_ref[...], s1_ref[...], s2_ref[...]
    off = [0]

    def proj(width):
        y = _dot(h, w_ref[:, off[0]:off[0] + width])
        off[0] += width
        return y

    def rope_slabs(y):
        return jnp.concatenate([_rope(y[:, j * LANE:(j + 1) * LANE], c, s1, s2) for j in range(y.shape[1] // LANE)],
                               axis=1)

    for q_ref in (qa_ref, qi_ref, qb_ref):
        q_ref[...] = rope_slabs(proj(Q_EXP)).astype(BF16)
    y = proj(3 * LANE)
    ka_ref[...] = _rope(y[:, :LANE], c, s1, s2)
    va_ref[...] = y[:, LANE:2 * LANE]
    is_ki = lax.broadcasted_iota(I32, (tm, LANE), 1) < D_IDX
    kiwi_ref[...] = _rope(y[:, 2 * LANE:], jnp.where(is_ki, c, 1.0), jnp.where(is_ki, s1, 0.0),
                          jnp.where(is_ki, s2, 0.0))
    kb_ref[...] = rope_slabs(proj(WKV))
    vb_ref[...] = proj(WKV)
    ga_ref[...] = proj(d_model)
    gb_ref[...] = proj(d_model)


def _project_sample(x2d, g, w_all, tabs):
    n, d = x2d.shape
    const = lambda i: (0, 0)
    widths = [(Q_EXP, BF16)] * 3 + [(LANE, F32)] * 3 + [(WKV, F32)] * 2 + [(d, F32)] * 2
    return pl.pallas_call(
        _proj_sample_kernel, grid=(1,),
        in_specs=[pl.BlockSpec((n, d), const), pl.BlockSpec((1, d), const),
                  pl.BlockSpec(w_all.shape, const, pipeline_mode=pl.Buffered(1))] + [pl.BlockSpec((n, LANE), const)] * 3,
        out_specs=[pl.BlockSpec((n, w), const) for w, _ in widths],
        out_shape=[jax.ShapeDtypeStruct((n, w), dt) for w, dt in widths],
        compiler_params=_params("arbitrary"), name="proj_sample",
    )(x2d, g.reshape(1, d), w_all, *tabs)


def _key_to_float(key):
    return pltpu.bitcast(key ^ ((key >> 31) & 0x7FFFFFFF), F32)


def _select_bias(score_ref, bias_ref, nc, k_top, n_keys_log2, key_axis):
    _, d0, d1 = score_ref.shape
    kc = (d0, d1)[key_axis]
    kidx0 = lax.broadcasted_iota(I32, (d0, d1), key_axis)
    k_top = float(k_top)
    qshape = (1, d1) if key_axis == 0 else (d0, 1)

    def fold(f):
        if key_axis == 0:
            return jnp.sum(f.reshape(SUBLANE, d0 // SUBLANE, d1), axis=0)
        part = f[:, :LANE]
        for j in range(1, d1 // LANE):
            part = part + f[:, j * LANE:(j + 1) * LANE]
        return part

    def count(pred):
        def body(c, cnt):
            return cnt + fold(jnp.where(pred(score_ref[c], c), 1.0, 0.0))
        zero = jnp.zeros((d0 // SUBLANE, d1) if key_axis == 0 else (d0, LANE), F32)
        return jnp.sum(lax.fori_loop(0, nc, body, zero), axis=key_axis, keepdims=True)

    nonneg = count(lambda s, c: s >= 0.0)
    v0 = jnp.where(nonneg >= k_top, 0, INT_MIN).astype(I32)

    def bit_body(b, v):
        cand = v | jnp.left_shift(jnp.int32(1), 30 - b)
        cf = _key_to_float(cand)
        return jnp.where(count(lambda s, c: s >= cf) >= k_top, cand, v)

    v = lax.fori_loop(0, 31, bit_body, v0)
    thr = _key_to_float(jnp.maximum(v, KEY_NEG_INF))

    at_least = count(lambda s, c: s >= thr)
    tied = (at_least > k_top) & (thr > -jnp.inf)
    all_keys = jnp.full(qshape, 2 ** n_keys_log2 - 1, I32)

    def tie_search():
        need = k_top - count(lambda s, c: s > thr)

        def tie_body(b, u):
            cand = u | jnp.left_shift(jnp.int32(1), n_keys_log2 - 1 - b)
            below = count(lambda s, c: (s == thr) & (c * kc + kidx0 < cand))
            return jnp.where(below < need, cand, u)

        return lax.fori_loop(0, n_keys_log2, tie_body, jnp.zeros(qshape, I32))

    u = lax.cond(jnp.max(jnp.where(tied, 1.0, 0.0)) > 0.0, tie_search, lambda: all_keys)

    def bias_body(c, carry):
        s = score_ref[c]
        sel = (s > -jnp.inf) & ((s > thr) | ((s == thr) & (c * kc + kidx0 <= u)))
        bias_ref[c] = jnp.where(sel, 0.0, NEG)
        return carry

    lax.fori_loop(0, nc, bias_body, 0)


def _dsa_prompt_kernel(qa_t_ref, qi_t_ref, wi_t_ref, kib_ref, kab_ref, va_tc_ref, o_ref, score_ref, bias_ref, *, k_top):
    nchunks, kc, q = score_ref.shape
    i = pl.program_id(1)
    nc = (i * q + q - 1) // kc + 1
    key0 = lax.broadcasted_iota(I32, (kc, q), 0)
    qpos = i * q + lax.broadcasted_iota(I32, (kc, q), 1)
    w = wi_t_ref[0] * (H_IDX ** -0.5 * D_IDX ** -0.5)
    qi_all = jnp.concatenate([qi_t_ref[0, h * D_IDX:(h + 1) * D_IDX, :] for h in range(H_IDX)], axis=1)

    def score_body(c, carry):
        d = _dot(kib_ref[pl.ds(pl.multiple_of(c * kc, kc), kc), :], qi_all)
        acc = jnp.zeros((kc, q), F32)
        for h in range(H_IDX):
            acc = acc + jnp.maximum(d[:, h * q:(h + 1) * q], 0.0) * w[h:h + 1, :]
        score_ref[c] = jnp.where(c * kc + key0 <= qpos, acc, -jnp.inf)
        return carry

    lax.fori_loop(0, nc, score_body, 0)
    _select_bias(score_ref, bias_ref, nc, k_top, (nchunks * kc - 1).bit_length(), key_axis=0)

    qa_all = jnp.concatenate([qa_t_ref[0, h * LANE:(h + 1) * LANE, :] for h in range(H_A)], axis=1)
    qa_all = qa_all * SOFTMAX_SCALE

    def att_body(c, carry):
        m, l, acc = carry
        s = _dot(kab_ref[pl.ds(pl.multiple_of(c * kc, kc), kc), :], qa_all)
        b = bias_ref[c]
        s = jnp.concatenate([s[:, h * q:(h + 1) * q] + b for h in range(H_A)], axis=1)
        m_new = jnp.maximum(m, jnp.max(s, axis=0, keepdims=True))
        alpha = jnp.exp(m - m_new)
        p = jnp.exp(s - m_new)
        l = alpha * l + jnp.sum(p, axis=0, keepdims=True)
        acc = alpha * acc + _dot(va_tc_ref[c], p.astype(BF16))
        return m_new, l, acc

    init = (jnp.full((1, H_A * q), NEG, F32), jnp.zeros((1, H_A * q), F32), jnp.zeros((KV_A * HEAD_DIM, H_A * q), F32))
    _, l, acc = lax.fori_loop(0, nc, att_body, init)
    o = acc / l
    heads_per_group = H_A // KV_A
    for j in range(H_A // 2):
        g0 = ((2 * j) // heads_per_group) * HEAD_DIM
        pair = jnp.concatenate([o[g0:g0 + HEAD_DIM, (2 * j) * q:(2 * j + 1) * q],
                                o[g0:g0 + HEAD_DIM, (2 * j + 1) * q:(2 * j + 2) * q]], axis=0)
        o_ref[:, j * LANE:(j + 1) * LANE] = pair.T.astype(BF16)


def _dsa_prompt(qa_t, qi_t, wi_t, kib, kab, va_tc, bsz, t):
    q, kc = DSA_Q, DSA_KC
    nq, nchunks = t // q, t // kc
    tile_t = lambda b, i: (b, 0, i)
    return pl.pallas_call(
        functools.partial(_dsa_prompt_kernel, k_top=min(TOPK_IDX, t // 4)),
        grid=(bsz, nq),
        in_specs=[pl.BlockSpec((1, Q_EXP, q), tile_t), pl.BlockSpec((1, H_IDX * D_IDX, q), tile_t),
                  pl.BlockSpec((1, H_IDX, q), tile_t),
                  pl.BlockSpec((t, D_IDX), lambda b, i: (b, 0)), pl.BlockSpec((t, KV_A * HEAD_DIM), lambda b, i: (b, 0)),
                  pl.BlockSpec((nchunks, KV_A * HEAD_DIM, kc), lambda b, i: (b, 0, 0))],
        out_specs=pl.BlockSpec((q, H_A * HEAD_DIM), lambda b, i: (b * nq + i, 0)),
        out_shape=jax.ShapeDtypeStruct((bsz * t, H_A * HEAD_DIM), BF16),
        scratch_shapes=[pltpu.VMEM((nchunks, kc, q), F32), pltpu.VMEM((nchunks, kc, q), F32)],
        compiler_params=_params("parallel", "arbitrary"), name="dsa_prompt",
    )(qa_t, qi_t, wi_t, kib, kab, va_tc)


def _top_blocks(gate, valid, idxf, n_sel, axis):
    gate = jnp.where(valid, gate, -jnp.inf)
    sel = jnp.zeros(gate.shape, jnp.bool_)
    for _ in range(n_sel):
        mx = jnp.max(gate, axis=axis, keepdims=True)
        first = jnp.min(jnp.where(gate == mx, idxf, 1e9), axis=axis, keepdims=True)
        pick = idxf == first
        sel = sel | (pick & valid)
        gate = jnp.where(pick, -jnp.inf, gate)
    return sel


def _moba_prompt_kernel(qb_t_ref, km_ref, kb_ref, vb_tc_ref, o_ref, *, n_sel):
    qb = o_ref.shape[0]
    nb = km_ref.shape[0]
    j = pl.program_id(1)
    r = 2 * qb
    nrow = lax.broadcasted_iota(I32, (nb, r), 0)
    nrowf = nrow.astype(F32)
    qc = lax.broadcasted_iota(I32, (qb, r), 1)
    own_mask = lax.broadcasted_iota(I32, (qb, r), 0) <= jnp.where(qc >= qb, qc - qb, qc)
    own = pl.multiple_of(j * qb, qb)

    pairs = range(H_B // 2)
    sls = [slice(p * LANE, (p + 1) * LANE) for p in pairs]
    qes, selbiases, init = [], [], []
    for p in pairs:
        qe = jnp.concatenate([qb_t_ref[0, (2 * p) * LANE:(2 * p + 1) * LANE, :],
                              qb_t_ref[0, (2 * p + 1) * LANE:(2 * p + 2) * LANE, :]], axis=1)
        gate = _dot(km_ref[:, sls[p]], qe.astype(F32), precision=_HI)
        sel = _top_blocks(gate, nrow < j, nrowf, n_sel, axis=0)
        selbiases.append(jnp.where(sel, 0.0, NEG))
        qe = qe * SOFTMAX_SCALE
        qes.append(qe)

        s = jnp.where(own_mask, _dot(kb_ref[pl.ds(own, qb), sls[p]], qe), NEG)
        m = jnp.max(s, axis=0, keepdims=True)
        pr = jnp.exp(s - m)
        init.append((m, jnp.sum(pr, axis=0, keepdims=True), _dot(vb_tc_ref[j, sls[p], :], pr.astype(BF16))))

    def body(n, carry):
        start = pl.multiple_of(n * MOBA_BLOCK, MOBA_BLOCK)
        out = []
        for p in pairs:
            m, l, acc = carry[p]
            rowb = jnp.sum(jnp.where(nrow == n, selbiases[p], 0.0), axis=0, keepdims=True)
            s = _dot(kb_ref[pl.ds(start, MOBA_BLOCK), sls[p]], qes[p]) + rowb
            m_new = jnp.maximum(m, jnp.max(s, axis=0, keepdims=True))
            alpha = jnp.exp(m - m_new)
            pr = jnp.exp(s - m_new)
            l = alpha * l + jnp.sum(pr, axis=0, keepdims=True)
            acc = alpha * acc + _dot(vb_tc_ref[n, sls[p], :], pr.astype(BF16))
            out.append((m_new, l, acc))
        return tuple(out)

    final = lax.fori_loop(0, j, body, tuple(init))
    for p in pairs:
        _, l, acc = final[p]
        o = acc / l
        pair = jnp.concatenate([o[:HEAD_DIM, :qb], o[HEAD_DIM:, qb:]], axis=0)
        o_ref[:, sls[p]] = pair.T.astype(BF16)


def _moba_prompt(qb_t, kmean, kbb, vb_tc, bsz, t):
    nb = t // MOBA_BLOCK
    return pl.pallas_call(
        functools.partial(_moba_prompt_kernel, n_sel=min(TOPK_BLK, nb - 1)),
        grid=(bsz, nb),
        in_specs=[pl.BlockSpec((1, Q_EXP, MOBA_BLOCK), lambda b, i: (b, 0, i)), pl.BlockSpec((nb, WKV), lambda b, i: (b, 0)),
                  pl.BlockSpec((t, WKV), lambda b, i: (b, 0)),
                  pl.BlockSpec((nb, WKV, MOBA_BLOCK), lambda b, i: (b, 0, 0))],
        out_specs=pl.BlockSpec((MOBA_BLOCK, WKV), lambda b, i: (b * nb + i, 0)),
        out_shape=jax.ShapeDtypeStruct((bsz * t, WKV), BF16),
        compiler_params=_params("parallel", "arbitrary"), name="moba_prompt",
    )(qb_t, kmean, kbb, vb_tc)


def _merge_kernel(oa_ref, ob_ref, ga_ref, gb_ref, x_ref, wa_ref, wb_ref, wo_ref, gf_ref, wr_ref, br_ref,
                  x1_ref, h2_ref, cw_ref):
    tm = x_ref.shape[0]
    u = jax.nn.sigmoid(ga_ref[...]) * _dot(oa_ref[...], wa_ref[...]) \
        + jax.nn.sigmoid(gb_ref[...]) * _dot(ob_ref[...], wb_ref[...])
    x1 = x_ref[...] + _dot(u.astype(BF16), wo_ref[...])
    x1_ref[...] = x1
    h2 = (x1 * _rms_scale(x1)) * gf_ref[...]
    h2_ref[...] = h2.astype(BF16)

    rl = _dot(h2, wr_ref[...], precision=_HI) + br_ref[...]
    lanef = lax.broadcasted_iota(I32, (tm, LANE), 1).astype(F32)

    def masked_softmax(mask):
        z = jnp.where(mask, rl, -jnp.inf)
        e = jnp.exp(z - jnp.max(z, axis=1, keepdims=True))
        return e / jnp.sum(e, axis=1, keepdims=True)

    def first_lane(cond):
        return jnp.min(jnp.where(cond, lanef, 1e9), axis=1, keepdims=True)

    gmask = lanef < N_GROUPS
    gp = masked_softmax(gmask)
    gprob = jnp.max(gp, axis=1, keepdims=True)
    gsel = first_lane((gp == gprob) & gmask)
    lo = N_GROUPS + EXPERTS_PER_GROUP * gsel
    emask = (lanef >= lo) & (lanef < lo + EXPERTS_PER_GROUP)
    ep = jnp.where(emask, masked_softmax(emask), -1.0)
    p1 = jnp.max(ep, axis=1, keepdims=True)
    i1 = first_lane(ep == p1)
    ep = jnp.where(lanef == i1, -1.0, ep)
    p2 = jnp.max(ep, axis=1, keepdims=True)
    i2 = first_lane(ep == p2)
    den = p1 + p2
    cw_ref[...] = jnp.where(lanef == i1, gprob * p1 / den, 0.0) + jnp.where(lanef == i2, gprob * p2 / den, 0.0)


def _merge(oa, ob, ga, gb, x2d, wa, wb, wo, g_ffn, w_router, b_router, tm):
    n, d = x2d.shape
    row = lambda i: (i, 0)
    const = lambda i: (0, 0)
    ha = oa.shape[1]
    return pl.pallas_call(
        _merge_kernel, grid=(n // tm,),
        in_specs=[pl.BlockSpec((tm, ha), row), pl.BlockSpec((tm, ha), row), pl.BlockSpec((tm, d), row),
                  pl.BlockSpec((tm, d), row), pl.BlockSpec((tm, d), row), pl.BlockSpec((ha, d), const),
                  pl.BlockSpec((ha, d), const), pl.BlockSpec((d, d), const), pl.BlockSpec((1, d), const),
                  pl.BlockSpec((d, LANE), const), pl.BlockSpec((1, LANE), const)],
        out_specs=[pl.BlockSpec((tm, d), row), pl.BlockSpec((tm, d), row), pl.BlockSpec((tm, LANE), row)],
        out_shape=[jax.ShapeDtypeStruct((n, d), F32), jax.ShapeDtypeStruct((n, d), BF16),
                   jax.ShapeDtypeStruct((n, LANE), F32)],
        compiler_params=_params("parallel"), name="merge",
    )(oa, ob, ga, gb, x2d, wa, wb, wo, g_ffn.reshape(1, d), w_router, b_router)


def _moe_kernel(h_ref, cw_ref, x_ref, wg_ref, wu_ref, wd_ref, gfin_ref, y_ref, acc_ref):
    tm = h_ref.shape[0]
    e = pl.program_id(1)

    @pl.when(e == 0)
    def _():
        acc_ref[...] = jnp.zeros_like(acc_ref)

    h = h_ref[...]
    a = _dot(h, wg_ref[0])
    hdn = (a * jax.nn.sigmoid(a)) * _dot(h, wu_ref[0])
    y = _dot(hdn.astype(BF16), wd_ref[0])
    lane = lax.broadcasted_iota(I32, (tm, LANE), 1)
    col = jnp.sum(jnp.where(lane == e + N_GROUPS, cw_ref[...], 0.0), axis=1, keepdims=True)
    acc_ref[...] += y * col

    @pl.when(e == pl.num_programs(1) - 1)
    def _():
        out = x_ref[...] + acc_ref[...]
        y_ref[...] = (out * _rms_scale(out)) * gfin_ref[...]


def _moe(h2, cw, x1, wg, wu, wd, g_final, tm):
    n, d = x1.shape
    ne, _, de = wg.shape
    row = lambda i, e: (i, 0)
    return pl.pallas_call(
        _moe_kernel, grid=(n // tm, ne),
        in_specs=[pl.BlockSpec((tm, d), row), pl.BlockSpec((tm, LANE), row), pl.BlockSpec((tm, d), row),
                  pl.BlockSpec((1, d, de), lambda i, e: (e, 0, 0)), pl.BlockSpec((1, d, de), lambda i, e: (e, 0, 0)),
                  pl.BlockSpec((1, de, d), lambda i, e: (e, 0, 0)), pl.BlockSpec((1, d), lambda i, e: (0, 0))],
        out_specs=pl.BlockSpec((tm, d), row),
        out_shape=jax.ShapeDtypeStruct((n, d), F32),
        scratch_shapes=[pltpu.VMEM((tm, d), F32)],
        compiler_params=_params("parallel", "arbitrary"), name="moe",
    )(h2, cw, x1, wg, wu, wd, g_final.reshape(1, d))


def _page_specs(shape_tail, pages_per_step):
    zeros = (0,) * len(shape_tail)
    return [pl.BlockSpec((None,) + shape_tail,
                         functools.partial(lambda b, j, pt, t: (pt[b, j * pages_per_step + t],) + zeros, t=t))
            for t in range(pages_per_step)]


def _dsa_sample_score_kernel(pt_ref, qi_ref, w_ref, kin_ref, *refs, pages):
    page_refs, (s_ref, snew_ref) = refs[:pages], refs[pages:]
    del pt_ref
    j = pl.program_id(1)
    rows = qi_ref.shape[1]
    s_len = rows // H_IDX
    q = qi_ref[0][:, :D_IDX]
    w = w_ref[0] * (H_IDX ** -0.5 * D_IDX ** -0.5)

    def scores(d):
        d = jnp.maximum(d, 0.0) * w
        return jnp.sum(d.reshape(s_len, H_IDX, PAGE_SIZE), axis=1)

    for t in range(pages):
        s_ref[0, :, t * PAGE_SIZE:(t + 1) * PAGE_SIZE] = scores(_dot(q, page_refs[t][...].astype(BF16)))

    @pl.when(j == pl.num_programs(1) - 1)
    def _():
        sn = scores(_dot_nt(q, kin_ref[0][:, :D_IDX].astype(BF16)))
        si = lax.broadcasted_iota(I32, (s_len, PAGE_SIZE), 0)
        ki = lax.broadcasted_iota(I32, (s_len, PAGE_SIZE), 1)
        snew_ref[0] = jnp.where(ki <= si, sn, -jnp.inf)


def _dsa_sample_scores(page_table, qi_s, wi_s, kin_pad, pool_kidx, pages):
    bd, n_pages = page_table.shape
    rows = qi_s.shape[1]
    s_len = rows // H_IDX
    per_b = lambda b, j, pt: (b, 0, 0)
    grid_spec = pltpu.PrefetchScalarGridSpec(
        num_scalar_prefetch=1, grid=(bd, n_pages // pages),
        in_specs=[pl.BlockSpec((1, rows, LANE), per_b), pl.BlockSpec((1, rows, 1), per_b),
                  pl.BlockSpec((1, PAGE_SIZE, LANE), per_b)] + _page_specs((D_IDX, PAGE_SIZE), pages),
        out_specs=[pl.BlockSpec((1, s_len, pages * PAGE_SIZE), lambda b, j, pt: (b, 0, j)),
                   pl.BlockSpec((1, s_len, PAGE_SIZE), per_b)])
    return pl.pallas_call(
        functools.partial(_dsa_sample_score_kernel, pages=pages), grid_spec=grid_spec,
        out_shape=[jax.ShapeDtypeStruct((bd, s_len, n_pages * PAGE_SIZE), F32),
                   jax.ShapeDtypeStruct((bd, s_len, PAGE_SIZE), F32)],
        compiler_params=_params("parallel", "arbitrary"), name="dsa_sample_scores",
    )(page_table, qi_s, wi_s, kin_pad, *([pool_kidx] * pages))


def _dsa_sample_select_kernel(sp_ref, sn_ref, bp_ref, bn_ref, score_ref, bias_ref, *, k_top):
    nch, q, kc = score_ref.shape
    npast = nch - 1

    def load(c, carry):
        score_ref[c] = sp_ref[:, pl.ds(pl.multiple_of(c * kc, kc), kc)]
        return carry

    lax.fori_loop(0, npast, load, 0)
    score_ref[npast] = jnp.concatenate([sn_ref[...], jnp.full((q, kc - LANE), -jnp.inf, F32)], axis=1)
    _select_bias(score_ref, bias_ref, nch, k_top, (nch * kc - 1).bit_length(), key_axis=1)

    def store(c, carry):
        bp_ref[:, pl.ds(pl.multiple_of(c * kc, kc), kc)] = bias_ref[c]
        return carry

    lax.fori_loop(0, npast, store, 0)
    bn_ref[...] = bias_ref[npast][:, :LANE]


def _dsa_sample_select(s_past, s_new, k_top):
    q, past = s_past.shape
    kc = 512
    nch = past // kc + 1
    return pl.pallas_call(
        functools.partial(_dsa_sample_select_kernel, k_top=k_top),
        out_shape=[jax.ShapeDtypeStruct((q, past), F32), jax.ShapeDtypeStruct((q, LANE), F32)],
        scratch_shapes=[pltpu.VMEM((nch, q, kc), F32), pltpu.VMEM((nch, q, kc), F32)],
        compiler_params=pltpu.CompilerParams(vmem_limit_bytes=VMEM_LIMIT), name="dsa_sample_select",
    )(s_past, s_new)


def _dsa_sample_attn_kernel(pt_ref, qa_ref, bp_ref, bn_ref, kn_ref, vn_ref, *refs, pages):
    k_refs, v_refs = refs[:pages], refs[pages:2 * pages]
    o_ref, m_ref, l_ref, acc_ref = refs[2 * pages:]
    del pt_ref
    j = pl.program_id(1)
    rows = qa_ref.shape[1]
    s_len = rows // H_A
    q = qa_ref[0]
    scale = HEAD_DIM ** -0.5

    @pl.when(j == 0)
    def _():
        m_ref[...] = jnp.full_like(m_ref, NEG)
        l_ref[...] = jnp.zeros_like(l_ref)
        acc_ref[...] = jnp.zeros_like(acc_ref)

    def update(s, bias, vs, v_dot):
        n = s.shape[1]
        s = (s.reshape(s_len, H_A, n) * scale + bias[:, None, :]).reshape(rows, n)
        m = m_ref[...]
        m_new = jnp.maximum(m, jnp.max(s, axis=1, keepdims=True))
        alpha = jnp.exp(m - m_new)
        p = jnp.exp(s - m_new)
        pv = v_dot(p[:, :PAGE_SIZE].astype(BF16), vs[0])
        for t in range(1, len(vs)):
            pv = pv + v_dot(p[:, t * PAGE_SIZE:(t + 1) * PAGE_SIZE].astype(BF16), vs[t])
        l_ref[...] = alpha * l_ref[...] + jnp.sum(p, axis=1, keepdims=True)
        acc_ref[...] = alpha * acc_ref[...] + pv
        m_ref[...] = m_new

    s = jnp.concatenate([_dot(q, k_refs[t][...].astype(BF16)) for t in range(pages)], axis=1)
    update(s, bp_ref[0], [v_refs[t][...].astype(BF16) for t in range(pages)], _dot_nt)

    @pl.when(j == pl.num_programs(1) - 1)
    def _():
        update(_dot_nt(q, kn_ref[0].astype(BF16)), bn_ref[0], [vn_ref[0].astype(BF16)], _dot)
        o_ref[0] = acc_ref[...] / l_ref[...]


def _dsa_sample_attn(page_table, qa_s, bias_past, bias_new, kn_pad, vn_pad, pool_k, pool_v, pages):
    bd, n_pages = page_table.shape
    rows = qa_s.shape[1]
    s_len = rows // H_A
    per_b = lambda b, j, pt: (b, 0, 0)
    grid_spec = pltpu.PrefetchScalarGridSpec(
        num_scalar_prefetch=1, grid=(bd, n_pages // pages),
        in_specs=[pl.BlockSpec((1, rows, LANE), per_b),
                  pl.BlockSpec((1, s_len, pages * PAGE_SIZE), lambda b, j, pt: (b, 0, j)),
                  pl.BlockSpec((1, s_len, LANE), per_b),
                  pl.BlockSpec((1, PAGE_SIZE, LANE), per_b), pl.BlockSpec((1, PAGE_SIZE, LANE), per_b)]
        + _page_specs((KV_A * HEAD_DIM, PAGE_SIZE), pages) * 2,
        out_specs=pl.BlockSpec((1, rows, LANE), per_b),
        scratch_shapes=[pltpu.VMEM((rows, 1), F32), pltpu.VMEM((rows, 1), F32), pltpu.VMEM((rows, LANE), F32)])
    return pl.pallas_call(
        functools.partial(_dsa_sample_attn_kernel, pages=pages), grid_spec=grid_spec,
        out_shape=jax.ShapeDtypeStruct((bd, rows, LANE), F32),
        compiler_params=_params("parallel", "arbitrary"), name="dsa_sample_attn",
    )(page_table, qa_s, bias_past, bias_new, kn_pad, vn_pad, *([pool_k] * pages), *([pool_v] * pages))


def _moba_sample_kernel(pt_ref, qb_ref, kn_ref, vn_ref, *refs, pages, n_sel):
    k_refs, v_refs = refs[:pages], refs[pages:2 * pages]
    o_ref, gate_ref, m_ref, l_ref, acc_ref = refs[2 * pages:]
    del pt_ref
    j = pl.program_id(1)
    rows = qb_ref.shape[1]
    s_len = rows // H_B
    nbp = acc_ref.shape[0]
    ppb = MOBA_BLOCK // PAGE_SIZE
    scale = HEAD_DIM ** -0.5

    qs = qb_ref[0]
    pair = (lax.broadcasted_iota(I32, (rows, LANE), 0) % H_B) // 2
    qe = jnp.concatenate([jnp.where(pair == p, qs, jnp.zeros_like(qs)) for p in range(H_B // 2)], axis=1)
    lane = lax.broadcasted_iota(I32, (rows, LANE), 1)

    @pl.when(j == 0)
    def _():
        gate_ref[...] = jnp.full_like(gate_ref, -jnp.inf)
        m_ref[...] = jnp.full_like(m_ref, NEG)
        l_ref[...] = jnp.zeros_like(l_ref)

    for t in range(pages // ppb):
        n = j * (pages // ppb) + t
        qk = jnp.concatenate([_dot(qe, k_refs[ppb * t + i][...].astype(BF16)) for i in range(ppb)], axis=1)
        gate_n = jnp.sum(qk, axis=1, keepdims=True) * (1.0 / MOBA_BLOCK)
        s = qk * scale
        m_n = jnp.max(s, axis=1, keepdims=True)
        p = jnp.exp(s - m_n)
        here = lane == n
        gate_ref[...] = jnp.where(here, gate_n, gate_ref[...])
        m_ref[...] = jnp.where(here, m_n, m_ref[...])
        l_ref[...] = jnp.where(here, jnp.sum(p, axis=1, keepdims=True), l_ref[...])
        pv = _dot_nt(p[:, :PAGE_SIZE].astype(BF16), v_refs[ppb * t][...].astype(BF16))
        for i in range(1, ppb):
            pv = pv + _dot_nt(p[:, i * PAGE_SIZE:(i + 1) * PAGE_SIZE].astype(BF16),
                              v_refs[ppb * t + i][...].astype(BF16))
        acc_ref[n] = pv

    @pl.when(j == pl.num_programs(1) - 1)
    def _():
        sel = _top_blocks(gate_ref[...], lane < nbp, lane.astype(F32), n_sel, axis=1)
        s_own = _dot_nt(qe, kn_ref[0].astype(BF16)) * scale
        srow = lax.broadcasted_iota(I32, (rows, LANE), 0) // H_B
        s_own = jnp.where(lane <= srow, s_own, NEG)
        m_blk = jnp.where(sel, m_ref[...], NEG)
        m_all = jnp.maximum(jnp.max(m_blk, axis=1, keepdims=True), jnp.max(s_own, axis=1, keepdims=True))
        wgt = jnp.where(sel, jnp.exp(m_blk - m_all), 0.0)
        p_own = jnp.exp(s_own - m_all)
        l_all = jnp.sum(wgt * l_ref[...], axis=1, keepdims=True) + jnp.sum(p_own, axis=1, keepdims=True)
        o = _dot(p_own.astype(BF16), vn_ref[0].astype(BF16))
        for n in range(nbp):
            o = o + wgt[:, n:n + 1] * acc_ref[n]
        o = o / l_all
        head_lane = lax.broadcasted_iota(I32, (rows, WKV), 1) // HEAD_DIM
        head_row = lax.broadcasted_iota(I32, (rows, WKV), 0) % H_B
        o = jnp.where(head_lane == head_row, o, 0.0)
        o_ref[0] = jnp.sum(o.reshape(s_len, H_B, WKV), axis=1)


def _moba_sample(page_table, qb_s, kn_pad, vn_pad, pool_k, pool_v, pages):
    bd, n_pages = page_table.shape
    rows = qb_s.shape[1]
    s_len = rows // H_B
    nbp = n_pages * PAGE_SIZE // MOBA_BLOCK
    per_b = lambda b, j, pt: (b, 0, 0)
    grid_spec = pltpu.PrefetchScalarGridSpec(
        num_scalar_prefetch=1, grid=(bd, n_pages // pages),
        in_specs=[pl.BlockSpec((1, rows, LANE), per_b), pl.BlockSpec((1, PAGE_SIZE, WKV), per_b),
                  pl.BlockSpec((1, PAGE_SIZE, WKV), per_b)] + _page_specs((WKV, PAGE_SIZE), pages) * 2,
        out_specs=pl.BlockSpec((1, s_len, WKV), per_b),
        scratch_shapes=[pltpu.VMEM((rows, LANE), F32), pltpu.VMEM((rows, LANE), F32), pltpu.VMEM((rows, LANE), F32),
                        pltpu.VMEM((nbp, rows, WKV), F32)])
    return pl.pallas_call(
        functools.partial(_moba_sample_kernel, pages=pages, n_sel=min(TOPK_BLK, nbp)), grid_spec=grid_spec,
        out_shape=jax.ShapeDtypeStruct((bd, s_len, WKV), F32),
        compiler_params=_params("parallel", "arbitrary"), name="moba_sample",
    )(page_table, qb_s, kn_pad, vn_pad, *([pool_k] * pages), *([pool_v] * pages))


def _split_w_in(w_in):
    d = w_in.shape[0]
    sizes = (H_A * HEAD_DIM, KV_A * HEAD_DIM, KV_A * HEAD_DIM, H_IDX * D_IDX, D_IDX, H_IDX, WKV, WKV, WKV, d, d)
    parts, o = [], 0
    for s in sizes:
        parts.append(w_in[:, o:o + s])
        o += s
    return parts


def _expand_heads(w, lane_offsets):
    z = jnp.zeros((w.shape[0], HEAD_DIM), w.dtype)
    parts = []
    for h, off in enumerate(lane_offsets):
        blk = w[:, h * HEAD_DIM:(h + 1) * HEAD_DIM]
        parts += [blk, z] if off == 0 else [z, blk]
    return jnp.concatenate(parts, axis=1)


def _layout_w_in(w_in):
    d = w_in.shape[0]
    qa, ka, va, qi, ki, wi, qb, kb, vb, ga, gb = _split_w_in(w_in)
    heads_per_group = H_A // KV_A
    qa_e = _expand_heads(qa, [(h // heads_per_group) * HEAD_DIM for h in range(H_A)])
    qb_e = _expand_heads(qb, [(h % 2) * HEAD_DIM for h in range(H_B)])
    kiwi = jnp.concatenate([ki, wi, jnp.zeros((d, LANE - D_IDX - H_IDX), w_in.dtype)], axis=1)
    ki_pad = jnp.concatenate([ki, jnp.zeros((d, LANE - D_IDX), w_in.dtype)], axis=1)
    w_sample = jnp.concatenate([qa_e, _expand_heads(qi, [0] * H_IDX), qb_e, ka, va, kiwi, kb, vb, ga, gb], axis=1)
    w_t = jnp.concatenate([qa_e, qi, qb_e, ka, va, ki, kb, vb, wi, jnp.zeros_like(wi)], axis=1).T
    w_r = jnp.concatenate([ka, ki_pad, kb, ga, gb], axis=1)
    return w_sample.astype(BF16), w_t.astype(BF16), w_r.astype(BF16)


def _rope_angles(pos):
    half = ROT_DIM // 2
    inv = ROPE_THETA ** (-jnp.arange(half, dtype=F32) / half)
    ang = pos.astype(F32)[:, None] * inv[None, :]
    return jnp.cos(ang), jnp.sin(ang)


def _rope_tables(pos):
    cos, sin = _rope_angles(pos)
    n = pos.shape[0]
    pad = jnp.zeros((n, HEAD_DIM - ROT_DIM), F32)
    zero = jnp.zeros((n, ROT_DIM // 2), F32)
    c = jnp.concatenate([cos, cos, pad + 1.0], axis=1)
    s1 = jnp.concatenate([-sin, zero, pad], axis=1)
    s2 = jnp.concatenate([zero, sin, pad], axis=1)
    return tuple(jnp.tile(t, (1, LANE // HEAD_DIM)) for t in (c, s1, s2))


def _pages_transposed(pool):
    n_pool, page = pool.shape[:2]
    nd = pool.ndim
    return jnp.transpose(pool, (0,) + tuple(range(2, nd)) + (1,)).reshape(n_pool, -1, page)


def _pad_rows(x, rows):
    b, s, w = x.shape
    return jnp.concatenate([x, jnp.zeros((b, rows - s, w), x.dtype)], axis=1)


def _ffn(oa, ob, ga, gb, x2d, lw, g_final, tm_merge, tm_moe):
    x1, h2, cw = _merge(oa, ob, ga, gb, x2d, lw["wa"], lw["wb"], lw["wo"], lw["g_ffn"], lw["w_router"],
                        lw["b_router"], tm_merge)
    return _moe(h2, cw, x1, lw["wg"], lw["wu"], lw["wd"], g_final, tm_moe)


@jax.jit
def kernel(x_prompt, x_sample, cache_k_a, cache_v_a, cache_k_idx, cache_k_b, cache_v_b, page_table, w_in, w_br_a,
           w_br_b, w_out, g_mix, g_ffn, w_grp, b_grp, w_exp, b_exp, w_e_gate, w_e_up, w_e_down, g_final):
    bsz, t, d = x_prompt.shape
    bd, s_len, _ = x_sample.shape
    depth = w_in.shape[0]
    n_pages = page_table.shape[1]
    past = n_pages * PAGE_SIZE
    assert depth == 1, "the final RMSNorm is fused into the last layer's MoE kernel"
    assert past % MOBA_BLOCK == 0 and s_len <= PAGE_SIZE and t % TM_PROJ == 0

    xp = x_prompt.reshape(bsz * t, d)
    xs = x_sample.reshape(bd * s_len, d)
    ns = bd * s_len

    l = 0
    zpad = jnp.zeros((d, LANE - N_GROUPS - N_EXPERTS), F32)
    lw = dict(
        wa=w_br_a[l].astype(BF16), wb=w_br_b[l].astype(BF16), wo=w_out[l].astype(BF16), g_ffn=g_ffn[l],
        w_router=jnp.concatenate([w_grp[l], w_exp[l], zpad], axis=1),
        b_router=jnp.concatenate([b_grp[l], b_exp[l], zpad[0]]).reshape(1, LANE),
        wg=w_e_gate[l].astype(BF16), wu=w_e_up[l].astype(BF16), wd=w_e_down[l].astype(BF16))
    w_sample, w_t, w_r = _layout_w_in(w_in[l])

    pos_p = jnp.arange(t)
    cos_p, sin_p = _rope_angles(pos_p)
    (qa_t, qi_t, qb_t, ka_t, va_t, ki_t, kb_t, vb_t, wi_t, kab, kib, kbb, va_tc, vb_tc, kmean, ga, gb) = _project_prompt(
        xp, g_mix[l], w_t, w_r, (cos_p.T, sin_p.T), _rope_tables(pos_p), bsz, t)
    o_a = _dsa_prompt(qa_t, qi_t, wi_t, kib, kab, va_tc, bsz, t)
    o_b = _moba_prompt(qb_t, kmean.reshape(bsz * t // MOBA_BLOCK, WKV), kbb, vb_tc, bsz, t)
    y_prompt = _ffn(o_a, o_b, ga, gb, xp, lw, g_final, TM_MERGE, TM_MOE).reshape(bsz, t, d)

    def heads_last(x_t, heads):
        return jnp.transpose(x_t.reshape(1, bsz, heads, HEAD_DIM, t), (0, 1, 4, 2, 3))

    rows_p = (heads_last(ka_t, KV_A), heads_last(va_t, KV_A), jnp.transpose(ki_t, (0, 2, 1))[None],
              heads_last(kb_t, H_B), heads_last(vb_t, H_B))

    tabs_s = tuple(jnp.tile(tb, (bd, 1)) for tb in _rope_tables(past + jnp.arange(s_len)))
    qa, qi, qb, ka, va, kiwi, kb, vb, ga, gb = _project_sample(xs, g_mix[l], w_sample, tabs_s)
    rows = s_len * N_HEAD_SLABS
    qa_s = qa.reshape(bd, rows, LANE)
    qi_s = qi.reshape(bd, rows, LANE)
    qb_s = qb.reshape(bd, rows, LANE)
    wi_s = kiwi[:, D_IDX:D_IDX + H_IDX].reshape(bd, rows, 1)
    s_past, s_new = _dsa_sample_scores(page_table, qi_s, wi_s, _pad_rows(kiwi.reshape(bd, s_len, LANE), PAGE_SIZE),
                                       _pages_transposed(cache_k_idx[l]), min(PAGES_SCORES, n_pages))
    bias_past, bias_new = _dsa_sample_select(s_past.reshape(ns, past), s_new.reshape(ns, PAGE_SIZE),
                                             min(TOPK_IDX, (past + s_len) // 4))
    o_a = _dsa_sample_attn(page_table, qa_s, bias_past.reshape(bd, s_len, past),
                           bias_new.reshape(bd, s_len, PAGE_SIZE),
                           _pad_rows(ka.reshape(bd, s_len, LANE), PAGE_SIZE),
                           _pad_rows(va.reshape(bd, s_len, LANE), PAGE_SIZE),
                           _pages_transposed(cache_k_a[l]), _pages_transposed(cache_v_a[l]), min(PAGES_ATTN, n_pages))
    o_a = o_a.reshape(bd, s_len, H_A, KV_A, HEAD_DIM)
    hpg = H_A // KV_A
    o_a = jnp.concatenate([o_a[:, :, g * hpg:(g + 1) * hpg, g] for g in range(KV_A)], axis=2)
    o_a = o_a.reshape(ns, H_A * HEAD_DIM).astype(BF16)
    o_b = _moba_sample(page_table, qb_s, _pad_rows(kb.reshape(bd, s_len, WKV), PAGE_SIZE),
                       _pad_rows(vb.reshape(bd, s_len, WKV), PAGE_SIZE),
                       _pages_transposed(cache_k_b[l]), _pages_transposed(cache_v_b[l]), min(PAGES_MOBA, n_pages))
    o_b = o_b.reshape(ns, WKV).astype(BF16)
    y_sample = _ffn(o_a, o_b, ga, gb, xs, lw, g_final, ns, ns).reshape(bd, s_len, d)
    rows_s = (ka.reshape(1, bd, s_len, KV_A, HEAD_DIM), va.reshape(1, bd, s_len, KV_A, HEAD_DIM),
              kiwi[:, :D_IDX].reshape(1, bd, s_len, D_IDX), kb.reshape(1, bd, s_len, H_B, HEAD_DIM),
              vb.reshape(1, bd, s_len, H_B, HEAD_DIM))
    return (y_prompt, y_sample) + rows_p + rows_s
```

```python
import functools

import jax
import jax.numpy as jnp
from jax import lax
from jax.experimental import pallas as pl
from jax.experimental.pallas import tpu as pltpu

F32, BF16, I32 = jnp.float32, jnp.bfloat16, jnp.int32

HEAD_DIM = 64
ROT_DIM = HEAD_DIM // 4
ROPE_THETA = 500000.0
H_A = 8
KV_A = 2
H_IDX = 8
D_IDX = 64
TOPK_IDX = 256
H_B = 8
MOBA_BLOCK = 256
TOPK_BLK = 3
N_GROUPS = 4
EXPERTS_PER_GROUP = 8
N_EXPERTS = N_GROUPS * EXPERTS_PER_GROUP
D_EXPERT = 256
RMS_EPS = 1e-6
PAGE_SIZE = 128

LANE = 128
SUBLANE = 8
BF16_ROWS = 16
VMEM_LIMIT = 56 * 1024 * 1024

TM_PROJ = 512
DSA_Q = 256
DSA_KC = TM_PROJ
DSA_HEAD_SPLITS = 1
TM_MERGE = 512
TM_MOE = 1024
PAGES_SCORES = 32
PAGES_ATTN = 32
PAGES_MOBA = 16

NEG = -1e30
INT_MIN = -(2 ** 31)
KEY_NEG_INF = -2139095041
N_HEAD_SLABS = 8
Q_EXP = N_HEAD_SLABS * LANE
WKV = H_B * HEAD_DIM
SOFTMAX_SCALE = HEAD_DIM ** -0.5
assert SOFTMAX_SCALE == 0.125

_NT = (((1,), (1,)), ((), ()))
_HI = lax.Precision.HIGHEST


def _dot(a, b, precision=None):
    return jnp.dot(a, b, preferred_element_type=F32, precision=precision)


def _dot_nt(a, b, precision=None):
    return lax.dot_general(a, b, _NT, preferred_element_type=F32, precision=precision)


def _params(*sem):
    return pltpu.CompilerParams(dimension_semantics=sem, vmem_limit_bytes=VMEM_LIMIT)


def _rms_scale(x):
    return lax.rsqrt(jnp.mean(x * x, axis=-1, keepdims=True) + RMS_EPS)


def _rope(y, c, s1, s2):
    half = ROT_DIM // 2
    return y * c + pltpu.roll(y, LANE - half, 1) * s1 + pltpu.roll(y, half, 1) * s2


def _rope_t(y, cos_t, sin_t):
    half = ROT_DIM // 2
    parts = []
    for r in range(0, y.shape[0], HEAD_DIM):
        x1, x2 = y[r:r + half], y[r + half:r + ROT_DIM]
        parts += [x1 * cos_t - x2 * sin_t, x2 * cos_t + x1 * sin_t, y[r + ROT_DIM:r + HEAD_DIM]]
    return jnp.concatenate(parts, axis=0)


def _proj_prompt_kernel(x_ref, g_ref, wt_ref, wr_ref, cos_t_ref, sin_t_ref, c_ref, s1_ref, s2_ref,
                        qa_t_ref, qi_t_ref, qb_t_ref, ka_t_ref, va_t_ref, ki_t_ref, kb_t_ref, vb_t_ref, wi_t_ref,
                        kab_ref, kib_ref, kbb_ref, va_tc_ref, vb_tc_ref, kmean_ref, ga_ref, gb_ref):
    tm, d_model = x_ref.shape
    x = x_ref[...]
    h = ((x * _rms_scale(x)) * g_ref[...]).astype(BF16)
    cos_t, sin_t = cos_t_ref[...], sin_t_ref[...]
    c, s1, s2 = c_ref[...], s1_ref[...], s2_ref[...]

    off = [0]

    def proj_t(rows):
        y = _dot_nt(wt_ref[off[0]:off[0] + rows, :], h)
        off[0] += rows
        return y

    half_q = Q_EXP // 2
    for r0 in (0, half_q):
        qa_t_ref[0, r0:r0 + half_q, :] = _rope_t(proj_t(half_q), cos_t, sin_t).astype(BF16)
    qi_t_ref[0] = _rope_t(proj_t(H_IDX * D_IDX), cos_t, sin_t).astype(BF16)
    for r0 in (0, half_q):
        qb_t_ref[0, r0:r0 + half_q, :] = _rope_t(proj_t(half_q), cos_t, sin_t).astype(BF16)
    ka_t_ref[0] = _rope_t(proj_t(KV_A * HEAD_DIM), cos_t, sin_t)
    va_t = proj_t(KV_A * HEAD_DIM)
    va_t_ref[0] = va_t
    va_tc_ref[0] = va_t.astype(BF16)
    ki_t_ref[0] = _rope_t(proj_t(D_IDX), cos_t, sin_t)
    kb_t_ref[0] = _rope_t(proj_t(WKV), cos_t, sin_t)
    vb_t = proj_t(WKV)
    vb_t_ref[0] = vb_t
    for n in range(tm // MOBA_BLOCK):
        vb_tc_ref[n] = vb_t[:, n * MOBA_BLOCK:(n + 1) * MOBA_BLOCK].astype(BF16)
    wi_t_ref[0] = proj_t(2 * H_IDX)[:H_IDX]

    col = [0]

    def proj(width):
        y = _dot(h, wr_ref[:, col[0]:col[0] + width])
        col[0] += width
        return y

    y = proj(2 * LANE)
    kab_ref[...] = _rope(y[:, :LANE], c, s1, s2).astype(BF16)
    kib_ref[...] = _rope(y[:, LANE:], c, s1, s2)[:, :D_IDX].astype(BF16)
    y = proj(WKV)
    kb = jnp.concatenate([_rope(y[:, j * LANE:(j + 1) * LANE], c, s1, s2) for j in range(WKV // LANE)], axis=1)
    kbb_ref[...] = kb.astype(BF16)
    nblk = tm // MOBA_BLOCK
    kmean_ref[0] = jnp.sum(kb.reshape(nblk, MOBA_BLOCK, WKV), axis=1) * (1.0 / MOBA_BLOCK)
    ga_ref[...] = proj(d_model)
    gb_ref[...] = proj(d_model)


def _project_prompt(x2d, g, w_t, w_r, tabs_t, tabs, bsz, t):
    n, d = x2d.shape
    tm = TM_PROJ
    tpb = t // tm
    row = lambda i: (i, 0)
    const = lambda i: (0, 0)
    bt = lambda i: (i // tpb, 0, i % tpb)
    chunk = lambda i: (i, 0, 0)
    once = dict(pipeline_mode=pl.Buffered(1))
    in_specs = [pl.BlockSpec((tm, d), row), pl.BlockSpec((1, d), const),
                pl.BlockSpec(w_t.shape, const, **once), pl.BlockSpec(w_r.shape, const, **once)]
    in_specs += [pl.BlockSpec((ROT_DIM // 2, tm), lambda i: (0, i % tpb))] * 2
    in_specs += [pl.BlockSpec((tm, LANE), lambda i: (i % tpb, 0))] * 3
    t_feats = [(Q_EXP, BF16), (H_IDX * D_IDX, BF16), (Q_EXP, BF16), (KV_A * HEAD_DIM, F32), (KV_A * HEAD_DIM, F32),
               (D_IDX, F32), (WKV, F32), (WKV, F32), (H_IDX, F32)]
    out_shape = [jax.ShapeDtypeStruct((bsz, f, t), dt) for f, dt in t_feats]
    out_specs = [pl.BlockSpec((1, f, tm), bt) for f, _ in t_feats]
    r_feats = [(KV_A * HEAD_DIM, BF16), (D_IDX, BF16), (WKV, BF16)]
    out_shape += [jax.ShapeDtypeStruct((n, f), dt) for f, dt in r_feats]
    out_specs += [pl.BlockSpec((tm, f), row) for f, _ in r_feats]
    nblk = tm // MOBA_BLOCK
    out_shape += [jax.ShapeDtypeStruct((n // tm, KV_A * HEAD_DIM, tm), BF16),
                  jax.ShapeDtypeStruct((n // MOBA_BLOCK, WKV, MOBA_BLOCK), BF16),
                  jax.ShapeDtypeStruct((n // tm, nblk, WKV), F32),
                  jax.ShapeDtypeStruct((n, d), F32), jax.ShapeDtypeStruct((n, d), F32)]
    out_specs += [pl.BlockSpec((1, KV_A * HEAD_DIM, tm), chunk), pl.BlockSpec((nblk, WKV, MOBA_BLOCK), chunk),
                  pl.BlockSpec((1, nblk, WKV), chunk), pl.BlockSpec((tm, d), row), pl.BlockSpec((tm, d), row)]
    return pl.pallas_call(
        _proj_prompt_kernel, grid=(n // tm,), in_specs=in_specs, out_specs=out_specs, out_shape=out_shape,
        compiler_params=_params("parallel"), name="proj_prompt",
    )(x2d, g.reshape(1, d), w_t, w_r, *tabs_t, *tabs)


def _proj_sample_kernel(x_ref, g_ref, w_ref, c_ref, s1_ref, s2_ref,
                        qa_ref, qi_ref, qb_ref, ka_ref, va_ref, kiwi_ref, kb_ref, vb_ref, ga_ref, gb_ref):
    tm, d_model = x_ref.shape
    x = x_ref[...]
    h = ((x * _rms_scale(x)) * g_ref[...]).astype(BF16)
    c, s1, s2 = c_ref[...], s1_ref[...], s2_ref[...]
    off = [0]

    def proj(width):
        y = _dot(h, w_ref[:, off[0]:off[0] + width])
        off[0] += width
        return y

    def rope_slabs(y):
        return jnp.concatenate([_rope(y[:, j * LANE:(j + 1) * LANE], c, s1, s2) for j in range(y.shape[1] // LANE)],
                               axis=1)

    for q_ref in (qa_ref, qi_ref, qb_ref):
        q_ref[...] = rope_slabs(proj(Q_EXP)).astype(BF16)
    y = proj(3 * LANE)
    ka_ref[...] = _rope(y[:, :LANE], c, s1, s2)
    va_ref[...] = y[:, LANE:2 * LANE]
    is_ki = lax.broadcasted_iota(I32, (tm, LANE), 1) < D_IDX
    kiwi_ref[...] = _rope(y[:, 2 * LANE:], jnp.where(is_ki, c, 1.0), jnp.where(is_ki, s1, 0.0),
                          jnp.where(is_ki, s2, 0.0))
    kb_ref[...] = rope_slabs(proj(WKV))
    vb_ref[...] = proj(WKV)
    ga_ref[...] = proj(d_model)
    gb_ref[...] = proj(d_model)


def _project_sample(x2d, g, w_all, tabs):
    n, d = x2d.shape
    const = lambda i: (0, 0)
    widths = [(Q_EXP, BF16)] * 3 + [(LANE, F32)] * 3 + [(WKV, F32)] * 2 + [(d, F32)] * 2
    return pl.pallas_call(
        _proj_sample_kernel, grid=(1,),
        in_specs=[pl.BlockSpec((n, d), const), pl.BlockSpec((1, d), const),
                  pl.BlockSpec(w_all.shape, const, pipeline_mode=pl.Buffered(1))] + [pl.BlockSpec((n, LANE), const)] * 3,
        out_specs=[pl.BlockSpec((n, w), const) for w, _ in widths],
        out_shape=[jax.ShapeDtypeStruct((n, w), dt) for w, dt in widths],
        compiler_params=_params("arbitrary"), name="proj_sample",
    )(x2d, g.reshape(1, d), w_all, *tabs)


def _key_to_float(key):
    return pltpu.bitcast(key ^ ((key >> 31) & 0x7FFFFFFF), F32)


def _select_bias(score_ref, bias_ref, nc, k_top, n_keys_log2, key_axis):
    _, d0, d1 = score_ref.shape
    kc = (d0, d1)[key_axis]
    kidx0 = lax.broadcasted_iota(I32, (d0, d1), key_axis)
    k_top = float(k_top)
    qshape = (1, d1) if key_axis == 0 else (d0, 1)

    def fold(f):
        if key_axis == 0:
            return jnp.sum(f.reshape(SUBLANE, d0 // SUBLANE, d1), axis=0)
        part = f[:, :LANE]
        for j in range(1, d1 // LANE):
            part = part + f[:, j * LANE:(j + 1) * LANE]
        return part

    def count(pred):
        def body(c, cnt):
            return cnt + fold(jnp.where(pred(score_ref[c], c), 1.0, 0.0))
        zero = jnp.zeros((d0 // SUBLANE, d1) if key_axis == 0 else (d0, LANE), F32)
        return jnp.sum(lax.fori_loop(0, nc, body, zero), axis=key_axis, keepdims=True)

    nonneg = count(lambda s, c: s >= 0.0)
    v0 = jnp.where(nonneg >= k_top, 0, INT_MIN).astype(I32)

    def bit_body(b, v):
        cand = v | jnp.left_shift(jnp.int32(1), 30 - b)
        cf = _key_to_float(cand)
        return jnp.where(count(lambda s, c: s >= cf) >= k_top, cand, v)

    v = lax.fori_loop(0, 31, bit_body, v0)
    thr = _key_to_float(jnp.maximum(v, KEY_NEG_INF))

    at_least = count(lambda s, c: s >= thr)
    tied = (at_least > k_top) & (thr > -jnp.inf)
    all_keys = jnp.full(qshape, 2 ** n_keys_log2 - 1, I32)

    def tie_search():
        need = k_top - count(lambda s, c: s > thr)

        def tie_body(b, u):
            cand = u | jnp.left_shift(jnp.int32(1), n_keys_log2 - 1 - b)
            below = count(lambda s, c: (s == thr) & (c * kc + kidx0 < cand))
            return jnp.where(below < need, cand, u)

        return lax.fori_loop(0, n_keys_log2, tie_body, jnp.zeros(qshape, I32))

    u = lax.cond(jnp.max(jnp.where(tied, 1.0, 0.0)) > 0.0, tie_search, lambda: all_keys)

    def bias_body(c, carry):
        s = score_ref[c]
        sel = (s > -jnp.inf) & ((s > thr) | ((s == thr) & (c * kc + kidx0 <= u)))
        bias_ref[c] = jnp.where(sel, 0.0, NEG)
        return carry

    lax.fori_loop(0, nc, bias_body, 0)


def _dsa_prompt_kernel(qa_t_ref, qi_t_ref, wi_t_ref, kib_ref, kab_ref, va_tc_ref, o_ref, score_ref, bias_ref, *, k_top):
    nchunks, kc, q = score_ref.shape
    i = pl.program_id(1)
    nc = (i * q + q - 1) // kc + 1
    key0 = lax.broadcasted_iota(I32, (kc, q), 0)
    qpos = i * q + lax.broadcasted_iota(I32, (kc, q), 1)
    w = wi_t_ref[0] * (H_IDX ** -0.5 * D_IDX ** -0.5)
    qi_all = jnp.concatenate([qi_t_ref[0, h * D_IDX:(h + 1) * D_IDX, :] for h in range(H_IDX)], axis=1)

    def score_body(c, carry):
        d = _dot(kib_ref[pl.ds(pl.multiple_of(c * kc, kc), kc), :], qi_all)
        acc = jnp.zeros((kc, q), F32)
        for h in range(H_IDX):
            acc = acc + jnp.maximum(d[:, h * q:(h + 1) * q], 0.0) * w[h:h + 1, :]
        score_ref[c] = jnp.where(c * kc + key0 <= qpos, acc, -jnp.inf)
        return carry

    lax.fori_loop(0, nc, score_body, 0)
    _select_bias(score_ref, bias_ref, nc, k_top, (nchunks * kc - 1).bit_length(), key_axis=0)

    qa_all = jnp.concatenate([qa_t_ref[0, h * LANE:(h + 1) * LANE, :] for h in range(H_A)], axis=1)
    qa_all = qa_all * SOFTMAX_SCALE

    hs = H_A // DSA_HEAD_SPLITS
    q_parts = [qa_all[:, g * hs * q:(g + 1) * hs * q] for g in range(DSA_HEAD_SPLITS)]

    dv = KV_A * HEAD_DIM
    ones_rows = jnp.ones((BF16_ROWS, kc), BF16)

    def att_body(c, carry):
        k = kab_ref[pl.ds(pl.multiple_of(c * kc, kc), kc), :]
        v = jnp.concatenate([va_tc_ref[c], ones_rows], axis=0)
        b = bias_ref[c]
        out = []
        for g in range(DSA_HEAD_SPLITS):
            m, l, acc = carry[g]
            s = _dot(k, q_parts[g])
            s = jnp.concatenate([s[:, h * q:(h + 1) * q] + b for h in range(hs)], axis=1)
            m_new = jnp.maximum(m, jnp.max(s, axis=0, keepdims=True))
            alpha = jnp.exp(m - m_new)
            p = jnp.exp((s - m_new).astype(BF16))
            pv = _dot(v, p)
            l = alpha * l + pv[dv:dv + 1]
            acc = alpha * acc + pv[:dv]
            out.append((m_new, l, acc))
        return tuple(out)

    init = tuple((jnp.full((1, hs * q), NEG, F32), jnp.zeros((1, hs * q), F32),
                  jnp.zeros((KV_A * HEAD_DIM, hs * q), F32)) for _ in range(DSA_HEAD_SPLITS))
    final = lax.fori_loop(0, nc, att_body, init)
    o = jnp.concatenate([acc / l for _, l, acc in final], axis=1)
    heads_per_group = H_A // KV_A
    for j in range(H_A // 2):
        g0 = ((2 * j) // heads_per_group) * HEAD_DIM
        pair = jnp.concatenate([o[g0:g0 + HEAD_DIM, (2 * j) * q:(2 * j + 1) * q],
                                o[g0:g0 + HEAD_DIM, (2 * j + 1) * q:(2 * j + 2) * q]], axis=0)
        o_ref[:, j * LANE:(j + 1) * LANE] = pair.T.astype(BF16)


def _dsa_prompt(qa_t, qi_t, wi_t, kib, kab, va_tc, bsz, t):
    q, kc = DSA_Q, DSA_KC
    nq, nchunks = t // q, t // kc
    tile_t = lambda b, i: (b, 0, i)
    return pl.pallas_call(
        functools.partial(_dsa_prompt_kernel, k_top=min(TOPK_IDX, t // 4)),
        grid=(bsz, nq),
        in_specs=[pl.BlockSpec((1, Q_EXP, q), tile_t), pl.BlockSpec((1, H_IDX * D_IDX, q), tile_t),
                  pl.BlockSpec((1, H_IDX, q), tile_t),
                  pl.BlockSpec((t, D_IDX), lambda b, i: (b, 0)), pl.BlockSpec((t, KV_A * HEAD_DIM), lambda b, i: (b, 0)),
                  pl.BlockSpec((nchunks, KV_A * HEAD_DIM, kc), lambda b, i: (b, 0, 0))],
        out_specs=pl.BlockSpec((q, H_A * HEAD_DIM), lambda b, i: (b * nq + i, 0)),
        out_shape=jax.ShapeDtypeStruct((bsz * t, H_A * HEAD_DIM), BF16),
        scratch_shapes=[pltpu.VMEM((nchunks, kc, q), F32), pltpu.VMEM((nchunks, kc, q), F32)],
        compiler_params=_params("parallel", "arbitrary"), name="dsa_prompt",
    )(qa_t, qi_t, wi_t, kib, kab, va_tc)


def _top_blocks(gate, valid, idxf, n_sel, axis):
    gate = jnp.where(valid, gate, -jnp.inf)
    sel = jnp.zeros(gate.shape, jnp.bool_)
    for _ in range(n_sel):
        mx = jnp.max(gate, axis=axis, keepdims=True)
        first = jnp.min(jnp.where(gate == mx, idxf, 1e9), axis=axis, keepdims=True)
        pick = idxf == first
        sel = sel | (pick & valid)
        gate = jnp.where(pick, -jnp.inf, gate)
    return sel


def _moba_prompt_kernel(qb_t_ref, km_ref, kb_ref, vb_tc_ref, o_ref, *, n_sel):
    qb = o_ref.shape[0]
    nb = km_ref.shape[0]
    j = pl.program_id(1)
    r = 2 * qb
    nrow = lax.broadcasted_iota(I32, (nb, r), 0)
    nrowf = nrow.astype(F32)
    qc = lax.broadcasted_iota(I32, (qb, r), 1)
    own_mask = lax.broadcasted_iota(I32, (qb, r), 0) <= jnp.where(qc >= qb, qc - qb, qc)
    own = pl.multiple_of(j * qb, qb)

    pairs = range(H_B // 2)
    sls = [slice(p * LANE, (p + 1) * LANE) for p in pairs]
    ones_rows = jnp.ones((BF16_ROWS, MOBA_BLOCK), BF16)

    def pv_and_sum(n, p, pr):
        return _dot(jnp.concatenate([vb_tc_ref[n, sls[p], :], ones_rows], axis=0), pr)

    qes, selbiases, init = [], [], []
    for p in pairs:
        qe = jnp.concatenate([qb_t_ref[0, (2 * p) * LANE:(2 * p + 1) * LANE, :],
                              qb_t_ref[0, (2 * p + 1) * LANE:(2 * p + 2) * LANE, :]], axis=1)
        gate = _dot(km_ref[:, sls[p]], qe.astype(F32), precision=_HI)
        sel = _top_blocks(gate, nrow < j, nrowf, n_sel, axis=0)
        selbiases.append(jnp.where(sel, 0.0, NEG))
        qe = qe * SOFTMAX_SCALE
        qes.append(qe)

        s = jnp.where(own_mask, _dot(kb_ref[pl.ds(own, qb), sls[p]], qe), NEG)
        m = jnp.max(s, axis=0, keepdims=True)
        pv = pv_and_sum(j, p, jnp.exp((s - m).astype(BF16)))
        init.append((m, pv[LANE:LANE + 1], pv[:LANE]))

    half = lax.broadcasted_iota(I32, (2 * MOBA_BLOCK, r), 0) < MOBA_BLOCK

    def body(n2, carry):
        start = pl.multiple_of(n2 * (2 * MOBA_BLOCK), 2 * MOBA_BLOCK)
        out = []
        for p in pairs:
            m, l, acc = carry[p]
            rowb = [jnp.sum(jnp.where(nrow == 2 * n2 + i, selbiases[p], 0.0), axis=0, keepdims=True) for i in (0, 1)]
            s = _dot(kb_ref[pl.ds(start, 2 * MOBA_BLOCK), sls[p]], qes[p]) + jnp.where(half, rowb[0], rowb[1])
            m_new = jnp.maximum(m, jnp.max(s, axis=0, keepdims=True))
            alpha = jnp.exp(m - m_new)
            pr = jnp.exp((s - m_new).astype(BF16))
            pv = pv_and_sum(2 * n2, p, pr[:MOBA_BLOCK]) + pv_and_sum(2 * n2 + 1, p, pr[MOBA_BLOCK:])
            out.append((m_new, alpha * l + pv[LANE:LANE + 1], alpha * acc + pv[:LANE]))
        return tuple(out)

    final = lax.fori_loop(0, (j + 1) // 2, body, tuple(init))
    for p in pairs:
        _, l, acc = final[p]
        o = acc / l
        pair = jnp.concatenate([o[:HEAD_DIM, :qb], o[HEAD_DIM:, qb:]], axis=0)
        o_ref[:, sls[p]] = pair.T.astype(BF16)


def _moba_prompt(qb_t, kmean, kbb, vb_tc, bsz, t):
    nb = t // MOBA_BLOCK
    return pl.pallas_call(
        functools.partial(_moba_prompt_kernel, n_sel=min(TOPK_BLK, nb - 1)),
        grid=(bsz, nb),
        in_specs=[pl.BlockSpec((1, Q_EXP, MOBA_BLOCK), lambda b, i: (b, 0, i)), pl.BlockSpec((nb, WKV), lambda b, i: (b, 0)),
                  pl.BlockSpec((t, WKV), lambda b, i: (b, 0)),
                  pl.BlockSpec((nb, WKV, MOBA_BLOCK), lambda b, i: (b, 0, 0))],
        out_specs=pl.BlockSpec((MOBA_BLOCK, WKV), lambda b, i: (b * nb + i, 0)),
        out_shape=jax.ShapeDtypeStruct((bsz * t, WKV), BF16),
        compiler_params=_params("parallel", "arbitrary"), name="moba_prompt",
    )(qb_t, kmean, kbb, vb_tc)


def _merge_kernel(oa_ref, ob_ref, ga_ref, gb_ref, x_ref, wa_ref, wb_ref, wo_ref, gf_ref, wr_ref, br_ref,
                  x1_ref, h2_ref, cw_ref):
    tm = x_ref.shape[0]
    u = jax.nn.sigmoid(ga_ref[...]) * _dot(oa_ref[...], wa_ref[...]) \
        + jax.nn.sigmoid(gb_ref[...]) * _dot(ob_ref[...], wb_ref[...])
    x1 = x_ref[...] + _dot(u.astype(BF16), wo_ref[...])
    x1_ref[...] = x1
    h2 = (x1 * _rms_scale(x1)) * gf_ref[...]
    h2_ref[...] = h2.astype(BF16)

    rl = _dot(h2, wr_ref[...], precision=_HI) + br_ref[...]
    lanef = lax.broadcasted_iota(I32, (tm, LANE), 1).astype(F32)

    def masked_softmax(mask):
        z = jnp.where(mask, rl, -jnp.inf)
        e = jnp.exp(z - jnp.max(z, axis=1, keepdims=True))
        return e / jnp.sum(e, axis=1, keepdims=True)

    def first_lane(cond):
        return jnp.min(jnp.where(cond, lanef, 1e9), axis=1, keepdims=True)

    gmask = lanef < N_GROUPS
    gp = masked_softmax(gmask)
    gprob = jnp.max(gp, axis=1, keepdims=True)
    gsel = first_lane((gp == gprob) & gmask)
    lo = N_GROUPS + EXPERTS_PER_GROUP * gsel
    emask = (lanef >= lo) & (lanef < lo + EXPERTS_PER_GROUP)
    ep = jnp.where(emask, masked_softmax(emask), -1.0)
    p1 = jnp.max(ep, axis=1, keepdims=True)
    i1 = first_lane(ep == p1)
    ep = jnp.where(lanef == i1, -1.0, ep)
    p2 = jnp.max(ep, axis=1, keepdims=True)
    i2 = first_lane(ep == p2)
    den = p1 + p2
    cw_ref[...] = jnp.where(lanef == i1, gprob * p1 / den, 0.0) + jnp.where(lanef == i2, gprob * p2 / den, 0.0)


def _merge(oa, ob, ga, gb, x2d, wa, wb, wo, g_ffn, w_router, b_router, tm):
    n, d = x2d.shape
    row = lambda i: (i, 0)
    const = lambda i: (0, 0)
    ha = oa.shape[1]
    return pl.pallas_call(
        _merge_kernel, grid=(n // tm,),
        in_specs=[pl.BlockSpec((tm, ha), row), pl.BlockSpec((tm, ha), row), pl.BlockSpec((tm, d), row),
                  pl.BlockSpec((tm, d), row), pl.BlockSpec((tm, d), row), pl.BlockSpec((ha, d), const),
                  pl.BlockSpec((ha, d), const), pl.BlockSpec((d, d), const), pl.BlockSpec((1, d), const),
                  pl.BlockSpec((d, LANE), const), pl.BlockSpec((1, LANE), const)],
        out_specs=[pl.BlockSpec((tm, d), row), pl.BlockSpec((tm, d), row), pl.BlockSpec((tm, LANE), row)],
        out_shape=[jax.ShapeDtypeStruct((n, d), F32), jax.ShapeDtypeStruct((n, d), BF16),
                   jax.ShapeDtypeStruct((n, LANE), F32)],
        compiler_params=_params("parallel"), name="merge",
    )(oa, ob, ga, gb, x2d, wa, wb, wo, g_ffn.reshape(1, d), w_router, b_router)


def _moe_kernel(h_ref, cw_ref, x_ref, wg_ref, wu_ref, wd_ref, gfin_ref, y_ref, acc_ref):
    tm = h_ref.shape[0]
    e = pl.program_id(1)

    @pl.when(e == 0)
    def _():
        acc_ref[...] = jnp.zeros_like(acc_ref)

    h = h_ref[...]
    a = _dot(h, wg_ref[0])
    hdn = (a * jax.nn.sigmoid(a)) * _dot(h, wu_ref[0])
    y = _dot(hdn.astype(BF16), wd_ref[0])
    lane = lax.broadcasted_iota(I32, (tm, LANE), 1)
    col = jnp.sum(jnp.where(lane == e + N_GROUPS, cw_ref[...], 0.0), axis=1, keepdims=True)
    acc_ref[...] += y * col

    @pl.when(e == pl.num_programs(1) - 1)
    def _():
        out = x_ref[...] + acc_ref[...]
        y_ref[...] = (out * _rms_scale(out)) * gfin_ref[...]


def _moe(h2, cw, x1, wg, wu, wd, g_final, tm):
    n, d = x1.shape
    ne, _, de = wg.shape
    row = lambda i, e: (i, 0)
    return pl.pallas_call(
        _moe_kernel, grid=(n // tm, ne),
        in_specs=[pl.BlockSpec((tm, d), row), pl.BlockSpec((tm, LANE), row), pl.BlockSpec((tm, d), row),
                  pl.BlockSpec((1, d, de), lambda i, e: (e, 0, 0)), pl.BlockSpec((1, d, de), lambda i, e: (e, 0, 0)),
                  pl.BlockSpec((1, de, d), lambda i, e: (e, 0, 0)), pl.BlockSpec((1, d), lambda i, e: (0, 0))],
        out_specs=pl.BlockSpec((tm, d), row),
        out_shape=jax.ShapeDtypeStruct((n, d), F32),
        scratch_shapes=[pltpu.VMEM((tm, d), F32)],
        compiler_params=_params("parallel", "arbitrary"), name="moe",
    )(h2, cw, x1, wg, wu, wd, g_final.reshape(1, d))


def _page_specs(shape_tail, pages_per_step):
    zeros = (0,) * len(shape_tail)
    return [pl.BlockSpec((None,) + shape_tail,
                         functools.partial(lambda b, j, pt, t: (pt[b, j * pages_per_step + t],) + zeros, t=t))
            for t in range(pages_per_step)]


def _dsa_sample_score_kernel(pt_ref, qi_ref, w_ref, kin_ref, *refs, pages):
    page_refs, (s_ref, snew_ref) = refs[:pages], refs[pages:]
    del pt_ref
    j = pl.program_id(1)
    rows = qi_ref.shape[1]
    s_len = rows // H_IDX
    q = qi_ref[0][:, :D_IDX]
    w = w_ref[0] * (H_IDX ** -0.5 * D_IDX ** -0.5)

    def scores(d):
        d = jnp.maximum(d, 0.0) * w
        return jnp.sum(d.reshape(s_len, H_IDX, d.shape[1]), axis=1)

    k_t = jnp.concatenate([page_refs[t][...].astype(BF16) for t in range(pages)], axis=1)
    s_ref[0] = scores(_dot(q, k_t))

    @pl.when(j == pl.num_programs(1) - 1)
    def _():
        sn = scores(_dot_nt(q, kin_ref[0][:, :D_IDX].astype(BF16)))
        si = lax.broadcasted_iota(I32, (s_len, PAGE_SIZE), 0)
        ki = lax.broadcasted_iota(I32, (s_len, PAGE_SIZE), 1)
        snew_ref[0] = jnp.where(ki <= si, sn, -jnp.inf)


def _dsa_sample_scores(page_table, qi_s, wi_s, kin_pad, pool_kidx, pages):
    bd, n_pages = page_table.shape
    rows = qi_s.shape[1]
    s_len = rows // H_IDX
    per_b = lambda b, j, pt: (b, 0, 0)
    grid_spec = pltpu.PrefetchScalarGridSpec(
        num_scalar_prefetch=1, grid=(bd, n_pages // pages),
        in_specs=[pl.BlockSpec((1, rows, LANE), per_b), pl.BlockSpec((1, rows, 1), per_b),
                  pl.BlockSpec((1, PAGE_SIZE, LANE), per_b)] + _page_specs((D_IDX, PAGE_SIZE), pages),
        out_specs=[pl.BlockSpec((1, s_len, pages * PAGE_SIZE), lambda b, j, pt: (b, 0, j)),
                   pl.BlockSpec((1, s_len, PAGE_SIZE), per_b)])
    return pl.pallas_call(
        functools.partial(_dsa_sample_score_kernel, pages=pages), grid_spec=grid_spec,
        out_shape=[jax.ShapeDtypeStruct((bd, s_len, n_pages * PAGE_SIZE), F32),
                   jax.ShapeDtypeStruct((bd, s_len, PAGE_SIZE), F32)],
        compiler_params=_params("parallel", "arbitrary"), name="dsa_sample_scores",
    )(page_table, qi_s, wi_s, kin_pad, *([pool_kidx] * pages))


def _dsa_sample_select_kernel(sp_ref, sn_ref, bp_ref, bn_ref, score_ref, bias_ref, *, k_top):
    nch, q, kc = score_ref.shape
    npast = nch - 1

    def load(c, carry):
        score_ref[c] = sp_ref[:, pl.ds(pl.multiple_of(c * kc, kc), kc)]
        return carry

    lax.fori_loop(0, npast, load, 0)
    score_ref[npast] = jnp.concatenate([sn_ref[...], jnp.full((q, kc - LANE), -jnp.inf, F32)], axis=1)
    _select_bias(score_ref, bias_ref, nch, k_top, (nch * kc - 1).bit_length(), key_axis=1)

    def store(c, carry):
        bp_ref[:, pl.ds(pl.multiple_of(c * kc, kc), kc)] = bias_ref[c]
        return carry

    lax.fori_loop(0, npast, store, 0)
    bn_ref[...] = bias_ref[npast][:, :LANE]


def _dsa_sample_select(s_past, s_new, k_top):
    q, past = s_past.shape
    kc = 512
    nch = past // kc + 1
    return pl.pallas_call(
        functools.partial(_dsa_sample_select_kernel, k_top=k_top),
        out_shape=[jax.ShapeDtypeStruct((q, past), F32), jax.ShapeDtypeStruct((q, LANE), F32)],
        scratch_shapes=[pltpu.VMEM((nch, q, kc), F32), pltpu.VMEM((nch, q, kc), F32)],
        compiler_params=pltpu.CompilerParams(vmem_limit_bytes=VMEM_LIMIT), name="dsa_sample_select",
    )(s_past, s_new)


def _dsa_sample_attn_kernel(pt_ref, qa_ref, bp_ref, bn_ref, kn_ref, vn_ref, *refs, pages):
    k_refs, v_refs = refs[:pages], refs[pages:2 * pages]
    o_ref, m_ref, l_ref, acc_ref = refs[2 * pages:]
    del pt_ref
    j = pl.program_id(1)
    rows = qa_ref.shape[1]
    s_len = rows // H_A
    q = qa_ref[0]
    scale = HEAD_DIM ** -0.5

    @pl.when(j == 0)
    def _():
        m_ref[...] = jnp.full_like(m_ref, NEG)
        l_ref[...] = jnp.zeros_like(l_ref)
        acc_ref[...] = jnp.zeros_like(acc_ref)

    def update(s, bias, v, v_dot):
        n = s.shape[1]
        s = (s.reshape(s_len, H_A, n) * scale + bias[:, None, :]).reshape(rows, n)
        m = m_ref[...]
        m_new = jnp.maximum(m, jnp.max(s, axis=1, keepdims=True))
        alpha = jnp.exp(m - m_new)
        p = jnp.exp(s - m_new)
        l_ref[...] = alpha * l_ref[...] + jnp.sum(p, axis=1, keepdims=True)
        acc_ref[...] = alpha * acc_ref[...] + v_dot(p.astype(BF16), v)
        m_ref[...] = m_new

    k_t = jnp.concatenate([k_refs[t][...].astype(BF16) for t in range(pages)], axis=1)
    v_t = jnp.concatenate([v_refs[t][...].astype(BF16) for t in range(pages)], axis=1)
    update(_dot(q, k_t), bp_ref[0], v_t, _dot_nt)

    @pl.when(j == pl.num_programs(1) - 1)
    def _():
        update(_dot_nt(q, kn_ref[0].astype(BF16)), bn_ref[0], vn_ref[0].astype(BF16), _dot)
        o_ref[0] = acc_ref[...] / l_ref[...]


def _dsa_sample_attn(page_table, qa_s, bias_past, bias_new, kn_pad, vn_pad, pool_k, pool_v, pages):
    bd, n_pages = page_table.shape
    rows = qa_s.shape[1]
    s_len = rows // H_A
    per_b = lambda b, j, pt: (b, 0, 0)
    grid_spec = pltpu.PrefetchScalarGridSpec(
        num_scalar_prefetch=1, grid=(bd, n_pages // pages),
        in_specs=[pl.BlockSpec((1, rows, LANE), per_b),
                  pl.BlockSpec((1, s_len, pages * PAGE_SIZE), lambda b, j, pt: (b, 0, j)),
                  pl.BlockSpec((1, s_len, LANE), per_b),
                  pl.BlockSpec((1, PAGE_SIZE, LANE), per_b), pl.BlockSpec((1, PAGE_SIZE, LANE), per_b)]
        + _page_specs((KV_A * HEAD_DIM, PAGE_SIZE), pages) * 2,
        out_specs=pl.BlockSpec((1, rows, LANE), per_b),
        scratch_shapes=[pltpu.VMEM((rows, 1), F32), pltpu.VMEM((rows, 1), F32), pltpu.VMEM((rows, LANE), F32)])
    return pl.pallas_call(
        functools.partial(_dsa_sample_attn_kernel, pages=pages), grid_spec=grid_spec,
        out_shape=jax.ShapeDtypeStruct((bd, rows, LANE), F32),
        compiler_params=_params("parallel", "arbitrary"), name="dsa_sample_attn",
    )(page_table, qa_s, bias_past, bias_new, kn_pad, vn_pad, *([pool_k] * pages), *([pool_v] * pages))


def _moba_sample_kernel(pt_ref, qb_ref, kn_ref, vn_ref, *refs, pages, n_sel):
    k_refs, v_refs = refs[:pages], refs[pages:2 * pages]
    o_ref, gate_ref, m_ref, l_ref, acc_ref = refs[2 * pages:]
    del pt_ref
    j = pl.program_id(1)
    rows = qb_ref.shape[1]
    s_len = rows // H_B
    nbp = acc_ref.shape[0]
    ppb = MOBA_BLOCK // PAGE_SIZE
    scale = HEAD_DIM ** -0.5

    qs = qb_ref[0]
    pair = (lax.broadcasted_iota(I32, (rows, LANE), 0) % H_B) // 2
    qe = jnp.concatenate([jnp.where(pair == p, qs, jnp.zeros_like(qs)) for p in range(H_B // 2)], axis=1)
    lane = lax.broadcasted_iota(I32, (rows, LANE), 1)

    @pl.when(j == 0)
    def _():
        gate_ref[...] = jnp.full_like(gate_ref, -jnp.inf)
        m_ref[...] = jnp.full_like(m_ref, NEG)
        l_ref[...] = jnp.zeros_like(l_ref)

    gate, m_blk, l_blk = gate_ref[...], m_ref[...], l_ref[...]
    for t in range(pages // ppb):
        n = j * (pages // ppb) + t
        k_t = jnp.concatenate([k_refs[ppb * t + i][...].astype(BF16) for i in range(ppb)], axis=1)
        v_t = jnp.concatenate([v_refs[ppb * t + i][...].astype(BF16) for i in range(ppb)], axis=1)
        qk = _dot(qe, k_t)
        gate_n = jnp.sum(qk, axis=1, keepdims=True) * (1.0 / MOBA_BLOCK)
        s = qk * scale
        m_n = jnp.max(s, axis=1, keepdims=True)
        p = jnp.exp(s - m_n)
        here = lane == n
        gate = jnp.where(here, gate_n, gate)
        m_blk = jnp.where(here, m_n, m_blk)
        l_blk = jnp.where(here, jnp.sum(p, axis=1, keepdims=True), l_blk)
        acc_ref[n] = _dot_nt(p.astype(BF16), v_t)
    gate_ref[...] = gate
    m_ref[...] = m_blk
    l_ref[...] = l_blk

    @pl.when(j == pl.num_programs(1) - 1)
    def _():
        sel = _top_blocks(gate_ref[...], lane < nbp, lane.astype(F32), n_sel, axis=1)
        s_own = _dot_nt(qe, kn_ref[0].astype(BF16)) * scale
        srow = lax.broadcasted_iota(I32, (rows, LANE), 0) // H_B
        s_own = jnp.where(lane <= srow, s_own, NEG)
        m_blk = jnp.where(sel, m_ref[...], NEG)
        m_all = jnp.maximum(jnp.max(m_blk, axis=1, keepdims=True), jnp.max(s_own, axis=1, keepdims=True))
        wgt = jnp.where(sel, jnp.exp(m_blk - m_all), 0.0)
        p_own = jnp.exp(s_own - m_all)
        l_all = jnp.sum(wgt * l_ref[...], axis=1, keepdims=True) + jnp.sum(p_own, axis=1, keepdims=True)
        o = _dot(p_own.astype(BF16), vn_ref[0].astype(BF16))
        for n in range(nbp):
            o = o + wgt[:, n:n + 1] * acc_ref[n]
        o = o / l_all
        head_lane = lax.broadcasted_iota(I32, (rows, WKV), 1) // HEAD_DIM
        head_row = lax.broadcasted_iota(I32, (rows, WKV), 0) % H_B
        o = jnp.where(head_lane == head_row, o, 0.0)
        o_ref[0] = jnp.sum(o.reshape(s_len, H_B, WKV), axis=1)


def _moba_sample(page_table, qb_s, kn_pad, vn_pad, pool_k, pool_v, pages):
    bd, n_pages = page_table.shape
    rows = qb_s.shape[1]
    s_len = rows // H_B
    nbp = n_pages * PAGE_SIZE // MOBA_BLOCK
    per_b = lambda b, j, pt: (b, 0, 0)
    grid_spec = pltpu.PrefetchScalarGridSpec(
        num_scalar_prefetch=1, grid=(bd, n_pages // pages),
        in_specs=[pl.BlockSpec((1, rows, LANE), per_b), pl.BlockSpec((1, PAGE_SIZE, WKV), per_b),
                  pl.BlockSpec((1, PAGE_SIZE, WKV), per_b)] + _page_specs((WKV, PAGE_SIZE), pages) * 2,
        out_specs=pl.BlockSpec((1, s_len, WKV), per_b),
        scratch_shapes=[pltpu.VMEM((rows, LANE), F32), pltpu.VMEM((rows, LANE), F32), pltpu.VMEM((rows, LANE), F32),
                        pltpu.VMEM((nbp, rows, WKV), F32)])
    return pl.pallas_call(
        functools.partial(_moba_sample_kernel, pages=pages, n_sel=min(TOPK_BLK, nbp)), grid_spec=grid_spec,
        out_shape=jax.ShapeDtypeStruct((bd, s_len, WKV), F32),
        compiler_params=_params("parallel", "arbitrary"), name="moba_sample",
    )(page_table, qb_s, kn_pad, vn_pad, *([pool_k] * pages), *([pool_v] * pages))


def _split_w_in(w_in):
    d = w_in.shape[0]
    sizes = (H_A * HEAD_DIM, KV_A * HEAD_DIM, KV_A * HEAD_DIM, H_IDX * D_IDX, D_IDX, H_IDX, WKV, WKV, WKV, d, d)
    parts, o = [], 0
    for s in sizes:
        parts.append(w_in[:, o:o + s])
        o += s
    return parts


def _expand_heads(w, lane_offsets):
    z = jnp.zeros((w.shape[0], HEAD_DIM), w.dtype)
    parts = []
    for h, off in enumerate(lane_offsets):
        blk = w[:, h * HEAD_DIM:(h + 1) * HEAD_DIM]
        parts += [blk, z] if off == 0 else [z, blk]
    return jnp.concatenate(parts, axis=1)


def _layout_w_in(w_in):
    d = w_in.shape[0]
    qa, ka, va, qi, ki, wi, qb, kb, vb, ga, gb = _split_w_in(w_in)
    heads_per_group = H_A // KV_A
    qa_e = _expand_heads(qa, [(h // heads_per_group) * HEAD_DIM for h in range(H_A)])
    qb_e = _expand_heads(qb, [(h % 2) * HEAD_DIM for h in range(H_B)])
    kiwi = jnp.concatenate([ki, wi, jnp.zeros((d, LANE - D_IDX - H_IDX), w_in.dtype)], axis=1)
    ki_pad = jnp.concatenate([ki, jnp.zeros((d, LANE - D_IDX), w_in.dtype)], axis=1)
    w_sample = jnp.concatenate([qa_e, _expand_heads(qi, [0] * H_IDX), qb_e, ka, va, kiwi, kb, vb, ga, gb], axis=1)
    w_t = jnp.concatenate([qa_e, qi, qb_e, ka, va, ki, kb, vb, wi, jnp.zeros_like(wi)], axis=1).T
    w_r = jnp.concatenate([ka, ki_pad, kb, ga, gb], axis=1)
    return w_sample.astype(BF16), w_t.astype(BF16), w_r.astype(BF16)


def _rope_angles(pos):
    half = ROT_DIM // 2
    inv = ROPE_THETA ** (-jnp.arange(half, dtype=F32) / half)
    ang = pos.astype(F32)[:, None] * inv[None, :]
    return jnp.cos(ang), jnp.sin(ang)


def _rope_tables(pos):
    cos, sin = _rope_angles(pos)
    n = pos.shape[0]
    pad = jnp.zeros((n, HEAD_DIM - ROT_DIM), F32)
    zero = jnp.zeros((n, ROT_DIM // 2), F32)
    c = jnp.concatenate([cos, cos, pad + 1.0], axis=1)
    s1 = jnp.concatenate([-sin, zero, pad], axis=1)
    s2 = jnp.concatenate([zero, sin, pad], axis=1)
    return tuple(jnp.tile(t, (1, LANE // HEAD_DIM)) for t in (c, s1, s2))


def _pages_transposed(pool):
    n_pool, page = pool.shape[:2]
    nd = pool.ndim
    return jnp.transpose(pool, (0,) + tuple(range(2, nd)) + (1,)).reshape(n_pool, -1, page)


def _pad_rows(x, rows):
    b, s, w = x.shape
    return jnp.concatenate([x, jnp.zeros((b, rows - s, w), x.dtype)], axis=1)


def _ffn(oa, ob, ga, gb, x2d, lw, g_final, tm_merge, tm_moe):
    x1, h2, cw = _merge(oa, ob, ga, gb, x2d, lw["wa"], lw["wb"], lw["wo"], lw["g_ffn"], lw["w_router"],
                        lw["b_router"], tm_merge)
    return _moe(h2, cw, x1, lw["wg"], lw["wu"], lw["wd"], g_final, tm_moe)


@jax.jit
def kernel(x_prompt, x_sample, cache_k_a, cache_v_a, cache_k_idx, cache_k_b, cache_v_b, page_table, w_in, w_br_a,
           w_br_b, w_out, g_mix, g_ffn, w_grp, b_grp, w_exp, b_exp, w_e_gate, w_e_up, w_e_down, g_final):
    bsz, t, d = x_prompt.shape
    bd, s_len, _ = x_sample.shape
    depth = w_in.shape[0]
    n_pages = page_table.shape[1]
    past = n_pages * PAGE_SIZE
    assert depth == 1, "the final RMSNorm is fused into the last layer's MoE kernel"
    assert past % MOBA_BLOCK == 0 and s_len <= PAGE_SIZE and t % TM_PROJ == 0

    xp = x_prompt.reshape(bsz * t, d)
    xs = x_sample.reshape(bd * s_len, d)
    ns = bd * s_len

    l = 0
    zpad = jnp.zeros((d, LANE - N_GROUPS - N_EXPERTS), F32)
    lw = dict(
        wa=w_br_a[l].astype(BF16), wb=w_br_b[l].astype(BF16), wo=w_out[l].astype(BF16), g_ffn=g_ffn[l],
        w_router=jnp.concatenate([w_grp[l], w_exp[l], zpad], axis=1),
        b_router=jnp.concatenate([b_grp[l], b_exp[l], zpad[0]]).reshape(1, LANE),
        wg=w_e_gate[l].astype(BF16), wu=w_e_up[l].astype(BF16), wd=w_e_down[l].astype(BF16))
    w_sample, w_t, w_r = _layout_w_in(w_in[l])

    pos_p = jnp.arange(t)
    cos_p, sin_p = _rope_angles(pos_p)
    (qa_t, qi_t, qb_t, ka_t, va_t, ki_t, kb_t, vb_t, wi_t, kab, kib, kbb, va_tc, vb_tc, kmean, ga, gb) = _project_prompt(
        xp, g_mix[l], w_t, w_r, (cos_p.T, sin_p.T), _rope_tables(pos_p), bsz, t)
    o_a = _dsa_prompt(qa_t, qi_t, wi_t, kib, kab, va_tc, bsz, t)
    o_b = _moba_prompt(qb_t, kmean.reshape(bsz * t // MOBA_BLOCK, WKV), kbb, vb_tc, bsz, t)
    y_prompt = _ffn(o_a, o_b, ga, gb, xp, lw, g_final, TM_MERGE, TM_MOE).reshape(bsz, t, d)

    def heads_last(x_t, heads):
        return jnp.transpose(x_t.reshape(1, bsz, heads, HEAD_DIM, t), (0, 1, 4, 2, 3))

    rows_p = (heads_last(ka_t, KV_A), heads_last(va_t, KV_A), jnp.transpose(ki_t, (0, 2, 1))[None],
              heads_last(kb_t, H_B), heads_last(vb_t, H_B))

    tabs_s = tuple(jnp.tile(tb, (bd, 1)) for tb in _rope_tables(past + jnp.arange(s_len)))
    qa, qi, qb, ka, va, kiwi, kb, vb, ga, gb = _project_sample(xs, g_mix[l], w_sample, tabs_s)
    rows = s_len * N_HEAD_SLABS
    qa_s = qa.reshape(bd, rows, LANE)
    qi_s = qi.reshape(bd, rows, LANE)
    qb_s = qb.reshape(bd, rows, LANE)
    wi_s = kiwi[:, D_IDX:D_IDX + H_IDX].reshape(bd, rows, 1)
    s_past, s_new = _dsa_sample_scores(page_table, qi_s, wi_s, _pad_rows(kiwi.reshape(bd, s_len, LANE), PAGE_SIZE),
                                       _pages_transposed(cache_k_idx[l]), min(PAGES_SCORES, n_pages))
    bias_past, bias_new = _dsa_sample_select(s_past.reshape(ns, past), s_new.reshape(ns, PAGE_SIZE),
                                             min(TOPK_IDX, (past + s_len) // 4))
    o_a = _dsa_sample_attn(page_table, qa_s, bias_past.reshape(bd, s_len, past),
                           bias_new.reshape(bd, s_len, PAGE_SIZE),
                           _pad_rows(ka.reshape(bd, s_len, LANE), PAGE_SIZE),
                           _pad_rows(va.reshape(bd, s_len, LANE), PAGE_SIZE),
                           _pages_transposed(cache_k_a[l]), _pages_transposed(cache_v_a[l]), min(PAGES_ATTN, n_pages))
    o_a = o_a.reshape(bd, s_len, H_A, KV_A, HEAD_DIM)
    hpg = H_A // KV_A
    o_a = jnp.concatenate([o_a[:, :, g * hpg:(g + 1) * hpg, g] for g in range(KV_A)], axis=2)
    o_a = o_a.reshape(ns, H_A * HEAD_DIM).astype(BF16)
    o_b = _moba_sample(page_table, qb_s, _pad_rows(kb.reshape(bd, s_len, WKV), PAGE_SIZE),
                       _pad_rows(vb.reshape(bd, s_len, WKV), PAGE_SIZE),
                       _pages_transposed(cache_k_b[l]), _pages_transposed(cache_v_b[l]), min(PAGES_MOBA, n_pages))
    o_b = o_b.reshape(ns, WKV).astype(BF16)
    y_sample = _ffn(o_a, o_b, ga, gb, xs, lw, g_final, ns, ns).reshape(bd, s_len, d)
    rows_s = (ka.reshape(1, bd, s_len, KV_A, HEAD_DIM), va.reshape(1, bd, s_len, KV_A, HEAD_DIM),
              kiwi[:, :D_IDX].reshape(1, bd, s_len, D_IDX), kb.reshape(1, bd, s_len, H_B, HEAD_DIM),
              vb.reshape(1, bd, s_len, H_B, HEAD_DIM))
    return (y_prompt, y_sample) + rows_p + rows_s
```

```python
import functools

import jax
import jax.numpy as jnp
from jax import lax
from jax.experimental import pallas as pl
from jax.experimental.pallas import tpu as pltpu

F32, BF16, I32 = jnp.float32, jnp.bfloat16, jnp.int32

HEAD_DIM = 64
ROT_DIM = HEAD_DIM // 4
ROPE_THETA = 500000.0
H_A = 8
KV_A = 2
H_IDX = 8
D_IDX = 64
TOPK_IDX = 256
H_B = 8
MOBA_BLOCK = 256
TOPK_BLK = 3
N_GROUPS = 4
EXPERTS_PER_GROUP = 8
N_EXPERTS = N_GROUPS * EXPERTS_PER_GROUP
D_EXPERT = 256
RMS_EPS = 1e-6
PAGE_SIZE = 128

LANE = 128
SUBLANE = 8
BF16_ROWS = 16
VMEM_LIMIT = 56 * 1024 * 1024

TM_PROJ = 512
DSA_Q = 256
DSA_KC = TM_PROJ
DSA_HEAD_SPLITS = 1
TM_MERGE = 512
TM_MOE = 1024
PAGES_SCORES = 32
PAGES_ATTN = 32
PAGES_MOBA = 16

NEG = -1e30
INT_MIN = -(2 ** 31)
KEY_NEG_INF = -2139095041
N_HEAD_SLABS = 8
Q_EXP = N_HEAD_SLABS * LANE
WKV = H_B * HEAD_DIM
SOFTMAX_SCALE = HEAD_DIM ** -0.5
assert SOFTMAX_SCALE == 0.125

_NT = (((1,), (1,)), ((), ()))
_HI = lax.Precision.HIGHEST


def _dot(a, b, precision=None):
    return jnp.dot(a, b, preferred_element_type=F32, precision=precision)


def _dot_nt(a, b, precision=None):
    return lax.dot_general(a, b, _NT, preferred_element_type=F32, precision=precision)


def _params(*sem):
    return pltpu.CompilerParams(dimension_semantics=sem, vmem_limit_bytes=VMEM_LIMIT)


def _rms_scale(x):
    return lax.rsqrt(jnp.mean(x * x, axis=-1, keepdims=True) + RMS_EPS)


def _rope(y, c, s1, s2):
    half = ROT_DIM // 2
    return y * c + pltpu.roll(y, LANE - half, 1) * s1 + pltpu.roll(y, half, 1) * s2


def _rope_t(y, cos_t, sin_t):
    half = ROT_DIM // 2
    parts = []
    for r in range(0, y.shape[0], HEAD_DIM):
        x1, x2 = y[r:r + half], y[r + half:r + ROT_DIM]
        parts += [x1 * cos_t - x2 * sin_t, x2 * cos_t + x1 * sin_t, y[r + ROT_DIM:r + HEAD_DIM]]
    return jnp.concatenate(parts, axis=0)


def _proj_prompt_kernel(x_ref, g_ref, wt_ref, wr_ref, cos_t_ref, sin_t_ref, c_ref, s1_ref, s2_ref,
                        qa_t_ref, qi_t_ref, qb_t_ref, ka_t_ref, va_t_ref, ki_t_ref, kb_t_ref, vb_t_ref, wi_t_ref,
                        kab_ref, kib_ref, kbb_ref, va_tc_ref, vb_tc_ref, kmean_ref, ga_ref, gb_ref):
    tm, d_model = x_ref.shape
    x = x_ref[...]
    h = ((x * _rms_scale(x)) * g_ref[...]).astype(BF16)
    cos_t, sin_t = cos_t_ref[...], sin_t_ref[...]
    c, s1, s2 = c_ref[...], s1_ref[...], s2_ref[...]

    off = [0]

    def proj_t(rows):
        y = _dot_nt(wt_ref[off[0]:off[0] + rows, :], h)
        off[0] += rows
        return y

    half_q = Q_EXP // 2
    for r0 in (0, half_q):
        qa_t_ref[0, r0:r0 + half_q, :] = _rope_t(proj_t(half_q), cos_t, sin_t).astype(BF16)
    qi_t_ref[0] = _rope_t(proj_t(H_IDX * D_IDX), cos_t, sin_t).astype(BF16)
    for r0 in (0, half_q):
        qb_t_ref[0, r0:r0 + half_q, :] = _rope_t(proj_t(half_q), cos_t, sin_t).astype(BF16)
    ka_t_ref[0] = _rope_t(proj_t(KV_A * HEAD_DIM), cos_t, sin_t)
    va_t = proj_t(KV_A * HEAD_DIM)
    va_t_ref[0] = va_t
    va_tc_ref[0] = va_t.astype(BF16)
    ki_t_ref[0] = _rope_t(proj_t(D_IDX), cos_t, sin_t)
    kb_t_ref[0] = _rope_t(proj_t(WKV), cos_t, sin_t)
    vb_t = proj_t(WKV)
    vb_t_ref[0] = vb_t
    for n in range(tm // MOBA_BLOCK):
        vb_tc_ref[n] = vb_t[:, n * MOBA_BLOCK:(n + 1) * MOBA_BLOCK].astype(BF16)
    wi_t_ref[0] = proj_t(2 * H_IDX)[:H_IDX]

    col = [0]

    def proj(width):
        y = _dot(h, wr_ref[:, col[0]:col[0] + width])
        col[0] += width
        return y

    y = proj(2 * LANE)
    kab_ref[...] = _rope(y[:, :LANE], c, s1, s2).astype(BF16)
    kib_ref[...] = _rope(y[:, LANE:], c, s1, s2)[:, :D_IDX].astype(BF16)
    y = proj(WKV)
    kb = jnp.concatenate([_rope(y[:, j * LANE:(j + 1) * LANE], c, s1, s2) for j in range(WKV // LANE)], axis=1)
    kbb_ref[...] = kb.astype(BF16)
    nblk = tm // MOBA_BLOCK
    kmean_ref[0] = jnp.sum(kb.reshape(nblk, MOBA_BLOCK, WKV), axis=1) * (1.0 / MOBA_BLOCK)
    ga_ref[...] = proj(d_model)
    gb_ref[...] = proj(d_model)


def _project_prompt(x2d, g, w_t, w_r, tabs_t, tabs, bsz, t):
    n, d = x2d.shape
    tm = TM_PROJ
    tpb = t // tm
    row = lambda i: (i, 0)
    const = lambda i: (0, 0)
    bt = lambda i: (i // tpb, 0, i % tpb)
    chunk = lambda i: (i, 0, 0)
    once = dict(pipeline_mode=pl.Buffered(1))
    in_specs = [pl.BlockSpec((tm, d), row), pl.BlockSpec((1, d), const),
                pl.BlockSpec(w_t.shape, const, **once), pl.BlockSpec(w_r.shape, const, **once)]
    in_specs += [pl.BlockSpec((ROT_DIM // 2, tm), lambda i: (0, i % tpb))] * 2
    in_specs += [pl.BlockSpec((tm, LANE), lambda i: (i % tpb, 0))] * 3
    t_feats = [(Q_EXP, BF16), (H_IDX * D_IDX, BF16), (Q_EXP, BF16), (KV_A * HEAD_DIM, F32), (KV_A * HEAD_DIM, F32),
               (D_IDX, F32), (WKV, F32), (WKV, F32), (H_IDX, F32)]
    out_shape = [jax.ShapeDtypeStruct((bsz, f, t), dt) for f, dt in t_feats]
    out_specs = [pl.BlockSpec((1, f, tm), bt) for f, _ in t_feats]
    r_feats = [(KV_A * HEAD_DIM, BF16), (D_IDX, BF16), (WKV, BF16)]
    out_shape += [jax.ShapeDtypeStruct((n, f), dt) for f, dt in r_feats]
    out_specs += [pl.BlockSpec((tm, f), row) for f, _ in r_feats]
    nblk = tm // MOBA_BLOCK
    out_shape += [jax.ShapeDtypeStruct((n // tm, KV_A * HEAD_DIM, tm), BF16),
                  jax.ShapeDtypeStruct((n // MOBA_BLOCK, WKV, MOBA_BLOCK), BF16),
                  jax.ShapeDtypeStruct((n // tm, nblk, WKV), F32),
                  jax.ShapeDtypeStruct((n, d), F32), jax.ShapeDtypeStruct((n, d), F32)]
    out_specs += [pl.BlockSpec((1, KV_A * HEAD_DIM, tm), chunk), pl.BlockSpec((nblk, WKV, MOBA_BLOCK), chunk),
                  pl.BlockSpec((1, nblk, WKV), chunk), pl.BlockSpec((tm, d), row), pl.BlockSpec((tm, d), row)]
    return pl.pallas_call(
        _proj_prompt_kernel, grid=(n // tm,), in_specs=in_specs, out_specs=out_specs, out_shape=out_shape,
        compiler_params=_params("parallel"), name="proj_prompt",
    )(x2d, g.reshape(1, d), w_t, w_r, *tabs_t, *tabs)


def _proj_sample_kernel(x_ref, g_ref, w_ref, c_ref, s1_ref, s2_ref,
                        qa_ref, qi_ref, qb_ref, ka_ref, va_ref, kiwi_ref, kb_ref, vb_ref, ga_ref, gb_ref):
    tm, d_model = x_ref.shape
    x = x_ref[...]
    h = ((x * _rms_scale(x)) * g_ref[...]).astype(BF16)
    c, s1, s2 = c_ref[...], s1_ref[...], s2_ref[...]
    off = [0]

    def proj(width):
        y = _dot(h, w_ref[:, off[0]:off[0] + width])
        off[0] += width
        return y

    def rope_slabs(y):
        return jnp.concatenate([_rope(y[:, j * LANE:(j + 1) * LANE], c, s1, s2) for j in range(y.shape[1] // LANE)],
                               axis=1)

    for q_ref in (qa_ref, qi_ref, qb_ref):
        q_ref[...] = rope_slabs(proj(Q_EXP)).astype(BF16)
    y = proj(3 * LANE)
    ka_ref[...] = _rope(y[:, :LANE], c, s1, s2)
    va_ref[...] = y[:, LANE:2 * LANE]
    is_ki = lax.broadcasted_iota(I32, (tm, LANE), 1) < D_IDX
    kiwi_ref[...] = _rope(y[:, 2 * LANE:], jnp.where(is_ki, c, 1.0), jnp.where(is_ki, s1, 0.0),
                          jnp.where(is_ki, s2, 0.0))
    kb_ref[...] = rope_slabs(proj(WKV))
    vb_ref[...] = proj(WKV)
    ga_ref[...] = proj(d_model)
    gb_ref[...] = proj(d_model)


def _project_sample(x2d, g, w_all, tabs):
    n, d = x2d.shape
    const = lambda i: (0, 0)
    widths = [(Q_EXP, BF16)] * 3 + [(LANE, F32)] * 3 + [(WKV, F32)] * 2 + [(d, F32)] * 2
    return pl.pallas_call(
        _proj_sample_kernel, grid=(1,),
        in_specs=[pl.BlockSpec((n, d), const), pl.BlockSpec((1, d), const),
                  pl.BlockSpec(w_all.shape, const, pipeline_mode=pl.Buffered(1))] + [pl.BlockSpec((n, LANE), const)] * 3,
        out_specs=[pl.BlockSpec((n, w), const) for w, _ in widths],
        out_shape=[jax.ShapeDtypeStruct((n, w), dt) for w, dt in widths],
        compiler_params=_params("arbitrary"), name="proj_sample",
    )(x2d, g.reshape(1, d), w_all, *tabs)


def _key_to_float(key):
    return pltpu.bitcast(key ^ ((key >> 31) & 0x7FFFFFFF), F32)


def _select_bias(score_ref, bias_ref, nc, k_top, n_keys_log2, key_axis):
    _, d0, d1 = score_ref.shape
    kc = (d0, d1)[key_axis]
    kidx0 = lax.broadcasted_iota(I32, (d0, d1), key_axis)
    k_top = float(k_top)
    qshape = (1, d1) if key_axis == 0 else (d0, 1)

    def fold(f):
        if key_axis == 0:
            return jnp.sum(f.reshape(SUBLANE, d0 // SUBLANE, d1), axis=0)
        part = f[:, :LANE]
        for j in range(1, d1 // LANE):
            part = part + f[:, j * LANE:(j + 1) * LANE]
        return part

    def count(pred):
        def body(c, cnt):
            return cnt + fold(jnp.where(pred(score_ref[c], c), 1.0, 0.0))
        zero = jnp.zeros((d0 // SUBLANE, d1) if key_axis == 0 else (d0, LANE), F32)
        return jnp.sum(lax.fori_loop(0, nc, body, zero), axis=key_axis, keepdims=True)

    nonneg = count(lambda s, c: s >= 0.0)
    v0 = jnp.where(nonneg >= k_top, 0, INT_MIN).astype(I32)

    def bit_body(b, v):
        cand = v | jnp.left_shift(jnp.int32(1), 30 - b)
        cf = _key_to_float(cand)
        return jnp.where(count(lambda s, c: s >= cf) >= k_top, cand, v)

    v = lax.fori_loop(0, 31, bit_body, v0)
    thr = _key_to_float(jnp.maximum(v, KEY_NEG_INF))

    at_least = count(lambda s, c: s >= thr)
    tied = (at_least > k_top) & (thr > -jnp.inf)
    all_keys = jnp.full(qshape, 2 ** n_keys_log2 - 1, I32)

    def tie_search():
        need = k_top - count(lambda s, c: s > thr)

        def tie_body(b, u):
            cand = u | jnp.left_shift(jnp.int32(1), n_keys_log2 - 1 - b)
            below = count(lambda s, c: (s == thr) & (c * kc + kidx0 < cand))
            return jnp.where(below < need, cand, u)

        return lax.fori_loop(0, n_keys_log2, tie_body, jnp.zeros(qshape, I32))

    u = lax.cond(jnp.max(jnp.where(tied, 1.0, 0.0)) > 0.0, tie_search, lambda: all_keys)

    def bias_body(c, carry):
        s = score_ref[c]
        sel = (s > -jnp.inf) & ((s > thr) | ((s == thr) & (c * kc + kidx0 <= u)))
        bias_ref[c] = jnp.where(sel, 0.0, NEG)
        return carry

    lax.fori_loop(0, nc, bias_body, 0)


def _dsa_prompt_kernel(qa_t_ref, qi_t_ref, wi_t_ref, kib_ref, kab_ref, va_tc_ref, o_ref, score_ref, bias_ref, *, k_top):
    nchunks, kc, q = score_ref.shape
    i = pl.program_id(1)
    nc = (i * q + q - 1) // kc + 1
    key0 = lax.broadcasted_iota(I32, (kc, q), 0)
    qpos = i * q + lax.broadcasted_iota(I32, (kc, q), 1)
    w = wi_t_ref[0] * (H_IDX ** -0.5 * D_IDX ** -0.5)
    qi_all = jnp.concatenate([qi_t_ref[0, h * D_IDX:(h + 1) * D_IDX, :] for h in range(H_IDX)], axis=1)

    def score_body(c, carry):
        d = _dot(kib_ref[pl.ds(pl.multiple_of(c * kc, kc), kc), :], qi_all)
        acc = jnp.zeros((kc, q), F32)
        for h in range(H_IDX):
            acc = acc + jnp.maximum(d[:, h * q:(h + 1) * q], 0.0) * w[h:h + 1, :]
        score_ref[c] = jnp.where(c * kc + key0 <= qpos, acc, -jnp.inf)
        return carry

    lax.fori_loop(0, nc, score_body, 0)
    _select_bias(score_ref, bias_ref, nc, k_top, (nchunks * kc - 1).bit_length(), key_axis=0)

    qa_all = jnp.concatenate([qa_t_ref[0, h * LANE:(h + 1) * LANE, :] for h in range(H_A)], axis=1)
    qa_all = qa_all * SOFTMAX_SCALE

    hs = H_A // DSA_HEAD_SPLITS
    q_parts = [qa_all[:, g * hs * q:(g + 1) * hs * q] for g in range(DSA_HEAD_SPLITS)]

    dv = KV_A * HEAD_DIM
    ones_rows = jnp.ones((BF16_ROWS, kc), BF16)

    def att_body(c, carry):
        k = kab_ref[pl.ds(pl.multiple_of(c * kc, kc), kc), :]
        v = jnp.concatenate([va_tc_ref[c], ones_rows], axis=0)
        b = bias_ref[c]
        out = []
        for g in range(DSA_HEAD_SPLITS):
            m, l, acc = carry[g]
            s = _dot(k, q_parts[g])
            s = jnp.concatenate([s[:, h * q:(h + 1) * q] + b for h in range(hs)], axis=1)
            m_new = jnp.maximum(m, jnp.max(s, axis=0, keepdims=True))
            alpha = jnp.exp(m - m_new)
            p = jnp.exp((s - m_new).astype(BF16))
            pv = _dot(v, p)
            l = alpha * l + pv[dv:dv + 1]
            acc = alpha * acc + pv[:dv]
            out.append((m_new, l, acc))
        return tuple(out)

    init = tuple((jnp.full((1, hs * q), NEG, F32), jnp.zeros((1, hs * q), F32),
                  jnp.zeros((KV_A * HEAD_DIM, hs * q), F32)) for _ in range(DSA_HEAD_SPLITS))
    final = lax.fori_loop(0, nc, att_body, init)
    o = jnp.concatenate([acc / l for _, l, acc in final], axis=1)
    heads_per_group = H_A // KV_A
    for j in range(H_A // 2):
        g0 = ((2 * j) // heads_per_group) * HEAD_DIM
        pair = jnp.concatenate([o[g0:g0 + HEAD_DIM, (2 * j) * q:(2 * j + 1) * q],
                                o[g0:g0 + HEAD_DIM, (2 * j + 1) * q:(2 * j + 2) * q]], axis=0)
        o_ref[:, j * LANE:(j + 1) * LANE] = pair.T.astype(BF16)


def _dsa_prompt(qa_t, qi_t, wi_t, kib, kab, va_tc, bsz, t):
    q, kc = DSA_Q, DSA_KC
    nq, nchunks = t // q, t // kc
    tile_t = lambda b, i: (b, 0, i)
    return pl.pallas_call(
        functools.partial(_dsa_prompt_kernel, k_top=min(TOPK_IDX, t // 4)),
        grid=(bsz, nq),
        in_specs=[pl.BlockSpec((1, Q_EXP, q), tile_t), pl.BlockSpec((1, H_IDX * D_IDX, q), tile_t),
                  pl.BlockSpec((1, H_IDX, q), tile_t),
                  pl.BlockSpec((t, D_IDX), lambda b, i: (b, 0)), pl.BlockSpec((t, KV_A * HEAD_DIM), lambda b, i: (b, 0)),
                  pl.BlockSpec((nchunks, KV_A * HEAD_DIM, kc), lambda b, i: (b, 0, 0))],
        out_specs=pl.BlockSpec((q, H_A * HEAD_DIM), lambda b, i: (b * nq + i, 0)),
        out_shape=jax.ShapeDtypeStruct((bsz * t, H_A * HEAD_DIM), BF16),
        scratch_shapes=[pltpu.VMEM((nchunks, kc, q), F32), pltpu.VMEM((nchunks, kc, q), F32)],
        compiler_params=_params("parallel", "arbitrary"), name="dsa_prompt",
    )(qa_t, qi_t, wi_t, kib, kab, va_tc)


def _top_blocks(gate, valid, idxf, n_sel, axis):
    gate = jnp.where(valid, gate, -jnp.inf)
    sel = jnp.zeros(gate.shape, jnp.bool_)
    for _ in range(n_sel):
        mx = jnp.max(gate, axis=axis, keepdims=True)
        first = jnp.min(jnp.where(gate == mx, idxf, 1e9), axis=axis, keepdims=True)
        pick = idxf == first
        sel = sel | (pick & valid)
        gate = jnp.where(pick, -jnp.inf, gate)
    return sel


def _moba_prompt_kernel(qb_t_ref, km_ref, kb_ref, vb_tc_ref, o_ref, *, n_sel):
    qb = o_ref.shape[0]
    nb = km_ref.shape[0]
    j = pl.program_id(1)
    r = 2 * qb
    nrow = lax.broadcasted_iota(I32, (nb, r), 0)
    nrowf = nrow.astype(F32)
    qc = lax.broadcasted_iota(I32, (qb, r), 1)
    own_mask = lax.broadcasted_iota(I32, (qb, r), 0) <= jnp.where(qc >= qb, qc - qb, qc)
    own = pl.multiple_of(j * qb, qb)

    pairs = range(H_B // 2)
    sls = [slice(p * LANE, (p + 1) * LANE) for p in pairs]
    ones_rows = jnp.ones((BF16_ROWS, MOBA_BLOCK), BF16)

    def pv_and_sum(n, p, pr):
        return _dot(jnp.concatenate([vb_tc_ref[n, sls[p], :], ones_rows], axis=0), pr)

    qes, selbiases, init = [], [], []
    for p in pairs:
        qe = jnp.concatenate([qb_t_ref[0, (2 * p) * LANE:(2 * p + 1) * LANE, :],
                              qb_t_ref[0, (2 * p + 1) * LANE:(2 * p + 2) * LANE, :]], axis=1)
        gate = _dot(km_ref[:, sls[p]], qe.astype(F32), precision=_HI)
        sel = _top_blocks(gate, nrow < j, nrowf, n_sel, axis=0)
        selbiases.append(jnp.where(sel, 0.0, NEG))
        qe = qe * SOFTMAX_SCALE
        qes.append(qe)

        s = jnp.where(own_mask, _dot(kb_ref[pl.ds(own, qb), sls[p]], qe), NEG)
        m = jnp.max(s, axis=0, keepdims=True)
        pv = pv_and_sum(j, p, jnp.exp((s - m).astype(BF16)))
        init.append((m, pv[LANE:LANE + 1], pv[:LANE]))

    half = lax.broadcasted_iota(I32, (2 * MOBA_BLOCK, r), 0) < MOBA_BLOCK

    def body(n2, carry):
        start = pl.multiple_of(n2 * (2 * MOBA_BLOCK), 2 * MOBA_BLOCK)
        out = []
        for p in pairs:
            m, l, acc = carry[p]
            rowb = [jnp.sum(jnp.where(nrow == 2 * n2 + i, selbiases[p], 0.0), axis=0, keepdims=True) for i in (0, 1)]
            s = _dot(kb_ref[pl.ds(start, 2 * MOBA_BLOCK), sls[p]], qes[p]) + jnp.where(half, rowb[0], rowb[1])
            m_new = jnp.maximum(m, jnp.max(s, axis=0, keepdims=True))
            alpha = jnp.exp(m - m_new)
            pr = jnp.exp((s - m_new).astype(BF16))
            pv = pv_and_sum(2 * n2, p, pr[:MOBA_BLOCK]) + pv_and_sum(2 * n2 + 1, p, pr[MOBA_BLOCK:])
            out.append((m_new, alpha * l + pv[LANE:LANE + 1], alpha * acc + pv[:LANE]))
        return tuple(out)

    final = lax.fori_loop(0, (j + 1) // 2, body, tuple(init))
    for p in pairs:
        _, l, acc = final[p]
        o = acc / l
        pair = jnp.concatenate([o[:HEAD_DIM, :qb], o[HEAD_DIM:, qb:]], axis=0)
        o_ref[:, sls[p]] = pair.T.astype(BF16)


def _moba_prompt(qb_t, kmean, kbb, vb_tc, bsz, t):
    nb = t // MOBA_BLOCK
    return pl.pallas_call(
        functools.partial(_moba_prompt_kernel, n_sel=min(TOPK_BLK, nb - 1)),
        grid=(bsz, nb),
        in_specs=[pl.BlockSpec((1, Q_EXP, MOBA_BLOCK), lambda b, i: (b, 0, i)), pl.BlockSpec((nb, WKV), lambda b, i: (b, 0)),
                  pl.BlockSpec((t, WKV), lambda b, i: (b, 0)),
                  pl.BlockSpec((nb, WKV, MOBA_BLOCK), lambda b, i: (b, 0, 0))],
        out_specs=pl.BlockSpec((MOBA_BLOCK, WKV), lambda b, i: (b * nb + i, 0)),
        out_shape=jax.ShapeDtypeStruct((bsz * t, WKV), BF16),
        compiler_params=_params("parallel", "arbitrary"), name="moba_prompt",
    )(qb_t, kmean, kbb, vb_tc)


def _merge_kernel(oa_ref, ob_ref, ga_ref, gb_ref, x_ref, wa_ref, wb_ref, wo_ref, gf_ref, wr_ref, br_ref,
                  x1_ref, h2_ref, cw_ref):
    tm = x_ref.shape[0]
    u = jax.nn.sigmoid(ga_ref[...]) * _dot(oa_ref[...], wa_ref[...]) \
        + jax.nn.sigmoid(gb_ref[...]) * _dot(ob_ref[...], wb_ref[...])
    x1 = x_ref[...] + _dot(u.astype(BF16), wo_ref[...])
    x1_ref[...] = x1
    h2 = (x1 * _rms_scale(x1)) * gf_ref[...]
    h2_ref[...] = h2.astype(BF16)

    rl = _dot(h2, wr_ref[...], precision=_HI) + br_ref[...]
    lanef = lax.broadcasted_iota(I32, (tm, LANE), 1).astype(F32)

    def masked_softmax(mask):
        z = jnp.where(mask, rl, -jnp.inf)
        e = jnp.exp(z - jnp.max(z, axis=1, keepdims=True))
        return e / jnp.sum(e, axis=1, keepdims=True)

    def first_lane(cond):
        return jnp.min(jnp.where(cond, lanef, 1e9), axis=1, keepdims=True)

    gmask = lanef < N_GROUPS
    gp = masked_softmax(gmask)
    gprob = jnp.max(gp, axis=1, keepdims=True)
    gsel = first_lane((gp == gprob) & gmask)
    lo = N_GROUPS + EXPERTS_PER_GROUP * gsel
    emask = (lanef >= lo) & (lanef < lo + EXPERTS_PER_GROUP)
    ep = jnp.where(emask, masked_softmax(emask), -1.0)
    p1 = jnp.max(ep, axis=1, keepdims=True)
    i1 = first_lane(ep == p1)
    ep = jnp.where(lanef == i1, -1.0, ep)
    p2 = jnp.max(ep, axis=1, keepdims=True)
    i2 = first_lane(ep == p2)
    den = p1 + p2
    cw_ref[...] = jnp.where(lanef == i1, gprob * p1 / den, 0.0) + jnp.where(lanef == i2, gprob * p2 / den, 0.0)


def _merge(oa, ob, ga, gb, x2d, wa, wb, wo, g_ffn, w_router, b_router, tm):
    n, d = x2d.shape
    row = lambda i: (i, 0)
    const = lambda i: (0, 0)
    ha = oa.shape[1]
    return pl.pallas_call(
        _merge_kernel, grid=(n // tm,),
        in_specs=[pl.BlockSpec((tm, ha), row), pl.BlockSpec((tm, ha), row), pl.BlockSpec((tm, d), row),
                  pl.BlockSpec((tm, d), row), pl.BlockSpec((tm, d), row), pl.BlockSpec((ha, d), const),
                  pl.BlockSpec((ha, d), const), pl.BlockSpec((d, d), const), pl.BlockSpec((1, d), const),
                  pl.BlockSpec((d, LANE), const), pl.BlockSpec((1, LANE), const)],
        out_specs=[pl.BlockSpec((tm, d), row), pl.BlockSpec((tm, d), row), pl.BlockSpec((tm, LANE), row)],
        out_shape=[jax.ShapeDtypeStruct((n, d), F32), jax.ShapeDtypeStruct((n, d), BF16),
                   jax.ShapeDtypeStruct((n, LANE), F32)],
        compiler_params=_params("parallel"), name="merge",
    )(oa, ob, ga, gb, x2d, wa, wb, wo, g_ffn.reshape(1, d), w_router, b_router)


def _moe_kernel(h_ref, cw_ref, x_ref, wg_ref, wu_ref, wd_ref, gfin_ref, y_ref, acc_ref):
    tm = h_ref.shape[0]
    e = pl.program_id(1)

    @pl.when(e == 0)
    def _():
        acc_ref[...] = jnp.zeros_like(acc_ref)

    h = h_ref[...]
    a = _dot(h, wg_ref[0])
    hdn = (a * jax.nn.sigmoid(a)) * _dot(h, wu_ref[0])
    y = _dot(hdn.astype(BF16), wd_ref[0])
    lane = lax.broadcasted_iota(I32, (tm, LANE), 1)
    col = jnp.sum(jnp.where(lane == e + N_GROUPS, cw_ref[...], 0.0), axis=1, keepdims=True)
    acc_ref[...] += y * col

    @pl.when(e == pl.num_programs(1) - 1)
    def _():
        out = x_ref[...] + acc_ref[...]
        y_ref[...] = (out * _rms_scale(out)) * gfin_ref[...]


def _moe(h2, cw, x1, wg, wu, wd, g_final, tm):
    n, d = x1.shape
    ne, _, de = wg.shape
    row = lambda i, e: (i, 0)
    return pl.pallas_call(
        _moe_kernel, grid=(n // tm, ne),
        in_specs=[pl.BlockSpec((tm, d), row), pl.BlockSpec((tm, LANE), row), pl.BlockSpec((tm, d), row),
                  pl.BlockSpec((1, d, de), lambda i, e: (e, 0, 0)), pl.BlockSpec((1, d, de), lambda i, e: (e, 0, 0)),
                  pl.BlockSpec((1, de, d), lambda i, e: (e, 0, 0)), pl.BlockSpec((1, d), lambda i, e: (0, 0))],
        out_specs=pl.BlockSpec((tm, d), row),
        out_shape=jax.ShapeDtypeStruct((n, d), F32),
        scratch_shapes=[pltpu.VMEM((tm, d), F32)],
        compiler_params=_params("parallel", "arbitrary"), name="moe",
    )(h2, cw, x1, wg, wu, wd, g_final.reshape(1, d))


def _page_specs(shape_tail, pages_per_step):
    zeros = (0,) * len(shape_tail)
    return [pl.BlockSpec((None,) + shape_tail,
                         functools.partial(lambda b, j, pt, t: (pt[b, j * pages_per_step + t],) + zeros, t=t))
            for t in range(pages_per_step)]


def _stream_pages(pt_ref, pools_and_bufs, sem_ref, pages):
    b, j = pl.program_id(0), pl.program_id(1)
    nj = pl.num_programs(1)
    total = pl.num_programs(0) * nj
    step = b * nj + j
    slot = step % 2

    def copy(pool_ref, buf_ref, page, sl, t):
        return pltpu.make_async_copy(pool_ref.at[page], buf_ref.at[sl, t], sem_ref.at[sl])

    def start(bb, jj, sl):
        for t in range(pages):
            page = pt_ref[bb, jj * pages + t]
            for pool_ref, buf_ref in pools_and_bufs:
                copy(pool_ref, buf_ref, page, sl, t).start()

    @pl.when(step == 0)
    def _():
        start(b, j, slot)

    @pl.when(step + 1 < total)
    def _():
        nxt = step + 1
        start(nxt // nj, nxt % nj, 1 - slot)

    for t in range(pages):
        for pool_ref, buf_ref in pools_and_bufs:
            copy(pool_ref, buf_ref, 0, slot, t).wait()
    return slot


def _dsa_sample_score_kernel(pt_ref, qi_ref, w_ref, kin_ref, pool_ref, s_ref, snew_ref, kbuf_ref, sem_ref, *, pages):
    slot = _stream_pages(pt_ref, [(pool_ref, kbuf_ref)], sem_ref, pages)
    j = pl.program_id(1)
    rows = qi_ref.shape[1]
    s_len = rows // H_IDX
    q = qi_ref[0][:, :D_IDX]
    w = w_ref[0] * (H_IDX ** -0.5 * D_IDX ** -0.5)

    def scores(d):
        d = jnp.maximum(d, 0.0) * w
        return jnp.sum(d.reshape(s_len, H_IDX, d.shape[1]), axis=1)

    k_t = jnp.concatenate([kbuf_ref[slot, t].astype(BF16) for t in range(pages)], axis=1)
    s_ref[0] = scores(_dot(q, k_t))

    @pl.when(j == pl.num_programs(1) - 1)
    def _():
        sn = scores(_dot_nt(q, kin_ref[0][:, :D_IDX].astype(BF16)))
        si = lax.broadcasted_iota(I32, (s_len, PAGE_SIZE), 0)
        ki = lax.broadcasted_iota(I32, (s_len, PAGE_SIZE), 1)
        snew_ref[0] = jnp.where(ki <= si, sn, -jnp.inf)


def _dsa_sample_scores(page_table, qi_s, wi_s, kin_pad, pool_kidx, pages):
    bd, n_pages = page_table.shape
    rows = qi_s.shape[1]
    s_len = rows // H_IDX
    per_b = lambda b, j, pt: (b, 0, 0)
    grid_spec = pltpu.PrefetchScalarGridSpec(
        num_scalar_prefetch=1, grid=(bd, n_pages // pages),
        in_specs=[pl.BlockSpec((1, rows, LANE), per_b), pl.BlockSpec((1, rows, 1), per_b),
                  pl.BlockSpec((1, PAGE_SIZE, LANE), per_b), pl.BlockSpec(memory_space=pl.ANY)],
        out_specs=[pl.BlockSpec((1, s_len, pages * PAGE_SIZE), lambda b, j, pt: (b, 0, j)),
                   pl.BlockSpec((1, s_len, PAGE_SIZE), per_b)],
        scratch_shapes=[pltpu.VMEM((2, pages, D_IDX, PAGE_SIZE), F32), pltpu.SemaphoreType.DMA((2,))])
    return pl.pallas_call(
        functools.partial(_dsa_sample_score_kernel, pages=pages), grid_spec=grid_spec,
        out_shape=[jax.ShapeDtypeStruct((bd, s_len, n_pages * PAGE_SIZE), F32),
                   jax.ShapeDtypeStruct((bd, s_len, PAGE_SIZE), F32)],
        compiler_params=_params("arbitrary", "arbitrary"), name="dsa_sample_scores",
    )(page_table, qi_s, wi_s, kin_pad, pool_kidx)


def _dsa_sample_select_kernel(sp_ref, sn_ref, bp_ref, bn_ref, score_ref, bias_ref, *, k_top):
    nch, q, kc = score_ref.shape
    npast = nch - 1

    def load(c, carry):
        score_ref[c] = sp_ref[:, pl.ds(pl.multiple_of(c * kc, kc), kc)]
        return carry

    lax.fori_loop(0, npast, load, 0)
    score_ref[npast] = jnp.concatenate([sn_ref[...], jnp.full((q, kc - LANE), -jnp.inf, F32)], axis=1)
    _select_bias(score_ref, bias_ref, nch, k_top, (nch * kc - 1).bit_length(), key_axis=1)

    def store(c, carry):
        bp_ref[:, pl.ds(pl.multiple_of(c * kc, kc), kc)] = bias_ref[c]
        return carry

    lax.fori_loop(0, npast, store, 0)
    bn_ref[...] = bias_ref[npast][:, :LANE]


def _dsa_sample_select(s_past, s_new, k_top):
    q, past = s_past.shape
    kc = 512
    nch = past // kc + 1
    return pl.pallas_call(
        functools.partial(_dsa_sample_select_kernel, k_top=k_top),
        out_shape=[jax.ShapeDtypeStruct((q, past), F32), jax.ShapeDtypeStruct((q, LANE), F32)],
        scratch_shapes=[pltpu.VMEM((nch, q, kc), F32), pltpu.VMEM((nch, q, kc), F32)],
        compiler_params=pltpu.CompilerParams(vmem_limit_bytes=VMEM_LIMIT), name="dsa_sample_select",
    )(s_past, s_new)


def _dsa_sample_attn_kernel(pt_ref, qa_ref, bp_ref, bn_ref, kn_ref, vn_ref, kpool_ref, vpool_ref, o_ref,
                            m_ref, l_ref, acc_ref, kbuf_ref, vbuf_ref, sem_ref, *, pages):
    slot = _stream_pages(pt_ref, [(kpool_ref, kbuf_ref), (vpool_ref, vbuf_ref)], sem_ref, pages)
    j = pl.program_id(1)
    rows = qa_ref.shape[1]
    s_len = rows // H_A
    q = qa_ref[0]
    scale = HEAD_DIM ** -0.5

    @pl.when(j == 0)
    def _():
        m_ref[...] = jnp.full_like(m_ref, NEG)
        l_ref[...] = jnp.zeros_like(l_ref)
        acc_ref[...] = jnp.zeros_like(acc_ref)

    def update(s, bias, v, v_dot):
        n = s.shape[1]
        s = (s.reshape(s_len, H_A, n) * scale + bias[:, None, :]).reshape(rows, n)
        m = m_ref[...]
        m_new = jnp.maximum(m, jnp.max(s, axis=1, keepdims=True))
        alpha = jnp.exp(m - m_new)
        p = jnp.exp(s - m_new)
        l_ref[...] = alpha * l_ref[...] + jnp.sum(p, axis=1, keepdims=True)
        acc_ref[...] = alpha * acc_ref[...] + v_dot(p.astype(BF16), v)
        m_ref[...] = m_new

    k_t = jnp.concatenate([kbuf_ref[slot, t].astype(BF16) for t in range(pages)], axis=1)
    v_t = jnp.concatenate([vbuf_ref[slot, t].astype(BF16) for t in range(pages)], axis=1)
    update(_dot(q, k_t), bp_ref[0], v_t, _dot_nt)

    @pl.when(j == pl.num_programs(1) - 1)
    def _():
        update(_dot_nt(q, kn_ref[0].astype(BF16)), bn_ref[0], vn_ref[0].astype(BF16), _dot)
        o_ref[0] = acc_ref[...] / l_ref[...]


def _dsa_sample_attn(page_table, qa_s, bias_past, bias_new, kn_pad, vn_pad, pool_k, pool_v, pages):
    bd, n_pages = page_table.shape
    rows = qa_s.shape[1]
    s_len = rows // H_A
    per_b = lambda b, j, pt: (b, 0, 0)
    grid_spec = pltpu.PrefetchScalarGridSpec(
        num_scalar_prefetch=1, grid=(bd, n_pages // pages),
        in_specs=[pl.BlockSpec((1, rows, LANE), per_b),
                  pl.BlockSpec((1, s_len, pages * PAGE_SIZE), lambda b, j, pt: (b, 0, j)),
                  pl.BlockSpec((1, s_len, LANE), per_b),
                  pl.BlockSpec((1, PAGE_SIZE, LANE), per_b), pl.BlockSpec((1, PAGE_SIZE, LANE), per_b),
                  pl.BlockSpec(memory_space=pl.ANY), pl.BlockSpec(memory_space=pl.ANY)],
        out_specs=pl.BlockSpec((1, rows, LANE), per_b),
        scratch_shapes=[pltpu.VMEM((rows, 1), F32), pltpu.VMEM((rows, 1), F32), pltpu.VMEM((rows, LANE), F32),
                        pltpu.VMEM((2, pages, KV_A * HEAD_DIM, PAGE_SIZE), F32),
                        pltpu.VMEM((2, pages, KV_A * HEAD_DIM, PAGE_SIZE), F32), pltpu.SemaphoreType.DMA((2,))])
    return pl.pallas_call(
        functools.partial(_dsa_sample_attn_kernel, pages=pages), grid_spec=grid_spec,
        out_shape=jax.ShapeDtypeStruct((bd, rows, LANE), F32),
        compiler_params=_params("arbitrary", "arbitrary"), name="dsa_sample_attn",
    )(page_table, qa_s, bias_past, bias_new, kn_pad, vn_pad, pool_k, pool_v)


def _moba_sample_kernel(pt_ref, qb_ref, kn_ref, vn_ref, *refs, pages, n_sel):
    k_refs, v_refs = refs[:pages], refs[pages:2 * pages]
    o_ref, gate_ref, m_ref, l_ref, acc_ref = refs[2 * pages:]
    del pt_ref
    j = pl.program_id(1)
    rows = qb_ref.shape[1]
    s_len = rows // H_B
    nbp = acc_ref.shape[0]
    ppb = MOBA_BLOCK // PAGE_SIZE
    scale = HEAD_DIM ** -0.5

    qs = qb_ref[0]
    pair = (lax.broadcasted_iota(I32, (rows, LANE), 0) % H_B) // 2
    qe = jnp.concatenate([jnp.where(pair == p, qs, jnp.zeros_like(qs)) for p in range(H_B // 2)], axis=1)
    lane = lax.broadcasted_iota(I32, (rows, LANE), 1)

    @pl.when(j == 0)
    def _():
        gate_ref[...] = jnp.full_like(gate_ref, -jnp.inf)
        m_ref[...] = jnp.full_like(m_ref, NEG)
        l_ref[...] = jnp.zeros_like(l_ref)

    nblk = pages // ppb
    blocks = [slice(t * MOBA_BLOCK, (t + 1) * MOBA_BLOCK) for t in range(nblk)]
    k_t = jnp.concatenate([k_refs[t][...].astype(BF16) for t in range(pages)], axis=1)
    qk = _dot(qe, k_t)
    s = qk * scale
    gate_t = [jnp.sum(qk[:, blk], axis=1, keepdims=True) * (1.0 / MOBA_BLOCK) for blk in blocks]
    m_t = [jnp.max(s[:, blk], axis=1, keepdims=True) for blk in blocks]
    p = jnp.exp(s - jnp.concatenate([jnp.broadcast_to(m, (rows, MOBA_BLOCK)) for m in m_t], axis=1))
    l_t = [jnp.sum(p[:, blk], axis=1, keepdims=True) for blk in blocks]
    pb = p.astype(BF16)
    gate, m_blk, l_blk = gate_ref[...], m_ref[...], l_ref[...]
    for t in range(nblk):
        n = j * nblk + t
        v_t = jnp.concatenate([v_refs[ppb * t + i][...].astype(BF16) for i in range(ppb)], axis=1)
        acc_ref[n] = _dot_nt(pb[:, blocks[t]], v_t)
        here = lane == n
        gate = jnp.where(here, gate_t[t], gate)
        m_blk = jnp.where(here, m_t[t], m_blk)
        l_blk = jnp.where(here, l_t[t], l_blk)
    gate_ref[...] = gate
    m_ref[...] = m_blk
    l_ref[...] = l_blk

    @pl.when(j == pl.num_programs(1) - 1)
    def _():
        sel = _top_blocks(gate_ref[...], lane < nbp, lane.astype(F32), n_sel, axis=1)
        s_own = _dot_nt(qe, kn_ref[0].astype(BF16)) * scale
        srow = lax.broadcasted_iota(I32, (rows, LANE), 0) // H_B
        s_own = jnp.where(lane <= srow, s_own, NEG)
        m_blk = jnp.where(sel, m_ref[...], NEG)
        m_all = jnp.maximum(jnp.max(m_blk, axis=1, keepdims=True), jnp.max(s_own, axis=1, keepdims=True))
        wgt = jnp.where(sel, jnp.exp(m_blk - m_all), 0.0)
        p_own = jnp.exp(s_own - m_all)
        l_all = jnp.sum(wgt * l_ref[...], axis=1, keepdims=True) + jnp.sum(p_own, axis=1, keepdims=True)
        o = _dot(p_own.astype(BF16), vn_ref[0].astype(BF16))
        for n in range(nbp):
            o = o + wgt[:, n:n + 1] * acc_ref[n]
        o = o / l_all
        head_lane = lax.broadcasted_iota(I32, (rows, WKV), 1) // HEAD_DIM
        head_row = lax.broadcasted_iota(I32, (rows, WKV), 0) % H_B
        o = jnp.where(head_lane == head_row, o, 0.0)
        o_ref[0] = jnp.sum(o.reshape(s_len, H_B, WKV), axis=1)


def _moba_sample(page_table, qb_s, kn_pad, vn_pad, pool_k, pool_v, pages):
    bd, n_pages = page_table.shape
    rows = qb_s.shape[1]
    s_len = rows // H_B
    nbp = n_pages * PAGE_SIZE // MOBA_BLOCK
    per_b = lambda b, j, pt: (b, 0, 0)
    grid_spec = pltpu.PrefetchScalarGridSpec(
        num_scalar_prefetch=1, grid=(bd, n_pages // pages),
        in_specs=[pl.BlockSpec((1, rows, LANE), per_b), pl.BlockSpec((1, PAGE_SIZE, WKV), per_b),
                  pl.BlockSpec((1, PAGE_SIZE, WKV), per_b)] + _page_specs((WKV, PAGE_SIZE), pages) * 2,
        out_specs=pl.BlockSpec((1, s_len, WKV), per_b),
        scratch_shapes=[pltpu.VMEM((rows, LANE), F32), pltpu.VMEM((rows, LANE), F32), pltpu.VMEM((rows, LANE), F32),
                        pltpu.VMEM((nbp, rows, WKV), F32)])
    return pl.pallas_call(
        functools.partial(_moba_sample_kernel, pages=pages, n_sel=min(TOPK_BLK, nbp)), grid_spec=grid_spec,
        out_shape=jax.ShapeDtypeStruct((bd, s_len, WKV), F32),
        compiler_params=_params("parallel", "arbitrary"), name="moba_sample",
    )(page_table, qb_s, kn_pad, vn_pad, *([pool_k] * pages), *([pool_v] * pages))


def _split_w_in(w_in):
    d = w_in.shape[0]
    sizes = (H_A * HEAD_DIM, KV_A * HEAD_DIM, KV_A * HEAD_DIM, H_IDX * D_IDX, D_IDX, H_IDX, WKV, WKV, WKV, d, d)
    parts, o = [], 0
    for s in sizes:
        parts.append(w_in[:, o:o + s])
        o += s
    return parts


def _expand_heads(w, lane_offsets):
    z = jnp.zeros((w.shape[0], HEAD_DIM), w.dtype)
    parts = []
    for h, off in enumerate(lane_offsets):
        blk = w[:, h * HEAD_DIM:(h + 1) * HEAD_DIM]
        parts += [blk, z] if off == 0 else [z, blk]
    return jnp.concatenate(parts, axis=1)


def _layout_w_in(w_in):
    d = w_in.shape[0]
    qa, ka, va, qi, ki, wi, qb, kb, vb, ga, gb = _split_w_in(w_in)
    heads_per_group = H_A // KV_A
    qa_e = _expand_heads(qa, [(h // heads_per_group) * HEAD_DIM for h in range(H_A)])
    qb_e = _expand_heads(qb, [(h % 2) * HEAD_DIM for h in range(H_B)])
    kiwi = jnp.concatenate([ki, wi, jnp.zeros((d, LANE - D_IDX - H_IDX), w_in.dtype)], axis=1)
    ki_pad = jnp.concatenate([ki, jnp.zeros((d, LANE - D_IDX), w_in.dtype)], axis=1)
    w_sample = jnp.concatenate([qa_e, _expand_heads(qi, [0] * H_IDX), qb_e, ka, va, kiwi, kb, vb, ga, gb], axis=1)
    w_t = jnp.concatenate([qa_e, qi, qb_e, ka, va, ki, kb, vb, wi, jnp.zeros_like(wi)], axis=1).T
    w_r = jnp.concatenate([ka, ki_pad, kb, ga, gb], axis=1)
    return w_sample.astype(BF16), w_t.astype(BF16), w_r.astype(BF16)


def _rope_angles(pos):
    half = ROT_DIM // 2
    inv = ROPE_THETA ** (-jnp.arange(half, dtype=F32) / half)
    ang = pos.astype(F32)[:, None] * inv[None, :]
    return jnp.cos(ang), jnp.sin(ang)


def _rope_tables(pos):
    cos, sin = _rope_angles(pos)
    n = pos.shape[0]
    pad = jnp.zeros((n, HEAD_DIM - ROT_DIM), F32)
    zero = jnp.zeros((n, ROT_DIM // 2), F32)
    c = jnp.concatenate([cos, cos, pad + 1.0], axis=1)
    s1 = jnp.concatenate([-sin, zero, pad], axis=1)
    s2 = jnp.concatenate([zero, sin, pad], axis=1)
    return tuple(jnp.tile(t, (1, LANE // HEAD_DIM)) for t in (c, s1, s2))


def _pages_transposed(pool):
    n_pool, page = pool.shape[:2]
    nd = pool.ndim
    return jnp.transpose(pool, (0,) + tuple(range(2, nd)) + (1,)).reshape(n_pool, -1, page)


def _pad_rows(x, rows):
    b, s, w = x.shape
    return jnp.concatenate([x, jnp.zeros((b, rows - s, w), x.dtype)], axis=1)


def _ffn(oa, ob, ga, gb, x2d, lw, g_final, tm_merge, tm_moe):
    x1, h2, cw = _merge(oa, ob, ga, gb, x2d, lw["wa"], lw["wb"], lw["wo"], lw["g_ffn"], lw["w_router"],
                        lw["b_router"], tm_merge)
    return _moe(h2, cw, x1, lw["wg"], lw["wu"], lw["wd"], g_final, tm_moe)


@jax.jit
def kernel(x_prompt, x_sample, cache_k_a, cache_v_a, cache_k_idx, cache_k_b, cache_v_b, page_table, w_in, w_br_a,
           w_br_b, w_out, g_mix, g_ffn, w_grp, b_grp, w_exp, b_exp, w_e_gate, w_e_up, w_e_down, g_final):
    bsz, t, d = x_prompt.shape
    bd, s_len, _ = x_sample.shape
    depth = w_in.shape[0]
    n_pages = page_table.shape[1]
    past = n_pages * PAGE_SIZE
    assert depth == 1, "the final RMSNorm is fused into the last layer's MoE kernel"
    assert past % MOBA_BLOCK == 0 and s_len <= PAGE_SIZE and t % TM_PROJ == 0

    xp = x_prompt.reshape(bsz * t, d)
    xs = x_sample.reshape(bd * s_len, d)
    ns = bd * s_len

    l = 0
    zpad = jnp.zeros((d, LANE - N_GROUPS - N_EXPERTS), F32)
    lw = dict(
        wa=w_br_a[l].astype(BF16), wb=w_br_b[l].astype(BF16), wo=w_out[l].astype(BF16), g_ffn=g_ffn[l],
        w_router=jnp.concatenate([w_grp[l], w_exp[l], zpad], axis=1),
        b_router=jnp.concatenate([b_grp[l], b_exp[l], zpad[0]]).reshape(1, LANE),
        wg=w_e_gate[l].astype(BF16), wu=w_e_up[l].astype(BF16), wd=w_e_down[l].astype(BF16))
    w_sample, w_t, w_r = _layout_w_in(w_in[l])

    pos_p = jnp.arange(t)
    cos_p, sin_p = _rope_angles(pos_p)
    (qa_t, qi_t, qb_t, ka_t, va_t, ki_t, kb_t, vb_t, wi_t, kab, kib, kbb, va_tc, vb_tc, kmean, ga, gb) = _project_prompt(
        xp, g_mix[l], w_t, w_r, (cos_p.T, sin_p.T), _rope_tables(pos_p), bsz, t)
    o_a = _dsa_prompt(qa_t, qi_t, wi_t, kib, kab, va_tc, bsz, t)
    o_b = _moba_prompt(qb_t, kmean.reshape(bsz * t // MOBA_BLOCK, WKV), kbb, vb_tc, bsz, t)
    y_prompt = _ffn(o_a, o_b, ga, gb, xp, lw, g_final, TM_MERGE, TM_MOE).reshape(bsz, t, d)

    def heads_last(x_t, heads):
        return jnp.transpose(x_t.reshape(1, bsz, heads, HEAD_DIM, t), (0, 1, 4, 2, 3))

    rows_p = (heads_last(ka_t, KV_A), heads_last(va_t, KV_A), jnp.transpose(ki_t, (0, 2, 1))[None],
              heads_last(kb_t, H_B), heads_last(vb_t, H_B))

    tabs_s = tuple(jnp.tile(tb, (bd, 1)) for tb in _rope_tables(past + jnp.arange(s_len)))
    qa, qi, qb, ka, va, kiwi, kb, vb, ga, gb = _project_sample(xs, g_mix[l], w_sample, tabs_s)
    rows = s_len * N_HEAD_SLABS
    qa_s = qa.reshape(bd, rows, LANE)
    qi_s = qi.reshape(bd, rows, LANE)
    qb_s = qb.reshape(bd, rows, LANE)
    wi_s = kiwi[:, D_IDX:D_IDX + H_IDX].reshape(bd, rows, 1)
    s_past, s_new = _dsa_sample_scores(page_table, qi_s, wi_s, _pad_rows(kiwi.reshape(bd, s_len, LANE), PAGE_SIZE),
                                       _pages_transposed(cache_k_idx[l]), min(PAGES_SCORES, n_pages))
    bias_past, bias_new = _dsa_sample_select(s_past.reshape(ns, past), s_new.reshape(ns, PAGE_SIZE),
                                             min(TOPK_IDX, (past + s_len) // 4))
    o_a = _dsa_sample_attn(page_table, qa_s, bias_past.reshape(bd, s_len, past),
                           bias_new.reshape(bd, s_len, PAGE_SIZE),
                           _pad_rows(ka.reshape(bd, s_len, LANE), PAGE_SIZE),
                           _pad_rows(va.reshape(bd, s_len, LANE), PAGE_SIZE),
                           _pages_transposed(cache_k_a[l]), _pages_transposed(cache_v_a[l]), min(PAGES_ATTN, n_pages))
    o_a = o_a.reshape(bd, s_len, H_A, KV_A, HEAD_DIM)
    hpg = H_A // KV_A
    o_a = jnp.concatenate([o_a[:, :, g * hpg:(g + 1) * hpg, g] for g in range(KV_A)], axis=2)
    o_a = o_a.reshape(ns, H_A * HEAD_DIM).astype(BF16)
    o_b = _moba_sample(page_table, qb_s, _pad_rows(kb.reshape(bd, s_len, WKV), PAGE_SIZE),
                       _pad_rows(vb.reshape(bd, s_len, WKV), PAGE_SIZE),
                       _pages_transposed(cache_k_b[l]), _pages_transposed(cache_v_b[l]), min(PAGES_MOBA, n_pages))
    o_b = o_b.reshape(ns, WKV).astype(BF16)
    y_sample = _ffn(o_a, o_b, ga, gb, xs, lw, g_final, ns, ns).reshape(bd, s_len, d)
    rows_s = (ka.reshape(1, bd, s_len, KV_A, HEAD_DIM), va.reshape(1, bd, s_len, KV_A, HEAD_DIM),
              kiwi[:, :D_IDX].reshape(1, bd, s_len, D_IDX), kb.reshape(1, bd, s_len, H_B, HEAD_DIM),
              vb.reshape(1, bd, s_len, H_B, HEAD_DIM))
    return (y_prompt, y_sample) + rows_p + rows_s
```

```python
import functools

import jax
import jax.numpy as jnp
from jax import lax
from jax.experimental import pallas as pl
from jax.experimental.pallas import tpu as pltpu

F32, BF16, I32 = jnp.float32, jnp.bfloat16, jnp.int32

HEAD_DIM = 64
ROT_DIM = HEAD_DIM // 4
ROPE_THETA = 500000.0
H_A = 8
KV_A = 2
H_IDX = 8
D_IDX = 64
TOPK_IDX = 256
H_B = 8
MOBA_BLOCK = 256
TOPK_BLK = 3
N_GROUPS = 4
EXPERTS_PER_GROUP = 8
N_EXPERTS = N_GROUPS * EXPERTS_PER_GROUP
D_EXPERT = 256
RMS_EPS = 1e-6
PAGE_SIZE = 128

LANE = 128
SUBLANE = 8
BF16_ROWS = 16
VMEM_LIMIT = 56 * 1024 * 1024

TM_PROJ = 512
DSA_Q = 256
DSA_KC = TM_PROJ
DSA_HEAD_SPLITS = 1
TM_MERGE = 512
TM_MOE = 1024
TM_ROWS = 512
ROW_DMA_UNROLL = 8
PAGES_SCORES = 32
PAGES_ATTN = 32
PAGES_MOBA = 16

NEG = -1e30
INT_MIN = -(2 ** 31)
KEY_NEG_INF = -2139095041
N_HEAD_SLABS = 8
Q_EXP = N_HEAD_SLABS * LANE
WKV = H_B * HEAD_DIM
SOFTMAX_SCALE = HEAD_DIM ** -0.5
assert SOFTMAX_SCALE == 0.125

_NT = (((1,), (1,)), ((), ()))
_HI = lax.Precision.HIGHEST


def _dot(a, b, precision=None):
    return jnp.dot(a, b, preferred_element_type=F32, precision=precision)


def _dot_nt(a, b, precision=None):
    return lax.dot_general(a, b, _NT, preferred_element_type=F32, precision=precision)


def _params(*sem):
    return pltpu.CompilerParams(dimension_semantics=sem, vmem_limit_bytes=VMEM_LIMIT)


def _rms_scale(x):
    return lax.rsqrt(jnp.mean(x * x, axis=-1, keepdims=True) + RMS_EPS)


def _rope(y, c, s1, s2):
    half = ROT_DIM // 2
    return y * c + pltpu.roll(y, LANE - half, 1) * s1 + pltpu.roll(y, half, 1) * s2


def _rope_t(y, cos_t, sin_t):
    half = ROT_DIM // 2
    parts = []
    for r in range(0, y.shape[0], HEAD_DIM):
        x1, x2 = y[r:r + half], y[r + half:r + ROT_DIM]
        parts += [x1 * cos_t - x2 * sin_t, x2 * cos_t + x1 * sin_t, y[r + ROT_DIM:r + HEAD_DIM]]
    return jnp.concatenate(parts, axis=0)


def _proj_prompt_kernel(x_ref, g_ref, wt_ref, wr_ref, cos_t_ref, sin_t_ref, c_ref, s1_ref, s2_ref,
                        qa_t_ref, qi_t_ref, qb_t_ref, ka_t_ref, va_t_ref, ki_t_ref, kb_t_ref, vb_t_ref, wi_t_ref,
                        kab_ref, kib_ref, kbb_ref, va_tc_ref, vb_tc_ref, kmean_ref, ga_ref, gb_ref):
    tm, d_model = x_ref.shape
    x = x_ref[...]
    h = ((x * _rms_scale(x)) * g_ref[...]).astype(BF16)
    cos_t, sin_t = cos_t_ref[...], sin_t_ref[...]
    c, s1, s2 = c_ref[...], s1_ref[...], s2_ref[...]

    off = [0]

    def proj_t(rows):
        y = _dot_nt(wt_ref[off[0]:off[0] + rows, :], h)
        off[0] += rows
        return y

    half_q = Q_EXP // 2
    for r0 in (0, half_q):
        qa_t_ref[0, r0:r0 + half_q, :] = _rope_t(proj_t(half_q), cos_t, sin_t).astype(BF16)
    qi_t_ref[0] = _rope_t(proj_t(H_IDX * D_IDX), cos_t, sin_t).astype(BF16)
    for r0 in (0, half_q):
        qb_t_ref[0, r0:r0 + half_q, :] = _rope_t(proj_t(half_q), cos_t, sin_t).astype(BF16)
    ka_t_ref[0] = _rope_t(proj_t(KV_A * HEAD_DIM), cos_t, sin_t)
    va_t = proj_t(KV_A * HEAD_DIM)
    va_t_ref[0] = va_t
    va_tc_ref[0] = va_t.astype(BF16)
    ki_t_ref[0] = _rope_t(proj_t(D_IDX), cos_t, sin_t)
    kb_t_ref[0] = _rope_t(proj_t(WKV), cos_t, sin_t)
    vb_t = proj_t(WKV)
    vb_t_ref[0] = vb_t
    for n in range(tm // MOBA_BLOCK):
        vb_tc_ref[n] = vb_t[:, n * MOBA_BLOCK:(n + 1) * MOBA_BLOCK].astype(BF16)
    wi_t_ref[0] = proj_t(2 * H_IDX)[:H_IDX]

    col = [0]

    def proj(width):
        y = _dot(h, wr_ref[:, col[0]:col[0] + width])
        col[0] += width
        return y

    y = proj(2 * LANE)
    kab_ref[...] = _rope(y[:, :LANE], c, s1, s2).astype(BF16)
    kib_ref[...] = _rope(y[:, LANE:], c, s1, s2)[:, :D_IDX].astype(BF16)
    y = proj(WKV)
    kb = jnp.concatenate([_rope(y[:, j * LANE:(j + 1) * LANE], c, s1, s2) for j in range(WKV // LANE)], axis=1)
    kbb_ref[...] = kb.astype(BF16)
    nblk = tm // MOBA_BLOCK
    kmean_ref[0] = jnp.sum(kb.reshape(nblk, MOBA_BLOCK, WKV), axis=1) * (1.0 / MOBA_BLOCK)
    ga_ref[...] = proj(d_model)
    gb_ref[...] = proj(d_model)


def _project_prompt(x2d, g, w_t, w_r, tabs_t, tabs, bsz, t):
    n, d = x2d.shape
    tm = TM_PROJ
    tpb = t // tm
    row = lambda i: (i, 0)
    const = lambda i: (0, 0)
    bt = lambda i: (i // tpb, 0, i % tpb)
    chunk = lambda i: (i, 0, 0)
    once = dict(pipeline_mode=pl.Buffered(1))
    in_specs = [pl.BlockSpec((tm, d), row), pl.BlockSpec((1, d), const),
                pl.BlockSpec(w_t.shape, const, **once), pl.BlockSpec(w_r.shape, const, **once)]
    in_specs += [pl.BlockSpec((ROT_DIM // 2, tm), lambda i: (0, i % tpb))] * 2
    in_specs += [pl.BlockSpec((tm, LANE), lambda i: (i % tpb, 0))] * 3
    t_feats = [(Q_EXP, BF16), (H_IDX * D_IDX, BF16), (Q_EXP, BF16), (KV_A * HEAD_DIM, F32), (KV_A * HEAD_DIM, F32),
               (D_IDX, F32), (WKV, F32), (WKV, F32), (H_IDX, F32)]
    out_shape = [jax.ShapeDtypeStruct((bsz, f, t), dt) for f, dt in t_feats]
    out_specs = [pl.BlockSpec((1, f, tm), bt) for f, _ in t_feats]
    r_feats = [(KV_A * HEAD_DIM, BF16), (D_IDX, BF16), (WKV, BF16)]
    out_shape += [jax.ShapeDtypeStruct((n, f), dt) for f, dt in r_feats]
    out_specs += [pl.BlockSpec((tm, f), row) for f, _ in r_feats]
    nblk = tm // MOBA_BLOCK
    out_shape += [jax.ShapeDtypeStruct((n // tm, KV_A * HEAD_DIM, tm), BF16),
                  jax.ShapeDtypeStruct((n // MOBA_BLOCK, WKV, MOBA_BLOCK), BF16),
                  jax.ShapeDtypeStruct((n // tm, nblk, WKV), F32),
                  jax.ShapeDtypeStruct((n, d), F32), jax.ShapeDtypeStruct((n, d), F32)]
    out_specs += [pl.BlockSpec((1, KV_A * HEAD_DIM, tm), chunk), pl.BlockSpec((nblk, WKV, MOBA_BLOCK), chunk),
                  pl.BlockSpec((1, nblk, WKV), chunk), pl.BlockSpec((tm, d), row), pl.BlockSpec((tm, d), row)]
    return pl.pallas_call(
        _proj_prompt_kernel, grid=(n // tm,), in_specs=in_specs, out_specs=out_specs, out_shape=out_shape,
        compiler_params=_params("parallel"), name="proj_prompt",
    )(x2d, g.reshape(1, d), w_t, w_r, *tabs_t, *tabs)


def _proj_sample_kernel(x_ref, g_ref, w_ref, c_ref, s1_ref, s2_ref,
                        qa_ref, qi_ref, qb_ref, ka_ref, va_ref, kiwi_ref, kb_ref, vb_ref, ga_ref, gb_ref):
    tm, d_model = x_ref.shape
    x = x_ref[...]
    h = ((x * _rms_scale(x)) * g_ref[...]).astype(BF16)
    c, s1, s2 = c_ref[...], s1_ref[...], s2_ref[...]
    off = [0]

    def proj(width):
        y = _dot(h, w_ref[:, off[0]:off[0] + width])
        off[0] += width
        return y

    def rope_slabs(y):
        return jnp.concatenate([_rope(y[:, j * LANE:(j + 1) * LANE], c, s1, s2) for j in range(y.shape[1] // LANE)],
                               axis=1)

    for q_ref in (qa_ref, qi_ref, qb_ref):
        q_ref[...] = rope_slabs(proj(Q_EXP)).astype(BF16)
    y = proj(3 * LANE)
    ka_ref[...] = _rope(y[:, :LANE], c, s1, s2)
    va_ref[...] = y[:, LANE:2 * LANE]
    is_ki = lax.broadcasted_iota(I32, (tm, LANE), 1) < D_IDX
    kiwi_ref[...] = _rope(y[:, 2 * LANE:], jnp.where(is_ki, c, 1.0), jnp.where(is_ki, s1, 0.0),
                          jnp.where(is_ki, s2, 0.0))
    kb_ref[...] = rope_slabs(proj(WKV))
    vb_ref[...] = proj(WKV)
    ga_ref[...] = proj(d_model)
    gb_ref[...] = proj(d_model)


def _project_sample(x2d, g, w_all, tabs):
    n, d = x2d.shape
    const = lambda i: (0, 0)
    widths = [(Q_EXP, BF16)] * 3 + [(LANE, F32)] * 3 + [(WKV, F32)] * 2 + [(d, F32)] * 2
    return pl.pallas_call(
        _proj_sample_kernel, grid=(1,),
        in_specs=[pl.BlockSpec((n, d), const), pl.BlockSpec((1, d), const),
                  pl.BlockSpec(w_all.shape, const, pipeline_mode=pl.Buffered(1))] + [pl.BlockSpec((n, LANE), const)] * 3,
        out_specs=[pl.BlockSpec((n, w), const) for w, _ in widths],
        out_shape=[jax.ShapeDtypeStruct((n, w), dt) for w, dt in widths],
        compiler_params=_params("arbitrary"), name="proj_sample",
    )(x2d, g.reshape(1, d), w_all, *tabs)


def _key_to_float(key):
    return pltpu.bitcast(key ^ ((key >> 31) & 0x7FFFFFFF), F32)


def _select_bias(score_ref, bias_ref, nc, k_top, n_keys_log2, key_axis):
    _, d0, d1 = score_ref.shape
    kc = (d0, d1)[key_axis]
    kidx0 = lax.broadcasted_iota(I32, (d0, d1), key_axis)
    k_top = float(k_top)
    qshape = (1, d1) if key_axis == 0 else (d0, 1)

    def fold(f):
        if key_axis == 0:
            return jnp.sum(f.reshape(SUBLANE, d0 // SUBLANE, d1), axis=0)
        part = f[:, :LANE]
        for j in range(1, d1 // LANE):
            part = part + f[:, j * LANE:(j + 1) * LANE]
        return part

    def count(pred):
        def body(c, cnt):
            return cnt + fold(jnp.where(pred(score_ref[c], c), 1.0, 0.0))
        zero = jnp.zeros((d0 // SUBLANE, d1) if key_axis == 0 else (d0, LANE), F32)
        return jnp.sum(lax.fori_loop(0, nc, body, zero), axis=key_axis, keepdims=True)

    nonneg = count(lambda s, c: s >= 0.0)
    v0 = jnp.where(nonneg >= k_top, 0, INT_MIN).astype(I32)

    def bit_body(b, v):
        cand = v | jnp.left_shift(jnp.int32(1), 30 - b)
        cf = _key_to_float(cand)
        return jnp.where(count(lambda s, c: s >= cf) >= k_top, cand, v)

    v = lax.fori_loop(0, 31, bit_body, v0)
    thr = _key_to_float(jnp.maximum(v, KEY_NEG_INF))

    at_least = count(lambda s, c: s >= thr)
    tied = (at_least > k_top) & (thr > -jnp.inf)
    all_keys = jnp.full(qshape, 2 ** n_keys_log2 - 1, I32)

    def tie_search():
        need = k_top - count(lambda s, c: s > thr)

        def tie_body(b, u):
            cand = u | jnp.left_shift(jnp.int32(1), n_keys_log2 - 1 - b)
            below = count(lambda s, c: (s == thr) & (c * kc + kidx0 < cand))
            return jnp.where(below < need, cand, u)

        return lax.fori_loop(0, n_keys_log2, tie_body, jnp.zeros(qshape, I32))

    u = lax.cond(jnp.max(jnp.where(tied, 1.0, 0.0)) > 0.0, tie_search, lambda: all_keys)

    def bias_body(c, carry):
        s = score_ref[c]
        sel = (s > -jnp.inf) & ((s > thr) | ((s == thr) & (c * kc + kidx0 <= u)))
        bias_ref[c] = jnp.where(sel, 0.0, NEG)
        return carry

    lax.fori_loop(0, nc, bias_body, 0)


def _dsa_prompt_kernel(qa_t_ref, qi_t_ref, wi_t_ref, kib_ref, kab_ref, va_tc_ref, o_ref, score_ref, bias_ref, *, k_top):
    nchunks, kc, q = score_ref.shape
    i = pl.program_id(1)
    nc = (i * q + q - 1) // kc + 1
    key0 = lax.broadcasted_iota(I32, (kc, q), 0)
    qpos = i * q + lax.broadcasted_iota(I32, (kc, q), 1)
    w = wi_t_ref[0] * (H_IDX ** -0.5 * D_IDX ** -0.5)
    qi_all = jnp.concatenate([qi_t_ref[0, h * D_IDX:(h + 1) * D_IDX, :] for h in range(H_IDX)], axis=1)

    def score_body(c, carry):
        d = _dot(kib_ref[pl.ds(pl.multiple_of(c * kc, kc), kc), :], qi_all)
        acc = jnp.zeros((kc, q), F32)
        for h in range(H_IDX):
            acc = acc + jnp.maximum(d[:, h * q:(h + 1) * q], 0.0) * w[h:h + 1, :]
        score_ref[c] = jnp.where(c * kc + key0 <= qpos, acc, -jnp.inf)
        return carry

    lax.fori_loop(0, nc, score_body, 0)
    _select_bias(score_ref, bias_ref, nc, k_top, (nchunks * kc - 1).bit_length(), key_axis=0)

    qa_all = jnp.concatenate([qa_t_ref[0, h * LANE:(h + 1) * LANE, :] for h in range(H_A)], axis=1)
    qa_all = qa_all * SOFTMAX_SCALE

    hs = H_A // DSA_HEAD_SPLITS
    q_parts = [qa_all[:, g * hs * q:(g + 1) * hs * q] for g in range(DSA_HEAD_SPLITS)]

    dv = KV_A * HEAD_DIM
    ones_rows = jnp.ones((BF16_ROWS, kc), BF16)

    def att_body(c, carry):
        k = kab_ref[pl.ds(pl.multiple_of(c * kc, kc), kc), :]
        v = jnp.concatenate([va_tc_ref[c], ones_rows], axis=0)
        b = bias_ref[c]
        out = []
        for g in range(DSA_HEAD_SPLITS):
            m, l, acc = carry[g]
            s = _dot(k, q_parts[g])
            s = jnp.concatenate([s[:, h * q:(h + 1) * q] + b for h in range(hs)], axis=1)
            m_new = jnp.maximum(m, jnp.max(s, axis=0, keepdims=True))
            alpha = jnp.exp(m - m_new)
            p = jnp.exp((s - m_new).astype(BF16))
            pv = _dot(v, p)
            l = alpha * l + pv[dv:dv + 1]
            acc = alpha * acc + pv[:dv]
            out.append((m_new, l, acc))
        return tuple(out)

    init = tuple((jnp.full((1, hs * q), NEG, F32), jnp.zeros((1, hs * q), F32),
                  jnp.zeros((KV_A * HEAD_DIM, hs * q), F32)) for _ in range(DSA_HEAD_SPLITS))
    final = lax.fori_loop(0, nc, att_body, init)
    o = jnp.concatenate([acc / l for _, l, acc in final], axis=1)
    heads_per_group = H_A // KV_A
    for j in range(H_A // 2):
        g0 = ((2 * j) // heads_per_group) * HEAD_DIM
        pair = jnp.concatenate([o[g0:g0 + HEAD_DIM, (2 * j) * q:(2 * j + 1) * q],
                                o[g0:g0 + HEAD_DIM, (2 * j + 1) * q:(2 * j + 2) * q]], axis=0)
        o_ref[:, j * LANE:(j + 1) * LANE] = pair.T.astype(BF16)


def _dsa_prompt(qa_t, qi_t, wi_t, kib, kab, va_tc, bsz, t):
    q, kc = DSA_Q, DSA_KC
    nq, nchunks = t // q, t // kc
    tile_t = lambda b, i: (b, 0, i)
    return pl.pallas_call(
        functools.partial(_dsa_prompt_kernel, k_top=min(TOPK_IDX, t // 4)),
        grid=(bsz, nq),
        in_specs=[pl.BlockSpec((1, Q_EXP, q), tile_t), pl.BlockSpec((1, H_IDX * D_IDX, q), tile_t),
                  pl.BlockSpec((1, H_IDX, q), tile_t),
                  pl.BlockSpec((t, D_IDX), lambda b, i: (b, 0)), pl.BlockSpec((t, KV_A * HEAD_DIM), lambda b, i: (b, 0)),
                  pl.BlockSpec((nchunks, KV_A * HEAD_DIM, kc), lambda b, i: (b, 0, 0))],
        out_specs=pl.BlockSpec((q, H_A * HEAD_DIM), lambda b, i: (b * nq + i, 0)),
        out_shape=jax.ShapeDtypeStruct((bsz * t, H_A * HEAD_DIM), BF16),
        scratch_shapes=[pltpu.VMEM((nchunks, kc, q), F32), pltpu.VMEM((nchunks, kc, q), F32)],
        compiler_params=_params("parallel", "arbitrary"), name="dsa_prompt",
    )(qa_t, qi_t, wi_t, kib, kab, va_tc)


def _top_blocks(gate, valid, idxf, n_sel, axis):
    gate = jnp.where(valid, gate, -jnp.inf)
    sel = jnp.zeros(gate.shape, jnp.bool_)
    for _ in range(n_sel):
        mx = jnp.max(gate, axis=axis, keepdims=True)
        first = jnp.min(jnp.where(gate == mx, idxf, 1e9), axis=axis, keepdims=True)
        pick = idxf == first
        sel = sel | (pick & valid)
        gate = jnp.where(pick, -jnp.inf, gate)
    return sel


def _moba_prompt_kernel(qb_t_ref, km_ref, kb_ref, vb_tc_ref, o_ref, *, n_sel):
    qb = o_ref.shape[0]
    nb = km_ref.shape[0]
    j = pl.program_id(1)
    r = 2 * qb
    nrow = lax.broadcasted_iota(I32, (nb, r), 0)
    nrowf = nrow.astype(F32)
    qc = lax.broadcasted_iota(I32, (qb, r), 1)
    own_mask = lax.broadcasted_iota(I32, (qb, r), 0) <= jnp.where(qc >= qb, qc - qb, qc)
    own = pl.multiple_of(j * qb, qb)

    pairs = range(H_B // 2)
    sls = [slice(p * LANE, (p + 1) * LANE) for p in pairs]
    ones_rows = jnp.ones((BF16_ROWS, MOBA_BLOCK), BF16)

    def pv_and_sum(n, p, pr):
        return _dot(jnp.concatenate([vb_tc_ref[n, sls[p], :], ones_rows], axis=0), pr)

    qes, selbiases, init = [], [], []
    for p in pairs:
        qe = jnp.concatenate([qb_t_ref[0, (2 * p) * LANE:(2 * p + 1) * LANE, :],
                              qb_t_ref[0, (2 * p + 1) * LANE:(2 * p + 2) * LANE, :]], axis=1)
        gate = _dot(km_ref[:, sls[p]], qe.astype(F32), precision=_HI)
        sel = _top_blocks(gate, nrow < j, nrowf, n_sel, axis=0)
        selbiases.append(jnp.where(sel, 0.0, NEG))
        qe = qe * SOFTMAX_SCALE
        qes.append(qe)

        s = jnp.where(own_mask, _dot(kb_ref[pl.ds(own, qb), sls[p]], qe), NEG)
        m = jnp.max(s, axis=0, keepdims=True)
        pv = pv_and_sum(j, p, jnp.exp((s - m).astype(BF16)))
        init.append((m, pv[LANE:LANE + 1], pv[:LANE]))

    half = lax.broadcasted_iota(I32, (2 * MOBA_BLOCK, r), 0) < MOBA_BLOCK

    def body(n2, carry):
        start = pl.multiple_of(n2 * (2 * MOBA_BLOCK), 2 * MOBA_BLOCK)
        out = []
        for p in pairs:
            m, l, acc = carry[p]
            rowb = [jnp.sum(jnp.where(nrow == 2 * n2 + i, selbiases[p], 0.0), axis=0, keepdims=True) for i in (0, 1)]
            s = _dot(kb_ref[pl.ds(start, 2 * MOBA_BLOCK), sls[p]], qes[p]) + jnp.where(half, rowb[0], rowb[1])
            m_new = jnp.maximum(m, jnp.max(s, axis=0, keepdims=True))
            alpha = jnp.exp(m - m_new)
            pr = jnp.exp((s - m_new).astype(BF16))
            pv = pv_and_sum(2 * n2, p, pr[:MOBA_BLOCK]) + pv_and_sum(2 * n2 + 1, p, pr[MOBA_BLOCK:])
            out.append((m_new, alpha * l + pv[LANE:LANE + 1], alpha * acc + pv[:LANE]))
        return tuple(out)

    final = lax.fori_loop(0, (j + 1) // 2, body, tuple(init))
    for p in pairs:
        _, l, acc = final[p]
        o = acc / l
        pair = jnp.concatenate([o[:HEAD_DIM, :qb], o[HEAD_DIM:, qb:]], axis=0)
        o_ref[:, sls[p]] = pair.T.astype(BF16)


def _moba_prompt(qb_t, kmean, kbb, vb_tc, bsz, t):
    nb = t // MOBA_BLOCK
    return pl.pallas_call(
        functools.partial(_moba_prompt_kernel, n_sel=min(TOPK_BLK, nb - 1)),
        grid=(bsz, nb),
        in_specs=[pl.BlockSpec((1, Q_EXP, MOBA_BLOCK), lambda b, i: (b, 0, i)), pl.BlockSpec((nb, WKV), lambda b, i: (b, 0)),
                  pl.BlockSpec((t, WKV), lambda b, i: (b, 0)),
                  pl.BlockSpec((nb, WKV, MOBA_BLOCK), lambda b, i: (b, 0, 0))],
        out_specs=pl.BlockSpec((MOBA_BLOCK, WKV), lambda b, i: (b * nb + i, 0)),
        out_shape=jax.ShapeDtypeStruct((bsz * t, WKV), BF16),
        compiler_params=_params("parallel", "arbitrary"), name="moba_prompt",
    )(qb_t, kmean, kbb, vb_tc)


GSEL_LANE = LANE - 1


def _route(rl, group=None):
    lanef = lax.broadcasted_iota(I32, rl.shape, 1).astype(F32)

    def masked_softmax(mask):
        z = jnp.where(mask, rl, -jnp.inf)
        e = jnp.exp(z - jnp.max(z, axis=1, keepdims=True))
        return e / jnp.sum(e, axis=1, keepdims=True)

    def first_lane(cond):
        return jnp.min(jnp.where(cond, lanef, 1e9), axis=1, keepdims=True)

    gmask = lanef < N_GROUPS
    gp = masked_softmax(gmask)
    if group is None:
        gprob = jnp.max(gp, axis=1, keepdims=True)
        gsel = first_lane((gp == gprob) & gmask)
    else:
        gsel = jnp.full((rl.shape[0], 1), group, I32).astype(F32)
        gprob = jnp.sum(jnp.where(lanef == gsel, gp, 0.0), axis=1, keepdims=True)
    lo = N_GROUPS + EXPERTS_PER_GROUP * gsel
    emask = (lanef >= lo) & (lanef < lo + EXPERTS_PER_GROUP)
    ep = jnp.where(emask, masked_softmax(emask), -1.0)
    p1 = jnp.max(ep, axis=1, keepdims=True)
    i1 = first_lane(ep == p1)
    ep = jnp.where(lanef == i1, -1.0, ep)
    p2 = jnp.max(ep, axis=1, keepdims=True)
    i2 = first_lane(ep == p2)
    den = p1 + p2
    cw = jnp.where(lanef == i1, gprob * p1 / den, 0.0) + jnp.where(lanef == i2, gprob * p2 / den, 0.0)
    return jnp.where(lanef == GSEL_LANE, gsel, cw)


def _merge_kernel(oa_ref, ob_ref, ga_ref, gb_ref, x_ref, wa_ref, wb_ref, wo_ref, gf_ref, wr_ref, br_ref,
                  x1_ref, h2_ref, cw_ref):
    tm = x_ref.shape[0]
    u = jax.nn.sigmoid(ga_ref[...]) * _dot(oa_ref[...], wa_ref[...]) \
        + jax.nn.sigmoid(gb_ref[...]) * _dot(ob_ref[...], wb_ref[...])
    x1 = x_ref[...] + _dot(u.astype(BF16), wo_ref[...])
    x1_ref[...] = x1
    h2 = (x1 * _rms_scale(x1)) * gf_ref[...]
    h2_ref[...] = h2.astype(BF16)

    cw_ref[...] = _route(_dot(h2, wr_ref[...], precision=_HI) + br_ref[...])


def _merge(oa, ob, ga, gb, x2d, wa, wb, wo, g_ffn, w_router, b_router, tm):
    n, d = x2d.shape
    row = lambda i: (i, 0)
    const = lambda i: (0, 0)
    ha = oa.shape[1]
    return pl.pallas_call(
        _merge_kernel, grid=(n // tm,),
        in_specs=[pl.BlockSpec((tm, ha), row), pl.BlockSpec((tm, ha), row), pl.BlockSpec((tm, d), row),
                  pl.BlockSpec((tm, d), row), pl.BlockSpec((tm, d), row), pl.BlockSpec((ha, d), const),
                  pl.BlockSpec((ha, d), const), pl.BlockSpec((d, d), const), pl.BlockSpec((1, d), const),
                  pl.BlockSpec((d, LANE), const), pl.BlockSpec((1, LANE), const)],
        out_specs=[pl.BlockSpec((tm, d), row), pl.BlockSpec((tm, d), row), pl.BlockSpec((tm, LANE), row)],
        out_shape=[jax.ShapeDtypeStruct((n, d), F32), jax.ShapeDtypeStruct((n, d), BF16),
                   jax.ShapeDtypeStruct((n, LANE), F32)],
        compiler_params=_params("parallel"), name="merge",
    )(oa, ob, ga, gb, x2d, wa, wb, wo, g_ffn.reshape(1, d), w_router, b_router)


def _moe_kernel(h_ref, cw_ref, x_ref, wg_ref, wu_ref, wd_ref, gfin_ref, y_ref, acc_ref):
    tm = h_ref.shape[0]
    e = pl.program_id(1)

    @pl.when(e == 0)
    def _():
        acc_ref[...] = jnp.zeros_like(acc_ref)

    h = h_ref[...]
    a = _dot(h, wg_ref[0])
    hdn = (a * jax.nn.sigmoid(a)) * _dot(h, wu_ref[0])
    y = _dot(hdn.astype(BF16), wd_ref[0])
    lane = lax.broadcasted_iota(I32, (tm, LANE), 1)
    col = jnp.sum(jnp.where(lane == e + N_GROUPS, cw_ref[...], 0.0), axis=1, keepdims=True)
    acc_ref[...] += y * col

    @pl.when(e == pl.num_programs(1) - 1)
    def _():
        out = x_ref[...] + acc_ref[...]
        y_ref[...] = (out * _rms_scale(out)) * gfin_ref[...]


def _moe(h2, cw, x1, wg, wu, wd, g_final, tm):
    n, d = x1.shape
    ne, _, de = wg.shape
    row = lambda i, e: (i, 0)
    return pl.pallas_call(
        _moe_kernel, grid=(n // tm, ne),
        in_specs=[pl.BlockSpec((tm, d), row), pl.BlockSpec((tm, LANE), row), pl.BlockSpec((tm, d), row),
                  pl.BlockSpec((1, d, de), lambda i, e: (e, 0, 0)), pl.BlockSpec((1, d, de), lambda i, e: (e, 0, 0)),
                  pl.BlockSpec((1, de, d), lambda i, e: (e, 0, 0)), pl.BlockSpec((1, d), lambda i, e: (0, 0))],
        out_specs=pl.BlockSpec((tm, d), row),
        out_shape=jax.ShapeDtypeStruct((n, d), F32),
        scratch_shapes=[pltpu.VMEM((tm, d), F32)],
        compiler_params=_params("parallel", "arbitrary"), name="moe",
    )(h2, cw, x1, wg, wu, wd, g_final.reshape(1, d))


def _scatter_rows_kernel(dest_ref, x_ref, zeros_ref, xs_ref, sem):
    del zeros_ref
    tm = x_ref.shape[0]
    base = pl.program_id(0) * tm

    def row_copy(r, dst):
        return pltpu.make_async_copy(x_ref.at[pl.ds(r, 1)], xs_ref.at[pl.ds(dst, 1)], sem)

    def start(r, carry):
        row_copy(r, dest_ref[base + r]).start()
        return carry

    def wait(r, carry):
        row_copy(r, 0).wait()
        return carry

    lax.fori_loop(0, tm, start, 0, unroll=ROW_DMA_UNROLL)
    lax.fori_loop(0, tm, wait, 0, unroll=ROW_DMA_UNROLL)


def _scatter_rows(dest, x, n_rows, tm):
    n, d = x.shape
    grid_spec = pltpu.PrefetchScalarGridSpec(
        num_scalar_prefetch=1, grid=(n // tm,),
        in_specs=[pl.BlockSpec((tm, d), lambda i, dest: (i, 0)), pl.BlockSpec(memory_space=pl.ANY)],
        out_specs=pl.BlockSpec(memory_space=pl.ANY),
        scratch_shapes=[pltpu.SemaphoreType.DMA(())])
    return pl.pallas_call(
        _scatter_rows_kernel, grid_spec=grid_spec, out_shape=jax.ShapeDtypeStruct((n_rows, d), x.dtype),
        input_output_aliases={2: 0}, compiler_params=_params("arbitrary"), name="moe_scatter_rows",
    )(dest, x, jnp.zeros((n_rows, d), x.dtype))


def _gather_rows_kernel(dest_ref, ys_ref, y_ref, sem):
    tm = y_ref.shape[0]
    base = pl.program_id(0) * tm

    def row_copy(r, src):
        return pltpu.make_async_copy(ys_ref.at[pl.ds(src, 1)], y_ref.at[pl.ds(r, 1)], sem)

    def start(r, carry):
        row_copy(r, dest_ref[base + r]).start()
        return carry

    def wait(r, carry):
        row_copy(r, 0).wait()
        return carry

    lax.fori_loop(0, tm, start, 0, unroll=ROW_DMA_UNROLL)
    lax.fori_loop(0, tm, wait, 0, unroll=ROW_DMA_UNROLL)


def _gather_rows(dest, ys, n, tm):
    d = ys.shape[1]
    grid_spec = pltpu.PrefetchScalarGridSpec(
        num_scalar_prefetch=1, grid=(n // tm,),
        in_specs=[pl.BlockSpec(memory_space=pl.ANY)],
        out_specs=pl.BlockSpec((tm, d), lambda i, dest: (i, 0)),
        scratch_shapes=[pltpu.SemaphoreType.DMA(())])
    return pl.pallas_call(
        _gather_rows_kernel, grid_spec=grid_spec, out_shape=jax.ShapeDtypeStruct((n, d), ys.dtype),
        compiler_params=_params("arbitrary"), name="moe_gather_rows",
    )(dest, ys)


def _moe_grouped_kernel(tg_ref, used_ref, x_ref, gf_ref, wr_ref, br_ref, wg_ref, wu_ref, wd_ref, gfin_ref, y_ref,
                        h_ref, cw_ref, acc_ref):
    tm = x_ref.shape[0]
    i, e = pl.program_id(0), pl.program_id(1)
    group = tg_ref[i]
    live = i < used_ref[0]
    last = e == pl.num_programs(1) - 1

    @pl.when(live & (e == 0))
    def _():
        x1 = x_ref[...]
        h2 = (x1 * _rms_scale(x1)) * gf_ref[...]
        h_ref[...] = h2.astype(BF16)
        cw_ref[...] = _route(_dot(h2, wr_ref[...], precision=_HI) + br_ref[...], group)
        acc_ref[...] = jnp.zeros_like(acc_ref)

    @pl.when(live)
    def _():
        h = h_ref[...]
        a = _dot(h, wg_ref[0])
        hdn = (a * jax.nn.sigmoid(a)) * _dot(h, wu_ref[0])
        y = _dot(hdn.astype(BF16), wd_ref[0])
        lane = lax.broadcasted_iota(I32, (tm, LANE), 1)
        expert_lane = N_GROUPS + group * EXPERTS_PER_GROUP + e
        col = jnp.sum(jnp.where(lane == expert_lane, cw_ref[...], 0.0), axis=1, keepdims=True)
        acc_ref[...] += y * col

    @pl.when(live & last)
    def _():
        out = x_ref[...] + acc_ref[...]
        y_ref[...] = (out * _rms_scale(out)) * gfin_ref[...]

    @pl.when(jnp.logical_not(live) & last)
    def _():
        y_ref[...] = jnp.zeros_like(y_ref)


def _moe_grouped(tile_group, tiles_used, xs, g_ffn, w_router, b_router, wg, wu, wd, g_final, tm):
    p, d = xs.shape
    de = wg.shape[2]
    row = lambda i, e, tg, used: (i, 0)
    const = lambda i, e, tg, used: (0, 0)
    expert = lambda i, e, tg, used: (tg[i] * EXPERTS_PER_GROUP + e, 0, 0)
    grid_spec = pltpu.PrefetchScalarGridSpec(
        num_scalar_prefetch=2, grid=(p // tm, EXPERTS_PER_GROUP),
        in_specs=[pl.BlockSpec((tm, d), row), pl.BlockSpec((1, d), const), pl.BlockSpec((d, LANE), const),
                  pl.BlockSpec((1, LANE), const), pl.BlockSpec((1, d, de), expert), pl.BlockSpec((1, d, de), expert),
                  pl.BlockSpec((1, de, d), expert), pl.BlockSpec((1, d), const)],
        out_specs=pl.BlockSpec((tm, d), row),
        scratch_shapes=[pltpu.VMEM((tm, d), BF16), pltpu.VMEM((tm, LANE), F32), pltpu.VMEM((tm, d), F32)])
    return pl.pallas_call(
        _moe_grouped_kernel, grid_spec=grid_spec, out_shape=jax.ShapeDtypeStruct((p, d), F32),
        compiler_params=_params("parallel", "arbitrary"), name="moe_grouped",
    )(tile_group, tiles_used, xs, g_ffn.reshape(1, d), w_router, b_router, wg, wu, wd, g_final.reshape(1, d))


def _group_dispatch(group, tm):
    n = group.shape[0]
    onehot = (group[:, None] == jnp.arange(N_GROUPS, dtype=I32)[None, :]).astype(I32)
    rank = jnp.cumsum(onehot, axis=0) - onehot
    counts = jnp.sum(onehot, axis=0)
    padded = (counts + tm - 1) // tm * tm
    ends = jnp.cumsum(padded)
    dest = jnp.sum(onehot * ((ends - padded)[None, :] + rank), axis=1).astype(I32)
    rows = -(-n // tm) * tm + N_GROUPS * tm
    tiles = rows // tm
    tile_start = jnp.arange(tiles, dtype=I32) * tm
    tile_group = jnp.minimum(jnp.sum((tile_start[:, None] >= ends[None, :]).astype(I32), axis=1), N_GROUPS - 1)
    return dest, tile_group, (ends[-1:] // tm).astype(I32), rows


def _page_specs(shape_tail, pages_per_step):
    zeros = (0,) * len(shape_tail)
    return [pl.BlockSpec((None,) + shape_tail,
                         functools.partial(lambda b, j, pt, t: (pt[b, j * pages_per_step + t],) + zeros, t=t))
            for t in range(pages_per_step)]


def _stream_pages(pt_ref, pools_and_bufs, sem_ref, pages):
    b, j = pl.program_id(0), pl.program_id(1)
    nj = pl.num_programs(1)
    total = pl.num_programs(0) * nj
    step = b * nj + j
    slot = step % 2

    def copy(pool_ref, buf_ref, page, sl, t):
        return pltpu.make_async_copy(pool_ref.at[page], buf_ref.at[sl, t], sem_ref.at[sl])

    def start(bb, jj, sl):
        for t in range(pages):
            page = pt_ref[bb, jj * pages + t]
            for pool_ref, buf_ref in pools_and_bufs:
                copy(pool_ref, buf_ref, page, sl, t).start()

    @pl.when(step == 0)
    def _():
        start(b, j, slot)

    @pl.when(step + 1 < total)
    def _():
        nxt = step + 1
        start(nxt // nj, nxt % nj, 1 - slot)

    for t in range(pages):
        for pool_ref, buf_ref in pools_and_bufs:
            copy(pool_ref, buf_ref, 0, slot, t).wait()
    return slot


def _dsa_sample_score_kernel(pt_ref, qi_ref, w_ref, kin_ref, pool_ref, s_ref, snew_ref, kbuf_ref, sem_ref, *, pages):
    slot = _stream_pages(pt_ref, [(pool_ref, kbuf_ref)], sem_ref, pages)
    j = pl.program_id(1)
    rows = qi_ref.shape[1]
    s_len = rows // H_IDX
    q = qi_ref[0][:, :D_IDX]
    w = w_ref[0] * (H_IDX ** -0.5 * D_IDX ** -0.5)

    def scores(d):
        d = jnp.maximum(d, 0.0) * w
        return jnp.sum(d.reshape(s_len, H_IDX, d.shape[1]), axis=1)

    k_t = jnp.concatenate([kbuf_ref[slot, t].astype(BF16) for t in range(pages)], axis=1)
    s_ref[0] = scores(_dot(q, k_t))

    @pl.when(j == pl.num_programs(1) - 1)
    def _():
        sn = scores(_dot_nt(q, kin_ref[0][:, :D_IDX].astype(BF16)))
        si = lax.broadcasted_iota(I32, (s_len, PAGE_SIZE), 0)
        ki = lax.broadcasted_iota(I32, (s_len, PAGE_SIZE), 1)
        snew_ref[0] = jnp.where(ki <= si, sn, -jnp.inf)


def _dsa_sample_scores(page_table, qi_s, wi_s, kin_pad, pool_kidx, pages):
    bd, n_pages = page_table.shape
    rows = qi_s.shape[1]
    s_len = rows // H_IDX
    per_b = lambda b, j, pt: (b, 0, 0)
    grid_spec = pltpu.PrefetchScalarGridSpec(
        num_scalar_prefetch=1, grid=(bd, n_pages // pages),
        in_specs=[pl.BlockSpec((1, rows, LANE), per_b), pl.BlockSpec((1, rows, 1), per_b),
                  pl.BlockSpec((1, PAGE_SIZE, LANE), per_b), pl.BlockSpec(memory_space=pl.ANY)],
        out_specs=[pl.BlockSpec((1, s_len, pages * PAGE_SIZE), lambda b, j, pt: (b, 0, j)),
                   pl.BlockSpec((1, s_len, PAGE_SIZE), per_b)],
        scratch_shapes=[pltpu.VMEM((2, pages, D_IDX, PAGE_SIZE), F32), pltpu.SemaphoreType.DMA((2,))])
    return pl.pallas_call(
        functools.partial(_dsa_sample_score_kernel, pages=pages), grid_spec=grid_spec,
        out_shape=[jax.ShapeDtypeStruct((bd, s_len, n_pages * PAGE_SIZE), F32),
                   jax.ShapeDtypeStruct((bd, s_len, PAGE_SIZE), F32)],
        compiler_params=_params("arbitrary", "arbitrary"), name="dsa_sample_scores",
    )(page_table, qi_s, wi_s, kin_pad, pool_kidx)


def _dsa_sample_select_kernel(sp_ref, sn_ref, bp_ref, bn_ref, score_ref, bias_ref, *, k_top):
    nch, q, kc = score_ref.shape
    npast = nch - 1

    def load(c, carry):
        score_ref[c] = sp_ref[:, pl.ds(pl.multiple_of(c * kc, kc), kc)]
        return carry

    lax.fori_loop(0, npast, load, 0)
    score_ref[npast] = jnp.concatenate([sn_ref[...], jnp.full((q, kc - LANE), -jnp.inf, F32)], axis=1)
    _select_bias(score_ref, bias_ref, nch, k_top, (nch * kc - 1).bit_length(), key_axis=1)

    def store(c, carry):
        bp_ref[:, pl.ds(pl.multiple_of(c * kc, kc), kc)] = bias_ref[c]
        return carry

    lax.fori_loop(0, npast, store, 0)
    bn_ref[...] = bias_ref[npast][:, :LANE]


def _dsa_sample_select(s_past, s_new, k_top):
    q, past = s_past.shape
    kc = 512
    nch = past // kc + 1
    return pl.pallas_call(
        functools.partial(_dsa_sample_select_kernel, k_top=k_top),
        out_shape=[jax.ShapeDtypeStruct((q, past), F32), jax.ShapeDtypeStruct((q, LANE), F32)],
        scratch_shapes=[pltpu.VMEM((nch, q, kc), F32), pltpu.VMEM((nch, q, kc), F32)],
        compiler_params=pltpu.CompilerParams(vmem_limit_bytes=VMEM_LIMIT), name="dsa_sample_select",
    )(s_past, s_new)


def _dsa_sample_attn_kernel(pt_ref, qa_ref, bp_ref, bn_ref, kn_ref, vn_ref, kpool_ref, vpool_ref, o_ref,
                            m_ref, l_ref, acc_ref, kbuf_ref, vbuf_ref, sem_ref, *, pages):
    slot = _stream_pages(pt_ref, [(kpool_ref, kbuf_ref), (vpool_ref, vbuf_ref)], sem_ref, pages)
    j = pl.program_id(1)
    rows = qa_ref.shape[1]
    s_len = rows // H_A
    q = qa_ref[0]
    scale = HEAD_DIM ** -0.5

    @pl.when(j == 0)
    def _():
        m_ref[...] = jnp.full_like(m_ref, NEG)
        l_ref[...] = jnp.zeros_like(l_ref)
        acc_ref[...] = jnp.zeros_like(acc_ref)

    def update(s, bias, v, v_dot):
        n = s.shape[1]
        s = (s.reshape(s_len, H_A, n) * scale + bias[:, None, :]).reshape(rows, n)
        m = m_ref[...]
        m_new = jnp.maximum(m, jnp.max(s, axis=1, keepdims=True))
        alpha = jnp.exp(m - m_new)
        p = jnp.exp(s - m_new)
        l_ref[...] = alpha * l_ref[...] + jnp.sum(p, axis=1, keepdims=True)
        acc_ref[...] = alpha * acc_ref[...] + v_dot(p.astype(BF16), v)
        m_ref[...] = m_new

    k_t = jnp.concatenate([kbuf_ref[slot, t].astype(BF16) for t in range(pages)], axis=1)
    v_t = jnp.concatenate([vbuf_ref[slot, t].astype(BF16) for t in range(pages)], axis=1)
    update(_dot(q, k_t), bp_ref[0], v_t, _dot_nt)

    @pl.when(j == pl.num_programs(1) - 1)
    def _():
        update(_dot_nt(q, kn_ref[0].astype(BF16)), bn_ref[0], vn_ref[0].astype(BF16), _dot)
        o_ref[0] = acc_ref[...] / l_ref[...]


def _dsa_sample_attn(page_table, qa_s, bias_past, bias_new, kn_pad, vn_pad, pool_k, pool_v, pages):
    bd, n_pages = page_table.shape
    rows = qa_s.shape[1]
    s_len = rows // H_A
    per_b = lambda b, j, pt: (b, 0, 0)
    grid_spec = pltpu.PrefetchScalarGridSpec(
        num_scalar_prefetch=1, grid=(bd, n_pages // pages),
        in_specs=[pl.BlockSpec((1, rows, LANE), per_b),
                  pl.BlockSpec((1, s_len, pages * PAGE_SIZE), lambda b, j, pt: (b, 0, j)),
                  pl.BlockSpec((1, s_len, LANE), per_b),
                  pl.BlockSpec((1, PAGE_SIZE, LANE), per_b), pl.BlockSpec((1, PAGE_SIZE, LANE), per_b),
                  pl.BlockSpec(memory_space=pl.ANY), pl.BlockSpec(memory_space=pl.ANY)],
        out_specs=pl.BlockSpec((1, rows, LANE), per_b),
        scratch_shapes=[pltpu.VMEM((rows, 1), F32), pltpu.VMEM((rows, 1), F32), pltpu.VMEM((rows, LANE), F32),
                        pltpu.VMEM((2, pages, KV_A * HEAD_DIM, PAGE_SIZE), F32),
                        pltpu.VMEM((2, pages, KV_A * HEAD_DIM, PAGE_SIZE), F32), pltpu.SemaphoreType.DMA((2,))])
    return pl.pallas_call(
        functools.partial(_dsa_sample_attn_kernel, pages=pages), grid_spec=grid_spec,
        out_shape=jax.ShapeDtypeStruct((bd, rows, LANE), F32),
        compiler_params=_params("arbitrary", "arbitrary"), name="dsa_sample_attn",
    )(page_table, qa_s, bias_past, bias_new, kn_pad, vn_pad, pool_k, pool_v)


def _moba_sample_kernel(pt_ref, qb_ref, kn_ref, vn_ref, *refs, pages, n_sel):
    k_refs, v_refs = refs[:pages], refs[pages:2 * pages]
    o_ref, gate_ref, m_ref, l_ref, acc_ref = refs[2 * pages:]
    del pt_ref
    j = pl.program_id(1)
    rows = qb_ref.shape[1]
    s_len = rows // H_B
    nbp = acc_ref.shape[0]
    ppb = MOBA_BLOCK // PAGE_SIZE
    scale = HEAD_DIM ** -0.5

    qs = qb_ref[0]
    pair = (lax.broadcasted_iota(I32, (rows, LANE), 0) % H_B) // 2
    qe = jnp.concatenate([jnp.where(pair == p, qs, jnp.zeros_like(qs)) for p in range(H_B // 2)], axis=1)
    lane = lax.broadcasted_iota(I32, (rows, LANE), 1)

    @pl.when(j == 0)
    def _():
        gate_ref[...] = jnp.full_like(gate_ref, -jnp.inf)
        m_ref[...] = jnp.full_like(m_ref, NEG)
        l_ref[...] = jnp.zeros_like(l_ref)

    nblk = pages // ppb
    blocks = [slice(t * MOBA_BLOCK, (t + 1) * MOBA_BLOCK) for t in range(nblk)]
    k_t = jnp.concatenate([k_refs[t][...].astype(BF16) for t in range(pages)], axis=1)
    qk = _dot(qe, k_t)
    s = qk * scale
    gate_t = [jnp.sum(qk[:, blk], axis=1, keepdims=True) * (1.0 / MOBA_BLOCK) for blk in blocks]
    m_t = [jnp.max(s[:, blk], axis=1, keepdims=True) for blk in blocks]
    p = jnp.exp(s - jnp.concatenate([jnp.broadcast_to(m, (rows, MOBA_BLOCK)) for m in m_t], axis=1))
    l_t = [jnp.sum(p[:, blk], axis=1, keepdims=True) for blk in blocks]
    pb = p.astype(BF16)
    gate, m_blk, l_blk = gate_ref[...], m_ref[...], l_ref[...]
    for t in range(nblk):
        n = j * nblk + t
        v_t = jnp.concatenate([v_refs[ppb * t + i][...].astype(BF16) for i in range(ppb)], axis=1)
        acc_ref[n] = _dot_nt(pb[:, blocks[t]], v_t)
        here = lane == n
        gate = jnp.where(here, gate_t[t], gate)
        m_blk = jnp.where(here, m_t[t], m_blk)
        l_blk = jnp.where(here, l_t[t], l_blk)
    gate_ref[...] = gate
    m_ref[...] = m_blk
    l_ref[...] = l_blk

    @pl.when(j == pl.num_programs(1) - 1)
    def _():
        sel = _top_blocks(gate_ref[...], lane < nbp, lane.astype(F32), n_sel, axis=1)
        s_own = _dot_nt(qe, kn_ref[0].astype(BF16)) * scale
        srow = lax.broadcasted_iota(I32, (rows, LANE), 0) // H_B
        s_own = jnp.where(lane <= srow, s_own, NEG)
        m_blk = jnp.where(sel, m_ref[...], NEG)
        m_all = jnp.maximum(jnp.max(m_blk, axis=1, keepdims=True), jnp.max(s_own, axis=1, keepdims=True))
        wgt = jnp.where(sel, jnp.exp(m_blk - m_all), 0.0)
        p_own = jnp.exp(s_own - m_all)
        l_all = jnp.sum(wgt * l_ref[...], axis=1, keepdims=True) + jnp.sum(p_own, axis=1, keepdims=True)
        o = _dot(p_own.astype(BF16), vn_ref[0].astype(BF16))
        for n in range(nbp):
            o = o + wgt[:, n:n + 1] * acc_ref[n]
        o = o / l_all
        head_lane = lax.broadcasted_iota(I32, (rows, WKV), 1) // HEAD_DIM
        head_row = lax.broadcasted_iota(I32, (rows, WKV), 0) % H_B
        o = jnp.where(head_lane == head_row, o, 0.0)
        o_ref[0] = jnp.sum(o.reshape(s_len, H_B, WKV), axis=1)


def _moba_sample(page_table, qb_s, kn_pad, vn_pad, pool_k, pool_v, pages):
    bd, n_pages = page_table.shape
    rows = qb_s.shape[1]
    s_len = rows // H_B
    nbp = n_pages * PAGE_SIZE // MOBA_BLOCK
    per_b = lambda b, j, pt: (b, 0, 0)
    grid_spec = pltpu.PrefetchScalarGridSpec(
        num_scalar_prefetch=1, grid=(bd, n_pages // pages),
        in_specs=[pl.BlockSpec((1, rows, LANE), per_b), pl.BlockSpec((1, PAGE_SIZE, WKV), per_b),
                  pl.BlockSpec((1, PAGE_SIZE, WKV), per_b)] + _page_specs((WKV, PAGE_SIZE), pages) * 2,
        out_specs=pl.BlockSpec((1, s_len, WKV), per_b),
        scratch_shapes=[pltpu.VMEM((rows, LANE), F32), pltpu.VMEM((rows, LANE), F32), pltpu.VMEM((rows, LANE), F32),
                        pltpu.VMEM((nbp, rows, WKV), F32)])
    return pl.pallas_call(
        functools.partial(_moba_sample_kernel, pages=pages, n_sel=min(TOPK_BLK, nbp)), grid_spec=grid_spec,
        out_shape=jax.ShapeDtypeStruct((bd, s_len, WKV), F32),
        compiler_params=_params("parallel", "arbitrary"), name="moba_sample",
    )(page_table, qb_s, kn_pad, vn_pad, *([pool_k] * pages), *([pool_v] * pages))


def _split_w_in(w_in):
    d = w_in.shape[0]
    sizes = (H_A * HEAD_DIM, KV_A * HEAD_DIM, KV_A * HEAD_DIM, H_IDX * D_IDX, D_IDX, H_IDX, WKV, WKV, WKV, d, d)
    parts, o = [], 0
    for s in sizes:
        parts.append(w_in[:, o:o + s])
        o += s
    return parts


def _expand_heads(w, lane_offsets):
    z = jnp.zeros((w.shape[0], HEAD_DIM), w.dtype)
    parts = []
    for h, off in enumerate(lane_offsets):
        blk = w[:, h * HEAD_DIM:(h + 1) * HEAD_DIM]
        parts += [blk, z] if off == 0 else [z, blk]
    return jnp.concatenate(parts, axis=1)


def _layout_w_in(w_in):
    d = w_in.shape[0]
    qa, ka, va, qi, ki, wi, qb, kb, vb, ga, gb = _split_w_in(w_in)
    heads_per_group = H_A // KV_A
    qa_e = _expand_heads(qa, [(h // heads_per_group) * HEAD_DIM for h in range(H_A)])
    qb_e = _expand_heads(qb, [(h % 2) * HEAD_DIM for h in range(H_B)])
    kiwi = jnp.concatenate([ki, wi, jnp.zeros((d, LANE - D_IDX - H_IDX), w_in.dtype)], axis=1)
    ki_pad = jnp.concatenate([ki, jnp.zeros((d, LANE - D_IDX), w_in.dtype)], axis=1)
    w_sample = jnp.concatenate([qa_e, _expand_heads(qi, [0] * H_IDX), qb_e, ka, va, kiwi, kb, vb, ga, gb], axis=1)
    w_t = jnp.concatenate([qa_e, qi, qb_e, ka, va, ki, kb, vb, wi, jnp.zeros_like(wi)], axis=1).T
    w_r = jnp.concatenate([ka, ki_pad, kb, ga, gb], axis=1)
    return w_sample.astype(BF16), w_t.astype(BF16), w_r.astype(BF16)


def _rope_angles(pos):
    half = ROT_DIM // 2
    inv = ROPE_THETA ** (-jnp.arange(half, dtype=F32) / half)
    ang = pos.astype(F32)[:, None] * inv[None, :]
    return jnp.cos(ang), jnp.sin(ang)


def _rope_tables(pos):
    cos, sin = _rope_angles(pos)
    n = pos.shape[0]
    pad = jnp.zeros((n, HEAD_DIM - ROT_DIM), F32)
    zero = jnp.zeros((n, ROT_DIM // 2), F32)
    c = jnp.concatenate([cos, cos, pad + 1.0], axis=1)
    s1 = jnp.concatenate([-sin, zero, pad], axis=1)
    s2 = jnp.concatenate([zero, sin, pad], axis=1)
    return tuple(jnp.tile(t, (1, LANE // HEAD_DIM)) for t in (c, s1, s2))


def _pages_transposed(pool):
    n_pool, page = pool.shape[:2]
    nd = pool.ndim
    return jnp.transpose(pool, (0,) + tuple(range(2, nd)) + (1,)).reshape(n_pool, -1, page)


def _pad_rows(x, rows):
    b, s, w = x.shape
    return jnp.concatenate([x, jnp.zeros((b, rows - s, w), x.dtype)], axis=1)


def _ffn(oa, ob, ga, gb, x2d, lw, g_final, tm_merge, tm_moe, grouped):
    x1, h2, cw = _merge(oa, ob, ga, gb, x2d, lw["wa"], lw["wb"], lw["wo"], lw["g_ffn"], lw["w_router"],
                        lw["b_router"], tm_merge)
    if not grouped:
        return _moe(h2, cw, x1, lw["wg"], lw["wu"], lw["wd"], g_final, tm_moe)
    dest, tile_group, tiles_used, rows = _group_dispatch(cw[:, GSEL_LANE].astype(I32), tm_moe)
    xs = _scatter_rows(dest, x1, rows, TM_ROWS)
    ys = _moe_grouped(tile_group, tiles_used, xs, lw["g_ffn"], lw["w_router"], lw["b_router"], lw["wg"], lw["wu"], lw["wd"],
                      g_final, tm_moe)
    return _gather_rows(dest, ys, x1.shape[0], TM_ROWS)


@jax.jit
def kernel(x_prompt, x_sample, cache_k_a, cache_v_a, cache_k_idx, cache_k_b, cache_v_b, page_table, w_in, w_br_a,
           w_br_b, w_out, g_mix, g_ffn, w_grp, b_grp, w_exp, b_exp, w_e_gate, w_e_up, w_e_down, g_final):
    bsz, t, d = x_prompt.shape
    bd, s_len, _ = x_sample.shape
    depth = w_in.shape[0]
    n_pages = page_table.shape[1]
    past = n_pages * PAGE_SIZE
    assert depth == 1, "the final RMSNorm is fused into the last layer's MoE kernel"
    assert past % MOBA_BLOCK == 0 and s_len <= PAGE_SIZE and t % TM_PROJ == 0

    xp = x_prompt.reshape(bsz * t, d)
    xs = x_sample.reshape(bd * s_len, d)
    ns = bd * s_len

    l = 0
    zpad = jnp.zeros((d, LANE - N_GROUPS - N_EXPERTS), F32)
    lw = dict(
        wa=w_br_a[l].astype(BF16), wb=w_br_b[l].astype(BF16), wo=w_out[l].astype(BF16), g_ffn=g_ffn[l],
        w_router=jnp.concatenate([w_grp[l], w_exp[l], zpad], axis=1),
        b_router=jnp.concatenate([b_grp[l], b_exp[l], zpad[0]]).reshape(1, LANE),
        wg=w_e_gate[l].astype(BF16), wu=w_e_up[l].astype(BF16), wd=w_e_down[l].astype(BF16))
    w_sample, w_t, w_r = _layout_w_in(w_in[l])

    pos_p = jnp.arange(t)
    cos_p, sin_p = _rope_angles(pos_p)
    (qa_t, qi_t, qb_t, ka_t, va_t, ki_t, kb_t, vb_t, wi_t, kab, kib, kbb, va_tc, vb_tc, kmean, ga, gb) = _project_prompt(
        xp, g_mix[l], w_t, w_r, (cos_p.T, sin_p.T), _rope_tables(pos_p), bsz, t)
    o_a = _dsa_prompt(qa_t, qi_t, wi_t, kib, kab, va_tc, bsz, t)
    o_b = _moba_prompt(qb_t, kmean.reshape(bsz * t // MOBA_BLOCK, WKV), kbb, vb_tc, bsz, t)
    y_prompt = _ffn(o_a, o_b, ga, gb, xp, lw, g_final, TM_MERGE, TM_MOE, True).reshape(bsz, t, d)

    def heads_last(x_t, heads):
        return jnp.transpose(x_t.reshape(1, bsz, heads, HEAD_DIM, t), (0, 1, 4, 2, 3))

    rows_p = (heads_last(ka_t, KV_A), heads_last(va_t, KV_A), jnp.transpose(ki_t, (0, 2, 1))[None],
              heads_last(kb_t, H_B), heads_last(vb_t, H_B))

    tabs_s = tuple(jnp.tile(tb, (bd, 1)) for tb in _rope_tables(past + jnp.arange(s_len)))
    qa, qi, qb, ka, va, kiwi, kb, vb, ga, gb = _project_sample(xs, g_mix[l], w_sample, tabs_s)
    rows = s_len * N_HEAD_SLABS
    qa_s = qa.reshape(bd, rows, LANE)
    qi_s = qi.reshape(bd, rows, LANE)
    qb_s = qb.reshape(bd, rows, LANE)
    wi_s = kiwi[:, D_IDX:D_IDX + H_IDX].reshape(bd, rows, 1)
    s_past, s_new = _dsa_sample_scores(page_table, qi_s, wi_s, _pad_rows(kiwi.reshape(bd, s_len, LANE), PAGE_SIZE),
                                       _pages_transposed(cache_k_idx[l]), min(PAGES_SCORES, n_pages))
    bias_past, bias_new = _dsa_sample_select(s_past.reshape(ns, past), s_new.reshape(ns, PAGE_SIZE),
                                             min(TOPK_IDX, (past + s_len) // 4))
    o_a = _dsa_sample_attn(page_table, qa_s, bias_past.reshape(bd, s_len, past),
                           bias_new.reshape(bd, s_len, PAGE_SIZE),
                           _pad_rows(ka.reshape(bd, s_len, LANE), PAGE_SIZE),
                           _pad_rows(va.reshape(bd, s_len, LANE), PAGE_SIZE),
                           _pages_transposed(cache_k_a[l]), _pages_transposed(cache_v_a[l]), min(PAGES_ATTN, n_pages))
    o_a = o_a.reshape(bd, s_len, H_A, KV_A, HEAD_DIM)
    hpg = H_A // KV_A
    o_a = jnp.concatenate([o_a[:, :, g * hpg:(g + 1) * hpg, g] for g in range(KV_A)], axis=2)
    o_a = o_a.reshape(ns, H_A * HEAD_DIM).astype(BF16)
    o_b = _moba_sample(page_table, qb_s, _pad_rows(kb.reshape(bd, s_len, WKV), PAGE_SIZE),
                       _pad_rows(vb.reshape(bd, s_len, WKV), PAGE_SIZE),
                       _pages_transposed(cache_k_b[l]), _pages_transposed(cache_v_b[l]), min(PAGES_MOBA, n_pages))
    o_b = o_b.reshape(ns, WKV).astype(BF16)
    y_sample = _ffn(o_a, o_b, ga, gb, xs, lw, g_final, ns, ns, False).reshape(bd, s_len, d)
    rows_s = (ka.reshape(1, bd, s_len, KV_A, HEAD_DIM), va.reshape(1, bd, s_len, KV_A, HEAD_DIM),
              kiwi[:, :D_IDX].reshape(1, bd, s_len, D_IDX), kb.reshape(1, bd, s_len, H_B, HEAD_DIM),
              vb.reshape(1, bd, s_len, H_B, HEAD_DIM))
    return (y_prompt, y_sample) + rows_p + rows_s
```

```python
import functools

import jax
import jax.numpy as jnp
from jax import lax
from jax.experimental import pallas as pl
from jax.experimental.pallas import tpu as pltpu

F32, BF16, I32 = jnp.float32, jnp.bfloat16, jnp.int32

HEAD_DIM = 64
ROT_DIM = HEAD_DIM // 4
ROPE_THETA = 500000.0
H_A = 8
KV_A = 2
H_IDX = 8
D_IDX = 64
TOPK_IDX = 256
H_B = 8
MOBA_BLOCK = 256
TOPK_BLK = 3
N_GROUPS = 4
EXPERTS_PER_GROUP = 8
N_EXPERTS = N_GROUPS * EXPERTS_PER_GROUP
D_EXPERT = 256
RMS_EPS = 1e-6
PAGE_SIZE = 128

LANE = 128
SUBLANE = 8
BF16_ROWS = 16
VMEM_LIMIT = 56 * 1024 * 1024

TM_PROJ = 512
DSA_Q = 256
DSA_KC = TM_PROJ
DSA_HEAD_SPLITS = 1
TM_MERGE = 512
TM_MOE = 1024
TM_ROWS = 512
ROW_DMA_UNROLL = 8
PAGES_SCORES = 32
PAGES_ATTN = 32
PAGES_MOBA = 16

NEG = -1e30
INT_MIN = -(2 ** 31)
KEY_NEG_INF = -2139095041
N_HEAD_SLABS = 8
Q_EXP = N_HEAD_SLABS * LANE
WKV = H_B * HEAD_DIM
SOFTMAX_SCALE = HEAD_DIM ** -0.5
assert SOFTMAX_SCALE == 0.125

_NT = (((1,), (1,)), ((), ()))
_HI = lax.Precision.HIGHEST


def _dot(a, b, precision=None):
    return jnp.dot(a, b, preferred_element_type=F32, precision=precision)


def _dot_nt(a, b, precision=None):
    return lax.dot_general(a, b, _NT, preferred_element_type=F32, precision=precision)


def _params(*sem):
    return pltpu.CompilerParams(dimension_semantics=sem, vmem_limit_bytes=VMEM_LIMIT)


def _rms_scale(x):
    return lax.rsqrt(jnp.mean(x * x, axis=-1, keepdims=True) + RMS_EPS)


def _rope(y, c, s1, s2):
    half = ROT_DIM // 2
    return y * c + pltpu.roll(y, LANE - half, 1) * s1 + pltpu.roll(y, half, 1) * s2


def _rope_t(y, cos_t, sin_t):
    half = ROT_DIM // 2
    parts = []
    for r in range(0, y.shape[0], HEAD_DIM):
        x1, x2 = y[r:r + half], y[r + half:r + ROT_DIM]
        parts += [x1 * cos_t - x2 * sin_t, x2 * cos_t + x1 * sin_t, y[r + ROT_DIM:r + HEAD_DIM]]
    return jnp.concatenate(parts, axis=0)


def _proj_prompt_kernel(x_ref, g_ref, wt_ref, wr_ref, cos_t_ref, sin_t_ref, c_ref, s1_ref, s2_ref,
                        qa_t_ref, qi_t_ref, qb_t_ref, ka_t_ref, va_t_ref, ki_t_ref, kb_t_ref, vb_t_ref, wi_t_ref,
                        kab_ref, kib_ref, kbb_ref, va_tc_ref, vb_tc_ref, kmean_ref, ga_ref, gb_ref):
    tm, d_model = x_ref.shape
    x = x_ref[...]
    h = ((x * _rms_scale(x)) * g_ref[...]).astype(BF16)
    cos_t, sin_t = cos_t_ref[...], sin_t_ref[...]
    c, s1, s2 = c_ref[...], s1_ref[...], s2_ref[...]

    off = [0]

    def proj_t(rows):
        y = _dot_nt(wt_ref[off[0]:off[0] + rows, :], h)
        off[0] += rows
        return y

    half_q = Q_EXP // 2
    for r0 in (0, half_q):
        qa_t_ref[0, r0:r0 + half_q, :] = _rope_t(proj_t(half_q), cos_t, sin_t).astype(BF16)
    qi_t_ref[0] = _rope_t(proj_t(H_IDX * D_IDX), cos_t, sin_t).astype(BF16)
    for r0 in (0, half_q):
        qb_t_ref[0, r0:r0 + half_q, :] = _rope_t(proj_t(half_q), cos_t, sin_t).astype(BF16)
    ka_t_ref[0] = _rope_t(proj_t(KV_A * HEAD_DIM), cos_t, sin_t)
    va_t = proj_t(KV_A * HEAD_DIM)
    va_t_ref[0] = va_t
    va_tc_ref[0] = va_t.astype(BF16)
    ki_t_ref[0] = _rope_t(proj_t(D_IDX), cos_t, sin_t)
    kb_t_ref[0] = _rope_t(proj_t(WKV), cos_t, sin_t)
    vb_t = proj_t(WKV)
    vb_t_ref[0] = vb_t
    for n in range(tm // MOBA_BLOCK):
        vb_tc_ref[n] = vb_t[:, n * MOBA_BLOCK:(n + 1) * MOBA_BLOCK].astype(BF16)
    wi_t_ref[0] = proj_t(2 * H_IDX)[:H_IDX]

    col = [0]

    def proj(width):
        y = _dot(h, wr_ref[:, col[0]:col[0] + width])
        col[0] += width
        return y

    y = proj(2 * LANE)
    kab_ref[...] = _rope(y[:, :LANE], c, s1, s2).astype(BF16)
    kib_ref[...] = _rope(y[:, LANE:], c, s1, s2)[:, :D_IDX].astype(BF16)
    y = proj(WKV)
    kb = jnp.concatenate([_rope(y[:, j * LANE:(j + 1) * LANE], c, s1, s2) for j in range(WKV // LANE)], axis=1)
    kbb_ref[...] = kb.astype(BF16)
    nblk = tm // MOBA_BLOCK
    kmean_ref[0] = jnp.sum(kb.reshape(nblk, MOBA_BLOCK, WKV), axis=1) * (1.0 / MOBA_BLOCK)
    ga_ref[...] = proj(d_model)
    gb_ref[...] = proj(d_model)


def _project_prompt(x2d, g, w_t, w_r, tabs_t, tabs, bsz, t):
    n, d = x2d.shape
    tm = TM_PROJ
    tpb = t // tm
    row = lambda i: (i, 0)
    const = lambda i: (0, 0)
    bt = lambda i: (i // tpb, 0, i % tpb)
    chunk = lambda i: (i, 0, 0)
    once = dict(pipeline_mode=pl.Buffered(1))
    in_specs = [pl.BlockSpec((tm, d), row), pl.BlockSpec((1, d), const),
                pl.BlockSpec(w_t.shape, const, **once), pl.BlockSpec(w_r.shape, const, **once)]
    in_specs += [pl.BlockSpec((ROT_DIM // 2, tm), lambda i: (0, i % tpb))] * 2
    in_specs += [pl.BlockSpec((tm, LANE), lambda i: (i % tpb, 0))] * 3
    t_feats = [(Q_EXP, BF16), (H_IDX * D_IDX, BF16), (Q_EXP, BF16), (KV_A * HEAD_DIM, F32), (KV_A * HEAD_DIM, F32),
               (D_IDX, F32), (WKV, F32), (WKV, F32), (H_IDX, F32)]
    out_shape = [jax.ShapeDtypeStruct((bsz, f, t), dt) for f, dt in t_feats]
    out_specs = [pl.BlockSpec((1, f, tm), bt) for f, _ in t_feats]
    r_feats = [(KV_A * HEAD_DIM, BF16), (D_IDX, BF16), (WKV, BF16)]
    out_shape += [jax.ShapeDtypeStruct((n, f), dt) for f, dt in r_feats]
    out_specs += [pl.BlockSpec((tm, f), row) for f, _ in r_feats]
    nblk = tm // MOBA_BLOCK
    out_shape += [jax.ShapeDtypeStruct((n // tm, KV_A * HEAD_DIM, tm), BF16),
                  jax.ShapeDtypeStruct((n // MOBA_BLOCK, WKV, MOBA_BLOCK), BF16),
                  jax.ShapeDtypeStruct((n // tm, nblk, WKV), F32),
                  jax.ShapeDtypeStruct((n, d), F32), jax.ShapeDtypeStruct((n, d), F32)]
    out_specs += [pl.BlockSpec((1, KV_A * HEAD_DIM, tm), chunk), pl.BlockSpec((nblk, WKV, MOBA_BLOCK), chunk),
                  pl.BlockSpec((1, nblk, WKV), chunk), pl.BlockSpec((tm, d), row), pl.BlockSpec((tm, d), row)]
    return pl.pallas_call(
        _proj_prompt_kernel, grid=(n // tm,), in_specs=in_specs, out_specs=out_specs, out_shape=out_shape,
        compiler_params=_params("parallel"), name="proj_prompt",
    )(x2d, g.reshape(1, d), w_t, w_r, *tabs_t, *tabs)


def _proj_sample_kernel(x_ref, g_ref, w_ref, c_ref, s1_ref, s2_ref,
                        qa_ref, qi_ref, qb_ref, ka_ref, va_ref, kiwi_ref, kb_ref, vb_ref, ga_ref, gb_ref):
    tm, d_model = x_ref.shape
    x = x_ref[...]
    h = ((x * _rms_scale(x)) * g_ref[...]).astype(BF16)
    c, s1, s2 = c_ref[...], s1_ref[...], s2_ref[...]
    off = [0]

    def proj(width):
        y = _dot(h, w_ref[:, off[0]:off[0] + width])
        off[0] += width
        return y

    def rope_slabs(y):
        return jnp.concatenate([_rope(y[:, j * LANE:(j + 1) * LANE], c, s1, s2) for j in range(y.shape[1] // LANE)],
                               axis=1)

    for q_ref in (qa_ref, qi_ref, qb_ref):
        q_ref[...] = rope_slabs(proj(Q_EXP)).astype(BF16)
    y = proj(3 * LANE)
    ka_ref[...] = _rope(y[:, :LANE], c, s1, s2)
    va_ref[...] = y[:, LANE:2 * LANE]
    is_ki = lax.broadcasted_iota(I32, (tm, LANE), 1) < D_IDX
    kiwi_ref[...] = _rope(y[:, 2 * LANE:], jnp.where(is_ki, c, 1.0), jnp.where(is_ki, s1, 0.0),
                          jnp.where(is_ki, s2, 0.0))
    kb_ref[...] = rope_slabs(proj(WKV))
    vb_ref[...] = proj(WKV)
    ga_ref[...] = proj(d_model)
    gb_ref[...] = proj(d_model)


def _project_sample(x2d, g, w_all, tabs):
    n, d = x2d.shape
    const = lambda i: (0, 0)
    widths = [(Q_EXP, BF16)] * 3 + [(LANE, F32)] * 3 + [(WKV, F32)] * 2 + [(d, F32)] * 2
    return pl.pallas_call(
        _proj_sample_kernel, grid=(1,),
        in_specs=[pl.BlockSpec((n, d), const), pl.BlockSpec((1, d), const),
                  pl.BlockSpec(w_all.shape, const, pipeline_mode=pl.Buffered(1))] + [pl.BlockSpec((n, LANE), const)] * 3,
        out_specs=[pl.BlockSpec((n, w), const) for w, _ in widths],
        out_shape=[jax.ShapeDtypeStruct((n, w), dt) for w, dt in widths],
        compiler_params=_params("arbitrary"), name="proj_sample",
    )(x2d, g.reshape(1, d), w_all, *tabs)


def _key_to_float(key):
    return pltpu.bitcast(key ^ ((key >> 31) & 0x7FFFFFFF), F32)


def _select_bias(score_ref, bias_ref, nc, k_top, n_keys_log2, key_axis):
    _, d0, d1 = score_ref.shape
    kc = (d0, d1)[key_axis]
    kidx0 = lax.broadcasted_iota(I32, (d0, d1), key_axis)
    k_top = float(k_top)
    qshape = (1, d1) if key_axis == 0 else (d0, 1)

    def fold(f):
        if key_axis == 0:
            return jnp.sum(f.reshape(SUBLANE, d0 // SUBLANE, d1), axis=0)
        part = f[:, :LANE]
        for j in range(1, d1 // LANE):
            part = part + f[:, j * LANE:(j + 1) * LANE]
        return part

    def count(pred):
        def body(c, cnt):
            return cnt + fold(jnp.where(pred(score_ref[c], c), 1.0, 0.0))
        zero = jnp.zeros((d0 // SUBLANE, d1) if key_axis == 0 else (d0, LANE), F32)
        return jnp.sum(lax.fori_loop(0, nc, body, zero), axis=key_axis, keepdims=True)

    nonneg = count(lambda s, c: s >= 0.0)
    v0 = jnp.where(nonneg >= k_top, 0, INT_MIN).astype(I32)

    def bit_body(b, carry):
        v, at_least = carry
        cand = v | jnp.left_shift(jnp.int32(1), 30 - b)
        cf = _key_to_float(cand)
        cnt = count(lambda s, c: s >= cf)
        ok = cnt >= k_top
        return jnp.where(ok, cand, v), jnp.where(ok, cnt, at_least)

    v, at_least = lax.fori_loop(0, 31, bit_body, (v0, nonneg))
    thr = _key_to_float(jnp.maximum(v, KEY_NEG_INF))

    tied = (at_least > k_top) & (thr > -jnp.inf)
    all_keys = jnp.full(qshape, 2 ** n_keys_log2 - 1, I32)

    def tie_search():
        need = k_top - count(lambda s, c: s > thr)

        def tie_body(b, u):
            cand = u | jnp.left_shift(jnp.int32(1), n_keys_log2 - 1 - b)
            below = count(lambda s, c: (s == thr) & (c * kc + kidx0 < cand))
            return jnp.where(below < need, cand, u)

        return lax.fori_loop(0, n_keys_log2, tie_body, jnp.zeros(qshape, I32))

    u = lax.cond(jnp.max(jnp.where(tied, 1.0, 0.0)) > 0.0, tie_search, lambda: all_keys)

    def bias_body(c, carry):
        s = score_ref[c]
        sel = (s > -jnp.inf) & ((s > thr) | ((s == thr) & (c * kc + kidx0 <= u)))
        bias_ref[c] = jnp.where(sel, 0.0, NEG)
        return carry

    lax.fori_loop(0, nc, bias_body, 0)


def _dsa_prompt_kernel(qa_t_ref, qi_t_ref, wi_t_ref, kib_ref, kab_ref, va_tc_ref, o_ref, score_ref, bias_ref, *, k_top):
    nchunks, kc, q = score_ref.shape
    i = pl.program_id(1)
    nc = (i * q + q - 1) // kc + 1
    key0 = lax.broadcasted_iota(I32, (kc, q), 0)
    qpos = i * q + lax.broadcasted_iota(I32, (kc, q), 1)
    w = wi_t_ref[0] * (H_IDX ** -0.5 * D_IDX ** -0.5)
    qi_all = jnp.concatenate([qi_t_ref[0, h * D_IDX:(h + 1) * D_IDX, :] for h in range(H_IDX)], axis=1)

    def score_body(c, carry):
        d = _dot(kib_ref[pl.ds(pl.multiple_of(c * kc, kc), kc), :], qi_all)
        acc = jnp.zeros((kc, q), F32)
        for h in range(H_IDX):
            acc = acc + jnp.maximum(d[:, h * q:(h + 1) * q], 0.0) * w[h:h + 1, :]
        score_ref[c] = jnp.where(c * kc + key0 <= qpos, acc, -jnp.inf)
        return carry

    lax.fori_loop(0, nc, score_body, 0)
    _select_bias(score_ref, bias_ref, nc, k_top, (nchunks * kc - 1).bit_length(), key_axis=0)

    qa_all = jnp.concatenate([qa_t_ref[0, h * LANE:(h + 1) * LANE, :] for h in range(H_A)], axis=1)
    qa_all = qa_all * SOFTMAX_SCALE

    hs = H_A // DSA_HEAD_SPLITS
    q_parts = [qa_all[:, g * hs * q:(g + 1) * hs * q] for g in range(DSA_HEAD_SPLITS)]

    dv = KV_A * HEAD_DIM
    ones_rows = jnp.ones((BF16_ROWS, kc), BF16)

    def att_body(c, carry):
        k = kab_ref[pl.ds(pl.multiple_of(c * kc, kc), kc), :]
        v = jnp.concatenate([va_tc_ref[c], ones_rows], axis=0)
        b = bias_ref[c]
        splits = range(DSA_HEAD_SPLITS)
        qk = [_dot(k, q_parts[g]) for g in splits]
        s = [jnp.concatenate([qk[g][:, h * q:(h + 1) * q] + b for h in range(hs)], axis=1) for g in splits]
        m_new = [jnp.maximum(carry[g][0], jnp.max(s[g], axis=0, keepdims=True)) for g in splits]
        p = [jnp.exp((s[g] - m_new[g]).astype(BF16)) for g in splits]
        pv = [_dot(v, p[g]) for g in splits]
        out = []
        for g in splits:
            m, l, acc = carry[g]
            alpha = jnp.exp(m - m_new[g])
            out.append((m_new[g], alpha * l + pv[g][dv:dv + 1], alpha * acc + pv[g][:dv]))
        return tuple(out)

    init = tuple((jnp.full((1, hs * q), NEG, F32), jnp.zeros((1, hs * q), F32),
                  jnp.zeros((KV_A * HEAD_DIM, hs * q), F32)) for _ in range(DSA_HEAD_SPLITS))
    final = lax.fori_loop(0, nc, att_body, init)
    o = jnp.concatenate([acc / l for _, l, acc in final], axis=1)
    heads_per_group = H_A // KV_A
    for j in range(H_A // 2):
        g0 = ((2 * j) // heads_per_group) * HEAD_DIM
        pair = jnp.concatenate([o[g0:g0 + HEAD_DIM, (2 * j) * q:(2 * j + 1) * q],
                                o[g0:g0 + HEAD_DIM, (2 * j + 1) * q:(2 * j + 2) * q]], axis=0)
        o_ref[:, j * LANE:(j + 1) * LANE] = pair.T.astype(BF16)


def _dsa_prompt(qa_t, qi_t, wi_t, kib, kab, va_tc, bsz, t):
    q, kc = DSA_Q, DSA_KC
    nq, nchunks = t // q, t // kc
    tile_t = lambda b, i: (b, 0, i)
    return pl.pallas_call(
        functools.partial(_dsa_prompt_kernel, k_top=min(TOPK_IDX, t // 4)),
        grid=(bsz, nq),
        in_specs=[pl.BlockSpec((1, Q_EXP, q), tile_t), pl.BlockSpec((1, H_IDX * D_IDX, q), tile_t),
                  pl.BlockSpec((1, H_IDX, q), tile_t),
                  pl.BlockSpec((t, D_IDX), lambda b, i: (b, 0)), pl.BlockSpec((t, KV_A * HEAD_DIM), lambda b, i: (b, 0)),
                  pl.BlockSpec((nchunks, KV_A * HEAD_DIM, kc), lambda b, i: (b, 0, 0))],
        out_specs=pl.BlockSpec((q, H_A * HEAD_DIM), lambda b, i: (b * nq + i, 0)),
        out_shape=jax.ShapeDtypeStruct((bsz * t, H_A * HEAD_DIM), BF16),
        scratch_shapes=[pltpu.VMEM((nchunks, kc, q), F32), pltpu.VMEM((nchunks, kc, q), F32)],
        compiler_params=_params("parallel", "arbitrary"), name="dsa_prompt",
    )(qa_t, qi_t, wi_t, kib, kab, va_tc)


def _top_blocks(gate, valid, idxf, n_sel, axis):
    gate = jnp.where(valid, gate, -jnp.inf)
    sel = jnp.zeros(gate.shape, jnp.bool_)
    for _ in range(n_sel):
        mx = jnp.max(gate, axis=axis, keepdims=True)
        first = jnp.min(jnp.where(gate == mx, idxf, 1e9), axis=axis, keepdims=True)
        pick = idxf == first
        sel = sel | (pick & valid)
        gate = jnp.where(pick, -jnp.inf, gate)
    return sel


def _moba_prompt_kernel(qb_t_ref, km_ref, kb_ref, vb_tc_ref, o_ref, *, n_sel):
    qb = o_ref.shape[0]
    nb = km_ref.shape[0]
    j = pl.program_id(1)
    r = 2 * qb
    nrow = lax.broadcasted_iota(I32, (nb, r), 0)
    nrowf = nrow.astype(F32)
    qc = lax.broadcasted_iota(I32, (qb, r), 1)
    own_mask = lax.broadcasted_iota(I32, (qb, r), 0) <= jnp.where(qc >= qb, qc - qb, qc)
    own = pl.multiple_of(j * qb, qb)

    pairs = range(H_B // 2)
    sls = [slice(p * LANE, (p + 1) * LANE) for p in pairs]
    ones_rows = jnp.ones((BF16_ROWS, MOBA_BLOCK), BF16)

    def pv_and_sum(n, p, pr):
        return _dot(jnp.concatenate([vb_tc_ref[n, sls[p], :], ones_rows], axis=0), pr)

    q_raw = [jnp.concatenate([qb_t_ref[0, (2 * p) * LANE:(2 * p + 1) * LANE, :],
                              qb_t_ref[0, (2 * p + 1) * LANE:(2 * p + 2) * LANE, :]], axis=1) for p in pairs]
    gates = [_dot(km_ref[:, sls[p]], q_raw[p].astype(F32), precision=_HI) for p in pairs]
    selbiases = [jnp.where(_top_blocks(gates[p], nrow < j, nrowf, n_sel, axis=0), 0.0, NEG) for p in pairs]
    qes = [q * SOFTMAX_SCALE for q in q_raw]
    s_own = [jnp.where(own_mask, _dot(kb_ref[pl.ds(own, qb), sls[p]], qes[p]), NEG) for p in pairs]
    m_own = [jnp.max(s, axis=0, keepdims=True) for s in s_own]
    pv_own = [pv_and_sum(j, p, jnp.exp((s_own[p] - m_own[p]).astype(BF16))) for p in pairs]
    init = [(m_own[p], pv_own[p][LANE:LANE + 1], pv_own[p][:LANE]) for p in pairs]

    half = lax.broadcasted_iota(I32, (2 * MOBA_BLOCK, r), 0) < MOBA_BLOCK

    def body(n2, carry):
        start = pl.multiple_of(n2 * (2 * MOBA_BLOCK), 2 * MOBA_BLOCK)
        qk = [_dot(kb_ref[pl.ds(start, 2 * MOBA_BLOCK), sls[p]], qes[p]) for p in pairs]
        rowb = [[jnp.sum(jnp.where(nrow == 2 * n2 + i, selbiases[p], 0.0), axis=0, keepdims=True) for i in (0, 1)]
                for p in pairs]
        s = [qk[p] + jnp.where(half, rowb[p][0], rowb[p][1]) for p in pairs]
        m_new = [jnp.maximum(carry[p][0], jnp.max(s[p], axis=0, keepdims=True)) for p in pairs]
        pr = [jnp.exp((s[p] - m_new[p]).astype(BF16)) for p in pairs]
        pv = [pv_and_sum(2 * n2, p, pr[p][:MOBA_BLOCK]) + pv_and_sum(2 * n2 + 1, p, pr[p][MOBA_BLOCK:]) for p in pairs]
        out = []
        for p in pairs:
            m, l, acc = carry[p]
            alpha = jnp.exp(m - m_new[p])
            out.append((m_new[p], alpha * l + pv[p][LANE:LANE + 1], alpha * acc + pv[p][:LANE]))
        return tuple(out)

    final = lax.fori_loop(0, (j + 1) // 2, body, tuple(init))
    for p in pairs:
        _, l, acc = final[p]
        o = acc / l
        pair = jnp.concatenate([o[:HEAD_DIM, :qb], o[HEAD_DIM:, qb:]], axis=0)
        o_ref[:, sls[p]] = pair.T.astype(BF16)


def _moba_prompt(qb_t, kmean, kbb, vb_tc, bsz, t):
    nb = t // MOBA_BLOCK
    return pl.pallas_call(
        functools.partial(_moba_prompt_kernel, n_sel=min(TOPK_BLK, nb - 1)),
        grid=(bsz, nb),
        in_specs=[pl.BlockSpec((1, Q_EXP, MOBA_BLOCK), lambda b, i: (b, 0, i)), pl.BlockSpec((nb, WKV), lambda b, i: (b, 0)),
                  pl.BlockSpec((t, WKV), lambda b, i: (b, 0)),
                  pl.BlockSpec((nb, WKV, MOBA_BLOCK), lambda b, i: (b, 0, 0))],
        out_specs=pl.BlockSpec((MOBA_BLOCK, WKV), lambda b, i: (b * nb + i, 0)),
        out_shape=jax.ShapeDtypeStruct((bsz * t, WKV), BF16),
        compiler_params=_params("parallel", "arbitrary"), name="moba_prompt",
    )(qb_t, kmean, kbb, vb_tc)


GSEL_LANE = LANE - 1


def _route(rl, group=None):
    lanef = lax.broadcasted_iota(I32, rl.shape, 1).astype(F32)

    def masked_softmax(mask):
        z = jnp.where(mask, rl, -jnp.inf)
        e = jnp.exp(z - jnp.max(z, axis=1, keepdims=True))
        return e / jnp.sum(e, axis=1, keepdims=True)

    def first_lane(cond):
        return jnp.min(jnp.where(cond, lanef, 1e9), axis=1, keepdims=True)

    gmask = lanef < N_GROUPS
    gp = masked_softmax(gmask)
    if group is None:
        gprob = jnp.max(gp, axis=1, keepdims=True)
        gsel = first_lane((gp == gprob) & gmask)
    else:
        gsel = jnp.full((rl.shape[0], 1), group, I32).astype(F32)
        gprob = jnp.sum(jnp.where(lanef == gsel, gp, 0.0), axis=1, keepdims=True)
    lo = N_GROUPS + EXPERTS_PER_GROUP * gsel
    emask = (lanef >= lo) & (lanef < lo + EXPERTS_PER_GROUP)
    ep = jnp.where(emask, masked_softmax(emask), -1.0)
    p1 = jnp.max(ep, axis=1, keepdims=True)
    i1 = first_lane(ep == p1)
    ep = jnp.where(lanef == i1, -1.0, ep)
    p2 = jnp.max(ep, axis=1, keepdims=True)
    i2 = first_lane(ep == p2)
    den = p1 + p2
    cw = jnp.where(lanef == i1, gprob * p1 / den, 0.0) + jnp.where(lanef == i2, gprob * p2 / den, 0.0)
    return jnp.where(lanef == GSEL_LANE, gsel, cw)


def _merge_kernel(oa_ref, ob_ref, ga_ref, gb_ref, x_ref, wa_ref, wb_ref, wo_ref, gf_ref, wr_ref, br_ref,
                  x1_ref, h2_ref, cw_ref):
    tm = x_ref.shape[0]
    u = jax.nn.sigmoid(ga_ref[...]) * _dot(oa_ref[...], wa_ref[...]) \
        + jax.nn.sigmoid(gb_ref[...]) * _dot(ob_ref[...], wb_ref[...])
    x1 = x_ref[...] + _dot(u.astype(BF16), wo_ref[...])
    x1_ref[...] = x1
    h2 = (x1 * _rms_scale(x1)) * gf_ref[...]
    h2_ref[...] = h2.astype(BF16)

    cw_ref[...] = _route(_dot(h2, wr_ref[...], precision=_HI) + br_ref[...])


def _merge(oa, ob, ga, gb, x2d, wa, wb, wo, g_ffn, w_router, b_router, tm):
    n, d = x2d.shape
    row = lambda i: (i, 0)
    const = lambda i: (0, 0)
    ha = oa.shape[1]
    return pl.pallas_call(
        _merge_kernel, grid=(n // tm,),
        in_specs=[pl.BlockSpec((tm, ha), row), pl.BlockSpec((tm, ha), row), pl.BlockSpec((tm, d), row),
                  pl.BlockSpec((tm, d), row), pl.BlockSpec((tm, d), row), pl.BlockSpec((ha, d), const),
                  pl.BlockSpec((ha, d), const), pl.BlockSpec((d, d), const), pl.BlockSpec((1, d), const),
                  pl.BlockSpec((d, LANE), const), pl.BlockSpec((1, LANE), const)],
        out_specs=[pl.BlockSpec((tm, d), row), pl.BlockSpec((tm, d), row), pl.BlockSpec((tm, LANE), row)],
        out_shape=[jax.ShapeDtypeStruct((n, d), F32), jax.ShapeDtypeStruct((n, d), BF16),
                   jax.ShapeDtypeStruct((n, LANE), F32)],
        compiler_params=_params("parallel"), name="merge",
    )(oa, ob, ga, gb, x2d, wa, wb, wo, g_ffn.reshape(1, d), w_router, b_router)


def _moe_kernel(h_ref, cw_ref, x_ref, wg_ref, wu_ref, wd_ref, gfin_ref, y_ref, acc_ref):
    tm = h_ref.shape[0]
    e = pl.program_id(1)

    @pl.when(e == 0)
    def _():
        acc_ref[...] = jnp.zeros_like(acc_ref)

    h = h_ref[...]
    a = _dot(h, wg_ref[0])
    hdn = (a * jax.nn.sigmoid(a)) * _dot(h, wu_ref[0])
    y = _dot(hdn.astype(BF16), wd_ref[0])
    lane = lax.broadcasted_iota(I32, (tm, LANE), 1)
    col = jnp.sum(jnp.where(lane == e + N_GROUPS, cw_ref[...], 0.0), axis=1, keepdims=True)
    acc_ref[...] += y * col

    @pl.when(e == pl.num_programs(1) - 1)
    def _():
        out = x_ref[...] + acc_ref[...]
        y_ref[...] = (out * _rms_scale(out)) * gfin_ref[...]


def _moe(h2, cw, x1, wg, wu, wd, g_final, tm):
    n, d = x1.shape
    ne, _, de = wg.shape
    row = lambda i, e: (i, 0)
    return pl.pallas_call(
        _moe_kernel, grid=(n // tm, ne),
        in_specs=[pl.BlockSpec((tm, d), row), pl.BlockSpec((tm, LANE), row), pl.BlockSpec((tm, d), row),
                  pl.BlockSpec((1, d, de), lambda i, e: (e, 0, 0)), pl.BlockSpec((1, d, de), lambda i, e: (e, 0, 0)),
                  pl.BlockSpec((1, de, d), lambda i, e: (e, 0, 0)), pl.BlockSpec((1, d), lambda i, e: (0, 0))],
        out_specs=pl.BlockSpec((tm, d), row),
        out_shape=jax.ShapeDtypeStruct((n, d), F32),
        scratch_shapes=[pltpu.VMEM((tm, d), F32)],
        compiler_params=_params("parallel", "arbitrary"), name="moe",
    )(h2, cw, x1, wg, wu, wd, g_final.reshape(1, d))


def _scatter_rows_kernel(dest_ref, x_ref, zeros_ref, xs_ref, sem):
    del zeros_ref
    tm = x_ref.shape[0]
    base = pl.program_id(0) * tm

    def row_copy(r, dst):
        return pltpu.make_async_copy(x_ref.at[pl.ds(r, 1)], xs_ref.at[pl.ds(dst, 1)], sem)

    def start(r, carry):
        row_copy(r, dest_ref[base + r]).start()
        return carry

    def wait(r, carry):
        row_copy(r, 0).wait()
        return carry

    lax.fori_loop(0, tm, start, 0, unroll=ROW_DMA_UNROLL)
    lax.fori_loop(0, tm, wait, 0, unroll=ROW_DMA_UNROLL)


def _scatter_rows(dest, x, n_rows, tm):
    n, d = x.shape
    grid_spec = pltpu.PrefetchScalarGridSpec(
        num_scalar_prefetch=1, grid=(n // tm,),
        in_specs=[pl.BlockSpec((tm, d), lambda i, dest: (i, 0)), pl.BlockSpec(memory_space=pl.ANY)],
        out_specs=pl.BlockSpec(memory_space=pl.ANY),
        scratch_shapes=[pltpu.SemaphoreType.DMA(())])
    return pl.pallas_call(
        _scatter_rows_kernel, grid_spec=grid_spec, out_shape=jax.ShapeDtypeStruct((n_rows, d), x.dtype),
        input_output_aliases={2: 0}, compiler_params=_params("arbitrary"), name="moe_scatter_rows",
    )(dest, x, jnp.zeros((n_rows, d), x.dtype))


def _gather_rows_kernel(dest_ref, ys_ref, y_ref, sem):
    tm = y_ref.shape[0]
    base = pl.program_id(0) * tm

    def row_copy(r, src):
        return pltpu.make_async_copy(ys_ref.at[pl.ds(src, 1)], y_ref.at[pl.ds(r, 1)], sem)

    def start(r, carry):
        row_copy(r, dest_ref[base + r]).start()
        return carry

    def wait(r, carry):
        row_copy(r, 0).wait()
        return carry

    lax.fori_loop(0, tm, start, 0, unroll=ROW_DMA_UNROLL)
    lax.fori_loop(0, tm, wait, 0, unroll=ROW_DMA_UNROLL)


def _gather_rows(dest, ys, n, tm):
    d = ys.shape[1]
    grid_spec = pltpu.PrefetchScalarGridSpec(
        num_scalar_prefetch=1, grid=(n // tm,),
        in_specs=[pl.BlockSpec(memory_space=pl.ANY)],
        out_specs=pl.BlockSpec((tm, d), lambda i, dest: (i, 0)),
        scratch_shapes=[pltpu.SemaphoreType.DMA(())])
    return pl.pallas_call(
        _gather_rows_kernel, grid_spec=grid_spec, out_shape=jax.ShapeDtypeStruct((n, d), ys.dtype),
        compiler_params=_params("arbitrary"), name="moe_gather_rows",
    )(dest, ys)


def _moe_grouped_kernel(tg_ref, used_ref, x_ref, gf_ref, wr_ref, br_ref, wg_ref, wu_ref, wd_ref, gfin_ref, y_ref,
                        h_ref, cw_ref, acc_ref):
    tm = x_ref.shape[0]
    i, e = pl.program_id(0), pl.program_id(1)
    group = tg_ref[i]
    live = i < used_ref[0]
    last = e == pl.num_programs(1) - 1

    @pl.when(live & (e == 0))
    def _():
        x1 = x_ref[...]
        h2 = (x1 * _rms_scale(x1)) * gf_ref[...]
        h_ref[...] = h2.astype(BF16)
        cw_ref[...] = _route(_dot(h2, wr_ref[...], precision=_HI) + br_ref[...], group)
        acc_ref[...] = jnp.zeros_like(acc_ref)

    @pl.when(live)
    def _():
        h = h_ref[...]
        a = _dot(h, wg_ref[0])
        hdn = (a * jax.nn.sigmoid(a)) * _dot(h, wu_ref[0])
        y = _dot(hdn.astype(BF16), wd_ref[0])
        lane = lax.broadcasted_iota(I32, (tm, LANE), 1)
        expert_lane = N_GROUPS + group * EXPERTS_PER_GROUP + e
        col = jnp.sum(jnp.where(lane == expert_lane, cw_ref[...], 0.0), axis=1, keepdims=True)
        acc_ref[...] += y * col

    @pl.when(live & last)
    def _():
        out = x_ref[...] + acc_ref[...]
        y_ref[...] = (out * _rms_scale(out)) * gfin_ref[...]

    @pl.when(jnp.logical_not(live) & last)
    def _():
        y_ref[...] = jnp.zeros_like(y_ref)


def _moe_grouped(tile_group, tiles_used, xs, g_ffn, w_router, b_router, wg, wu, wd, g_final, tm):
    p, d = xs.shape
    de = wg.shape[2]
    row = lambda i, e, tg, used: (i, 0)
    const = lambda i, e, tg, used: (0, 0)
    expert = lambda i, e, tg, used: (tg[i] * EXPERTS_PER_GROUP + e, 0, 0)
    grid_spec = pltpu.PrefetchScalarGridSpec(
        num_scalar_prefetch=2, grid=(p // tm, EXPERTS_PER_GROUP),
        in_specs=[pl.BlockSpec((tm, d), row), pl.BlockSpec((1, d), const), pl.BlockSpec((d, LANE), const),
                  pl.BlockSpec((1, LANE), const), pl.BlockSpec((1, d, de), expert), pl.BlockSpec((1, d, de), expert),
                  pl.BlockSpec((1, de, d), expert), pl.BlockSpec((1, d), const)],
        out_specs=pl.BlockSpec((tm, d), row),
        scratch_shapes=[pltpu.VMEM((tm, d), BF16), pltpu.VMEM((tm, LANE), F32), pltpu.VMEM((tm, d), F32)])
    return pl.pallas_call(
        _moe_grouped_kernel, grid_spec=grid_spec, out_shape=jax.ShapeDtypeStruct((p, d), F32),
        compiler_params=_params("parallel", "arbitrary"), name="moe_grouped",
    )(tile_group, tiles_used, xs, g_ffn.reshape(1, d), w_router, b_router, wg, wu, wd, g_final.reshape(1, d))


def _group_dispatch(group, tm):
    n = group.shape[0]
    onehot = (group[:, None] == jnp.arange(N_GROUPS, dtype=I32)[None, :]).astype(I32)
    rank = jnp.cumsum(onehot, axis=0) - onehot
    counts = jnp.sum(onehot, axis=0)
    padded = (counts + tm - 1) // tm * tm
    ends = jnp.cumsum(padded)
    dest = jnp.sum(onehot * ((ends - padded)[None, :] + rank), axis=1).astype(I32)
    rows = -(-n // tm) * tm + N_GROUPS * tm
    tiles = rows // tm
    tile_start = jnp.arange(tiles, dtype=I32) * tm
    tile_group = jnp.minimum(jnp.sum((tile_start[:, None] >= ends[None, :]).astype(I32), axis=1), N_GROUPS - 1)
    return dest, tile_group, (ends[-1:] // tm).astype(I32), rows


def _page_specs(shape_tail, pages_per_step):
    zeros = (0,) * len(shape_tail)
    return [pl.BlockSpec((None,) + shape_tail,
                         functools.partial(lambda b, j, pt, t: (pt[b, j * pages_per_step + t],) + zeros, t=t))
            for t in range(pages_per_step)]


def _stream_pages(pt_ref, pools_and_bufs, sem_ref, pages):
    b, j = pl.program_id(0), pl.program_id(1)
    nj = pl.num_programs(1)
    total = pl.num_programs(0) * nj
    step = b * nj + j
    slot = step % 2

    def copy(pool_ref, buf_ref, page, sl, t):
        return pltpu.make_async_copy(pool_ref.at[page], buf_ref.at[sl, t], sem_ref.at[sl])

    def start(bb, jj, sl):
        for t in range(pages):
            page = pt_ref[bb, jj * pages + t]
            for pool_ref, buf_ref in pools_and_bufs:
                copy(pool_ref, buf_ref, page, sl, t).start()

    @pl.when(step == 0)
    def _():
        start(b, j, slot)

    @pl.when(step + 1 < total)
    def _():
        nxt = step + 1
        start(nxt // nj, nxt % nj, 1 - slot)

    for t in range(pages):
        for pool_ref, buf_ref in pools_and_bufs:
            copy(pool_ref, buf_ref, 0, slot, t).wait()
    return slot


def _dsa_sample_score_kernel(pt_ref, qi_ref, w_ref, kin_ref, pool_ref, s_ref, snew_ref, kbuf_ref, sem_ref, *, pages):
    slot = _stream_pages(pt_ref, [(pool_ref, kbuf_ref)], sem_ref, pages)
    j = pl.program_id(1)
    rows = qi_ref.shape[1]
    s_len = rows // H_IDX
    q = qi_ref[0][:, :D_IDX]
    w = w_ref[0] * (H_IDX ** -0.5 * D_IDX ** -0.5)

    def scores(d):
        d = jnp.maximum(d, 0.0) * w
        return jnp.sum(d.reshape(s_len, H_IDX, d.shape[1]), axis=1)

    k_t = jnp.concatenate([kbuf_ref[slot, t].astype(BF16) for t in range(pages)], axis=1)
    s_ref[0] = scores(_dot(q, k_t))

    @pl.when(j == pl.num_programs(1) - 1)
    def _():
        sn = scores(_dot_nt(q, kin_ref[0][:, :D_IDX].astype(BF16)))
        si = lax.broadcasted_iota(I32, (s_len, PAGE_SIZE), 0)
        ki = lax.broadcasted_iota(I32, (s_len, PAGE_SIZE), 1)
        snew_ref[0] = jnp.where(ki <= si, sn, -jnp.inf)


def _dsa_sample_scores(page_table, qi_s, wi_s, kin_pad, pool_kidx, pages):
    bd, n_pages = page_table.shape
    rows = qi_s.shape[1]
    s_len = rows // H_IDX
    per_b = lambda b, j, pt: (b, 0, 0)
    grid_spec = pltpu.PrefetchScalarGridSpec(
        num_scalar_prefetch=1, grid=(bd, n_pages // pages),
        in_specs=[pl.BlockSpec((1, rows, LANE), per_b), pl.BlockSpec((1, rows, 1), per_b),
                  pl.BlockSpec((1, PAGE_SIZE, LANE), per_b), pl.BlockSpec(memory_space=pl.ANY)],
        out_specs=[pl.BlockSpec((1, s_len, pages * PAGE_SIZE), lambda b, j, pt: (b, 0, j)),
                   pl.BlockSpec((1, s_len, PAGE_SIZE), per_b)],
        scratch_shapes=[pltpu.VMEM((2, pages, D_IDX, PAGE_SIZE), F32), pltpu.SemaphoreType.DMA((2,))])
    return pl.pallas_call(
        functools.partial(_dsa_sample_score_kernel, pages=pages), grid_spec=grid_spec,
        out_shape=[jax.ShapeDtypeStruct((bd, s_len, n_pages * PAGE_SIZE), F32),
                   jax.ShapeDtypeStruct((bd, s_len, PAGE_SIZE), F32)],
        compiler_params=_params("arbitrary", "arbitrary"), name="dsa_sample_scores",
    )(page_table, qi_s, wi_s, kin_pad, pool_kidx)


def _dsa_sample_select_kernel(sp_ref, sn_ref, bp_ref, bn_ref, score_ref, bias_ref, *, k_top):
    nch, q, kc = score_ref.shape
    npast = nch - 1

    def load(c, carry):
        score_ref[c] = sp_ref[:, pl.ds(pl.multiple_of(c * kc, kc), kc)]
        return carry

    lax.fori_loop(0, npast, load, 0)
    score_ref[npast] = jnp.concatenate([sn_ref[...], jnp.full((q, kc - LANE), -jnp.inf, F32)], axis=1)
    _select_bias(score_ref, bias_ref, nch, k_top, (nch * kc - 1).bit_length(), key_axis=1)

    def store(c, carry):
        bp_ref[:, pl.ds(pl.multiple_of(c * kc, kc), kc)] = bias_ref[c]
        return carry

    lax.fori_loop(0, npast, store, 0)
    bn_ref[...] = bias_ref[npast][:, :LANE]


def _dsa_sample_select(s_past, s_new, k_top):
    q, past = s_past.shape
    kc = 512
    nch = past // kc + 1
    return pl.pallas_call(
        functools.partial(_dsa_sample_select_kernel, k_top=k_top),
        out_shape=[jax.ShapeDtypeStruct((q, past), F32), jax.ShapeDtypeStruct((q, LANE), F32)],
        scratch_shapes=[pltpu.VMEM((nch, q, kc), F32), pltpu.VMEM((nch, q, kc), F32)],
        compiler_params=pltpu.CompilerParams(vmem_limit_bytes=VMEM_LIMIT), name="dsa_sample_select",
    )(s_past, s_new)


def _dsa_sample_attn_kernel(pt_ref, qa_ref, bp_ref, bn_ref, kn_ref, vn_ref, kpool_ref, vpool_ref, o_ref,
                            m_ref, l_ref, acc_ref, kbuf_ref, vbuf_ref, sem_ref, *, pages):
    slot = _stream_pages(pt_ref, [(kpool_ref, kbuf_ref), (vpool_ref, vbuf_ref)], sem_ref, pages)
    j = pl.program_id(1)
    rows = qa_ref.shape[1]
    s_len = rows // H_A
    q = qa_ref[0]
    scale = HEAD_DIM ** -0.5

    @pl.when(j == 0)
    def _():
        m_ref[...] = jnp.full_like(m_ref, NEG)
        l_ref[...] = jnp.zeros_like(l_ref)
        acc_ref[...] = jnp.zeros_like(acc_ref)

    def update(s, bias, v, v_dot):
        n = s.shape[1]
        s = (s.reshape(s_len, H_A, n) * scale + bias[:, None, :]).reshape(rows, n)
        m = m_ref[...]
        m_new = jnp.maximum(m, jnp.max(s, axis=1, keepdims=True))
        alpha = jnp.exp(m - m_new)
        p = jnp.exp(s - m_new)
        l_ref[...] = alpha * l_ref[...] + jnp.sum(p, axis=1, keepdims=True)
        acc_ref[...] = alpha * acc_ref[...] + v_dot(p.astype(BF16), v)
        m_ref[...] = m_new

    k_t = jnp.concatenate([kbuf_ref[slot, t].astype(BF16) for t in range(pages)], axis=1)
    v_t = jnp.concatenate([vbuf_ref[slot, t].astype(BF16) for t in range(pages)], axis=1)
    update(_dot(q, k_t), bp_ref[0], v_t, _dot_nt)

    @pl.when(j == pl.num_programs(1) - 1)
    def _():
        update(_dot_nt(q, kn_ref[0].astype(BF16)), bn_ref[0], vn_ref[0].astype(BF16), _dot)
        o_ref[0] = acc_ref[...] / l_ref[...]


def _dsa_sample_attn(page_table, qa_s, bias_past, bias_new, kn_pad, vn_pad, pool_k, pool_v, pages):
    bd, n_pages = page_table.shape
    rows = qa_s.shape[1]
    s_len = rows // H_A
    per_b = lambda b, j, pt: (b, 0, 0)
    grid_spec = pltpu.PrefetchScalarGridSpec(
        num_scalar_prefetch=1, grid=(bd, n_pages // pages),
        in_specs=[pl.BlockSpec((1, rows, LANE), per_b),
                  pl.BlockSpec((1, s_len, pages * PAGE_SIZE), lambda b, j, pt: (b, 0, j)),
                  pl.BlockSpec((1, s_len, LANE), per_b),
                  pl.BlockSpec((1, PAGE_SIZE, LANE), per_b), pl.BlockSpec((1, PAGE_SIZE, LANE), per_b),
                  pl.BlockSpec(memory_space=pl.ANY), pl.BlockSpec(memory_space=pl.ANY)],
        out_specs=pl.BlockSpec((1, rows, LANE), per_b),
        scratch_shapes=[pltpu.VMEM((rows, 1), F32), pltpu.VMEM((rows, 1), F32), pltpu.VMEM((rows, LANE), F32),
                        pltpu.VMEM((2, pages, KV_A * HEAD_DIM, PAGE_SIZE), F32),
                        pltpu.VMEM((2, pages, KV_A * HEAD_DIM, PAGE_SIZE), F32), pltpu.SemaphoreType.DMA((2,))])
    return pl.pallas_call(
        functools.partial(_dsa_sample_attn_kernel, pages=pages), grid_spec=grid_spec,
        out_shape=jax.ShapeDtypeStruct((bd, rows, LANE), F32),
        compiler_params=_params("arbitrary", "arbitrary"), name="dsa_sample_attn",
    )(page_table, qa_s, bias_past, bias_new, kn_pad, vn_pad, pool_k, pool_v)


def _moba_sample_kernel(pt_ref, qb_ref, kn_ref, vn_ref, *refs, pages, n_sel):
    k_refs, v_refs = refs[:pages], refs[pages:2 * pages]
    o_ref, gate_ref, m_ref, l_ref, acc_ref = refs[2 * pages:]
    del pt_ref
    j = pl.program_id(1)
    rows = qb_ref.shape[1]
    s_len = rows // H_B
    nbp = acc_ref.shape[0]
    ppb = MOBA_BLOCK // PAGE_SIZE
    scale = HEAD_DIM ** -0.5

    qs = qb_ref[0]
    pair = (lax.broadcasted_iota(I32, (rows, LANE), 0) % H_B) // 2
    qe = jnp.concatenate([jnp.where(pair == p, qs, jnp.zeros_like(qs)) for p in range(H_B // 2)], axis=1)
    lane = lax.broadcasted_iota(I32, (rows, LANE), 1)

    @pl.when(j == 0)
    def _():
        gate_ref[...] = jnp.full_like(gate_ref, -jnp.inf)
        m_ref[...] = jnp.full_like(m_ref, NEG)
        l_ref[...] = jnp.zeros_like(l_ref)

    nblk = pages // ppb
    blocks = [slice(t * MOBA_BLOCK, (t + 1) * MOBA_BLOCK) for t in range(nblk)]
    k_t = jnp.concatenate([k_refs[t][...].astype(BF16) for t in range(pages)], axis=1)
    qk = _dot(qe, k_t)
    s = qk * scale
    gate_t = [jnp.sum(qk[:, blk], axis=1, keepdims=True) * (1.0 / MOBA_BLOCK) for blk in blocks]
    m_t = [jnp.max(s[:, blk], axis=1, keepdims=True) for blk in blocks]
    p = jnp.exp(s - jnp.concatenate([jnp.broadcast_to(m, (rows, MOBA_BLOCK)) for m in m_t], axis=1))
    l_t = [jnp.sum(p[:, blk], axis=1, keepdims=True) for blk in blocks]
    pb = p.astype(BF16)
    gate, m_blk, l_blk = gate_ref[...], m_ref[...], l_ref[...]
    for t in range(nblk):
        n = j * nblk + t
        v_t = jnp.concatenate([v_refs[ppb * t + i][...].astype(BF16) for i in range(ppb)], axis=1)
        acc_ref[n] = _dot_nt(pb[:, blocks[t]], v_t)
        here = lane == n
        gate = jnp.where(here, gate_t[t], gate)
        m_blk = jnp.where(here, m_t[t], m_blk)
        l_blk = jnp.where(here, l_t[t], l_blk)
    gate_ref[...] = gate
    m_ref[...] = m_blk
    l_ref[...] = l_blk

    @pl.when(j == pl.num_programs(1) - 1)
    def _():
        sel = _top_blocks(gate_ref[...], lane < nbp, lane.astype(F32), n_sel, axis=1)
        s_own = _dot_nt(qe, kn_ref[0].astype(BF16)) * scale
        srow = lax.broadcasted_iota(I32, (rows, LANE), 0) // H_B
        s_own = jnp.where(lane <= srow, s_own, NEG)
        m_blk = jnp.where(sel, m_ref[...], NEG)
        m_all = jnp.maximum(jnp.max(m_blk, axis=1, keepdims=True), jnp.max(s_own, axis=1, keepdims=True))
        wgt = jnp.where(sel, jnp.exp(m_blk - m_all), 0.0)
        p_own = jnp.exp(s_own - m_all)
        l_all = jnp.sum(wgt * l_ref[...], axis=1, keepdims=True) + jnp.sum(p_own, axis=1, keepdims=True)
        o = _dot(p_own.astype(BF16), vn_ref[0].astype(BF16))
        for n in range(nbp):
            o = o + wgt[:, n:n + 1] * acc_ref[n]
        o = o / l_all
        head_lane = lax.broadcasted_iota(I32, (rows, WKV), 1) // HEAD_DIM
        head_row = lax.broadcasted_iota(I32, (rows, WKV), 0) % H_B
        o = jnp.where(head_lane == head_row, o, 0.0)
        o_ref[0] = jnp.sum(o.reshape(s_len, H_B, WKV), axis=1)


def _moba_sample(page_table, qb_s, kn_pad, vn_pad, pool_k, pool_v, pages):
    bd, n_pages = page_table.shape
    rows = qb_s.shape[1]
    s_len = rows // H_B
    nbp = n_pages * PAGE_SIZE // MOBA_BLOCK
    per_b = lambda b, j, pt: (b, 0, 0)
    grid_spec = pltpu.PrefetchScalarGridSpec(
        num_scalar_prefetch=1, grid=(bd, n_pages // pages),
        in_specs=[pl.BlockSpec((1, rows, LANE), per_b), pl.BlockSpec((1, PAGE_SIZE, WKV), per_b),
                  pl.BlockSpec((1, PAGE_SIZE, WKV), per_b)] + _page_specs((WKV, PAGE_SIZE), pages) * 2,
        out_specs=pl.BlockSpec((1, s_len, WKV), per_b),
        scratch_shapes=[pltpu.VMEM((rows, LANE), F32), pltpu.VMEM((rows, LANE), F32), pltpu.VMEM((rows, LANE), F32),
                        pltpu.VMEM((nbp, rows, WKV), F32)])
    return pl.pallas_call(
        functools.partial(_moba_sample_kernel, pages=pages, n_sel=min(TOPK_BLK, nbp)), grid_spec=grid_spec,
        out_shape=jax.ShapeDtypeStruct((bd, s_len, WKV), F32),
        compiler_params=_params("parallel", "arbitrary"), name="moba_sample",
    )(page_table, qb_s, kn_pad, vn_pad, *([pool_k] * pages), *([pool_v] * pages))


def _split_w_in(w_in):
    d = w_in.shape[0]
    sizes = (H_A * HEAD_DIM, KV_A * HEAD_DIM, KV_A * HEAD_DIM, H_IDX * D_IDX, D_IDX, H_IDX, WKV, WKV, WKV, d, d)
    parts, o = [], 0
    for s in sizes:
        parts.append(w_in[:, o:o + s])
        o += s
    return parts


def _expand_heads(w, lane_offsets):
    z = jnp.zeros((w.shape[0], HEAD_DIM), w.dtype)
    parts = []
    for h, off in enumerate(lane_offsets):
        blk = w[:, h * HEAD_DIM:(h + 1) * HEAD_DIM]
        parts += [blk, z] if off == 0 else [z, blk]
    return jnp.concatenate(parts, axis=1)


def _layout_w_in(w_in):
    d = w_in.shape[0]
    qa, ka, va, qi, ki, wi, qb, kb, vb, ga, gb = _split_w_in(w_in)
    heads_per_group = H_A // KV_A
    qa_e = _expand_heads(qa, [(h // heads_per_group) * HEAD_DIM for h in range(H_A)])
    qb_e = _expand_heads(qb, [(h % 2) * HEAD_DIM for h in range(H_B)])
    kiwi = jnp.concatenate([ki, wi, jnp.zeros((d, LANE - D_IDX - H_IDX), w_in.dtype)], axis=1)
    ki_pad = jnp.concatenate([ki, jnp.zeros((d, LANE - D_IDX), w_in.dtype)], axis=1)
    w_sample = jnp.concatenate([qa_e, _expand_heads(qi, [0] * H_IDX), qb_e, ka, va, kiwi, kb, vb, ga, gb], axis=1)
    w_t = jnp.concatenate([qa_e, qi, qb_e, ka, va, ki, kb, vb, wi, jnp.zeros_like(wi)], axis=1).T
    w_r = jnp.concatenate([ka, ki_pad, kb, ga, gb], axis=1)
    return w_sample.astype(BF16), w_t.astype(BF16), w_r.astype(BF16)


def _rope_angles(pos):
    half = ROT_DIM // 2
    inv = ROPE_THETA ** (-jnp.arange(half, dtype=F32) / half)
    ang = pos.astype(F32)[:, None] * inv[None, :]
    return jnp.cos(ang), jnp.sin(ang)


def _rope_tables(pos):
    cos, sin = _rope_angles(pos)
    n = pos.shape[0]
    pad = jnp.zeros((n, HEAD_DIM - ROT_DIM), F32)
    zero = jnp.zeros((n, ROT_DIM // 2), F32)
    c = jnp.concatenate([cos, cos, pad + 1.0], axis=1)
    s1 = jnp.concatenate([-sin, zero, pad], axis=1)
    s2 = jnp.concatenate([zero, sin, pad], axis=1)
    return tuple(jnp.tile(t, (1, LANE // HEAD_DIM)) for t in (c, s1, s2))


def _pages_transposed(pool):
    n_pool, page = pool.shape[:2]
    nd = pool.ndim
    return jnp.transpose(pool, (0,) + tuple(range(2, nd)) + (1,)).reshape(n_pool, -1, page)


def _pad_rows(x, rows):
    b, s, w = x.shape
    return jnp.concatenate([x, jnp.zeros((b, rows - s, w), x.dtype)], axis=1)


def _ffn(oa, ob, ga, gb, x2d, lw, g_final, tm_merge, tm_moe, grouped):
    x1, h2, cw = _merge(oa, ob, ga, gb, x2d, lw["wa"], lw["wb"], lw["wo"], lw["g_ffn"], lw["w_router"],
                        lw["b_router"], tm_merge)
    if not grouped:
        return _moe(h2, cw, x1, lw["wg"], lw["wu"], lw["wd"], g_final, tm_moe)
    dest, tile_group, tiles_used, rows = _group_dispatch(cw[:, GSEL_LANE].astype(I32), tm_moe)
    xs = _scatter_rows(dest, x1, rows, TM_ROWS)
    ys = _moe_grouped(tile_group, tiles_used, xs, lw["g_ffn"], lw["w_router"], lw["b_router"], lw["wg"], lw["wu"], lw["wd"],
                      g_final, tm_moe)
    return _gather_rows(dest, ys, x1.shape[0], TM_ROWS)


@jax.jit
def kernel(x_prompt, x_sample, cache_k_a, cache_v_a, cache_k_idx, cache_k_b, cache_v_b, page_table, w_in, w_br_a,
           w_br_b, w_out, g_mix, g_ffn, w_grp, b_grp, w_exp, b_exp, w_e_gate, w_e_up, w_e_down, g_final):
    bsz, t, d = x_prompt.shape
    bd, s_len, _ = x_sample.shape
    depth = w_in.shape[0]
    n_pages = page_table.shape[1]
    past = n_pages * PAGE_SIZE
    assert depth == 1, "the final RMSNorm is fused into the last layer's MoE kernel"
    assert past % MOBA_BLOCK == 0 and s_len <= PAGE_SIZE and t % TM_PROJ == 0

    xp = x_prompt.reshape(bsz * t, d)
    xs = x_sample.reshape(bd * s_len, d)
    ns = bd * s_len

    l = 0
    zpad = jnp.zeros((d, LANE - N_GROUPS - N_EXPERTS), F32)
    lw = dict(
        wa=w_br_a[l].astype(BF16), wb=w_br_b[l].astype(BF16), wo=w_out[l].astype(BF16), g_ffn=g_ffn[l],
        w_router=jnp.concatenate([w_grp[l], w_exp[l], zpad], axis=1),
        b_router=jnp.concatenate([b_grp[l], b_exp[l], zpad[0]]).reshape(1, LANE),
        wg=w_e_gate[l].astype(BF16), wu=w_e_up[l].astype(BF16), wd=w_e_down[l].astype(BF16))
    w_sample, w_t, w_r = _layout_w_in(w_in[l])

    pos_p = jnp.arange(t)
    cos_p, sin_p = _rope_angles(pos_p)
    (qa_t, qi_t, qb_t, ka_t, va_t, ki_t, kb_t, vb_t, wi_t, kab, kib, kbb, va_tc, vb_tc, kmean, ga, gb) = _project_prompt(
        xp, g_mix[l], w_t, w_r, (cos_p.T, sin_p.T), _rope_tables(pos_p), bsz, t)
    o_a = _dsa_prompt(qa_t, qi_t, wi_t, kib, kab, va_tc, bsz, t)
    o_b = _moba_prompt(qb_t, kmean.reshape(bsz * t // MOBA_BLOCK, WKV), kbb, vb_tc, bsz, t)
    y_prompt = _ffn(o_a, o_b, ga, gb, xp, lw, g_final, TM_MERGE, TM_MOE, True).reshape(bsz, t, d)

    def heads_last(x_t, heads):
        return jnp.transpose(x_t.reshape(1, bsz, heads, HEAD_DIM, t), (0, 1, 4, 2, 3))

    rows_p = (heads_last(ka_t, KV_A), heads_last(va_t, KV_A), jnp.transpose(ki_t, (0, 2, 1))[None],
              heads_last(kb_t, H_B), heads_last(vb_t, H_B))

    tabs_s = tuple(jnp.tile(tb, (bd, 1)) for tb in _rope_tables(past + jnp.arange(s_len)))
    qa, qi, qb, ka, va, kiwi, kb, vb, ga, gb = _project_sample(xs, g_mix[l], w_sample, tabs_s)
    rows = s_len * N_HEAD_SLABS
    qa_s = qa.reshape(bd, rows, LANE)
    qi_s = qi.reshape(bd, rows, LANE)
    qb_s = qb.reshape(bd, rows, LANE)
    wi_s = kiwi[:, D_IDX:D_IDX + H_IDX].reshape(bd, rows, 1)
    s_past, s_new = _dsa_sample_scores(page_table, qi_s, wi_s, _pad_rows(kiwi.reshape(bd, s_len, LANE), PAGE_SIZE),
                                       _pages_transposed(cache_k_idx[l]), min(PAGES_SCORES, n_pages))
    bias_past, bias_new = _dsa_sample_select(s_past.reshape(ns, past), s_new.reshape(ns, PAGE_SIZE),
                                             min(TOPK_IDX, (past + s_len) // 4))
    o_a = _dsa_sample_attn(page_table, qa_s, bias_past.reshape(bd, s_len, past),
                           bias_new.reshape(bd, s_len, PAGE_SIZE),
                           _pad_rows(ka.reshape(bd, s_len, LANE), PAGE_SIZE),
                           _pad_rows(va.reshape(bd, s_len, LANE), PAGE_SIZE),
                           _pages_transposed(cache_k_a[l]), _pages_transposed(cache_v_a[l]), min(PAGES_ATTN, n_pages))
    o_a = o_a.reshape(bd, s_len, H_A, KV_A, HEAD_DIM)
    hpg = H_A // KV_A
    o_a = jnp.concatenate([o_a[:, :, g * hpg:(g + 1) * hpg, g] for g in range(KV_A)], axis=2)
    o_a = o_a.reshape(ns, H_A * HEAD_DIM).astype(BF16)
    o_b = _moba_sample(page_table, qb_s, _pad_rows(kb.reshape(bd, s_len, WKV), PAGE_SIZE),
                       _pad_rows(vb.reshape(bd, s_len, WKV), PAGE_SIZE),
                       _pages_transposed(cache_k_b[l]), _pages_transposed(cache_v_b[l]), min(PAGES_MOBA, n_pages))
    o_b = o_b.reshape(ns, WKV).astype(BF16)
    y_sample = _ffn(o_a, o_b, ga, gb, xs, lw, g_final, ns, ns, False).reshape(bd, s_len, d)
    rows_s = (ka.reshape(1, bd, s_len, KV_A, HEAD_DIM), va.reshape(1, bd, s_len, KV_A, HEAD_DIM),
              kiwi[:, :D_IDX].reshape(1, bd, s_len, D_IDX), kb.reshape(1, bd, s_len, H_B, HEAD_DIM),
              vb.reshape(1, bd, s_len, H_B, HEAD_DIM))
    return (y_prompt, y_sample) + rows_p + rows_s
```

```python
import functools

import jax
import jax.numpy as jnp
from jax import lax
from jax.experimental import pallas as pl
from jax.experimental.pallas import tpu as pltpu

F32, BF16, I32 = jnp.float32, jnp.bfloat16, jnp.int32

HEAD_DIM = 64
ROT_DIM = HEAD_DIM // 4
ROPE_THETA = 500000.0
H_A = 8
KV_A = 2
H_IDX = 8
D_IDX = 64
TOPK_IDX = 256
H_B = 8
MOBA_BLOCK = 256
TOPK_BLK = 3
N_GROUPS = 4
EXPERTS_PER_GROUP = 8
N_EXPERTS = N_GROUPS * EXPERTS_PER_GROUP
D_EXPERT = 256
RMS_EPS = 1e-6
PAGE_SIZE = 128

LANE = 128
SUBLANE = 8
BF16_ROWS = 16
VMEM_LIMIT = 56 * 1024 * 1024

TM_PROJ = 512
DSA_Q = 256
DSA_KC = TM_PROJ
DSA_HEAD_SPLITS = 1
TM_MERGE = 512
TM_MOE = 1024
TM_ROWS = 512
ROW_DMA_UNROLL = 8
PAGES_SCORES = 128
PAGES_ATTN = 128
PAGES_MOBA = 16

NEG = -1e30
INT_MIN = -(2 ** 31)
KEY_NEG_INF = -2139095041
N_HEAD_SLABS = 8
Q_EXP = N_HEAD_SLABS * LANE
WKV = H_B * HEAD_DIM
SOFTMAX_SCALE = HEAD_DIM ** -0.5
assert SOFTMAX_SCALE == 0.125

_NT = (((1,), (1,)), ((), ()))
_HI = lax.Precision.HIGHEST


def _dot(a, b, precision=None):
    return jnp.dot(a, b, preferred_element_type=F32, precision=precision)


def _dot_nt(a, b, precision=None):
    return lax.dot_general(a, b, _NT, preferred_element_type=F32, precision=precision)


def _params(*sem):
    return pltpu.CompilerParams(dimension_semantics=sem, vmem_limit_bytes=VMEM_LIMIT)


def _rms_scale(x):
    return lax.rsqrt(jnp.mean(x * x, axis=-1, keepdims=True) + RMS_EPS)


def _rope(y, c, s1, s2):
    half = ROT_DIM // 2
    return y * c + pltpu.roll(y, LANE - half, 1) * s1 + pltpu.roll(y, half, 1) * s2


def _rope_t(y, cos_t, sin_t):
    half = ROT_DIM // 2
    parts = []
    for r in range(0, y.shape[0], HEAD_DIM):
        x1, x2 = y[r:r + half], y[r + half:r + ROT_DIM]
        parts += [x1 * cos_t - x2 * sin_t, x2 * cos_t + x1 * sin_t, y[r + ROT_DIM:r + HEAD_DIM]]
    return jnp.concatenate(parts, axis=0)


def _proj_prompt_kernel(x_ref, g_ref, wt_ref, wr_ref, cos_t_ref, sin_t_ref, c_ref, s1_ref, s2_ref,
                        qa_t_ref, qi_t_ref, qb_t_ref, ka_t_ref, va_t_ref, ki_t_ref, kb_t_ref, vb_t_ref, wi_t_ref,
                        kab_ref, kib_ref, kbb_ref, va_tc_ref, vb_tc_ref, kmean_ref, ga_ref, gb_ref):
    tm, d_model = x_ref.shape
    x = x_ref[...]
    h = ((x * _rms_scale(x)) * g_ref[...]).astype(BF16)
    cos_t, sin_t = cos_t_ref[...], sin_t_ref[...]
    c, s1, s2 = c_ref[...], s1_ref[...], s2_ref[...]

    off = [0]

    def proj_t(rows):
        y = _dot_nt(wt_ref[off[0]:off[0] + rows, :], h)
        off[0] += rows
        return y

    half_q = Q_EXP // 2
    for r0 in (0, half_q):
        qa_t_ref[0, r0:r0 + half_q, :] = _rope_t(proj_t(half_q), cos_t, sin_t).astype(BF16)
    qi_t_ref[0] = _rope_t(proj_t(H_IDX * D_IDX), cos_t, sin_t).astype(BF16)
    for r0 in (0, half_q):
        qb_t_ref[0, r0:r0 + half_q, :] = _rope_t(proj_t(half_q), cos_t, sin_t).astype(BF16)
    ka_t_ref[0] = _rope_t(proj_t(KV_A * HEAD_DIM), cos_t, sin_t)
    va_t = proj_t(KV_A * HEAD_DIM)
    va_t_ref[0] = va_t
    va_tc_ref[0] = va_t.astype(BF16)
    ki_t_ref[0] = _rope_t(proj_t(D_IDX), cos_t, sin_t)
    kb_t_ref[0] = _rope_t(proj_t(WKV), cos_t, sin_t)
    vb_t = proj_t(WKV)
    vb_t_ref[0] = vb_t
    for n in range(tm // MOBA_BLOCK):
        vb_tc_ref[n] = vb_t[:, n * MOBA_BLOCK:(n + 1) * MOBA_BLOCK].astype(BF16)
    wi_t_ref[0] = proj_t(2 * H_IDX)[:H_IDX]

    col = [0]

    def proj(width):
        y = _dot(h, wr_ref[:, col[0]:col[0] + width])
        col[0] += width
        return y

    y = proj(2 * LANE)
    kab_ref[...] = _rope(y[:, :LANE], c, s1, s2).astype(BF16)
    kib_ref[...] = _rope(y[:, LANE:], c, s1, s2)[:, :D_IDX].astype(BF16)
    y = proj(WKV)
    kb = jnp.concatenate([_rope(y[:, j * LANE:(j + 1) * LANE], c, s1, s2) for j in range(WKV // LANE)], axis=1)
    kbb_ref[...] = kb.astype(BF16)
    nblk = tm // MOBA_BLOCK
    kmean_ref[0] = jnp.sum(kb.reshape(nblk, MOBA_BLOCK, WKV), axis=1) * (1.0 / MOBA_BLOCK)
    ga_ref[...] = proj(d_model)
    gb_ref[...] = proj(d_model)


def _project_prompt(x2d, g, w_t, w_r, tabs_t, tabs, bsz, t):
    n, d = x2d.shape
    tm = TM_PROJ
    tpb = t // tm
    row = lambda i: (i, 0)
    const = lambda i: (0, 0)
    bt = lambda i: (i // tpb, 0, i % tpb)
    chunk = lambda i: (i, 0, 0)
    once = dict(pipeline_mode=pl.Buffered(1))
    in_specs = [pl.BlockSpec((tm, d), row), pl.BlockSpec((1, d), const),
                pl.BlockSpec(w_t.shape, const, **once), pl.BlockSpec(w_r.shape, const, **once)]
    in_specs += [pl.BlockSpec((ROT_DIM // 2, tm), lambda i: (0, i % tpb))] * 2
    in_specs += [pl.BlockSpec((tm, LANE), lambda i: (i % tpb, 0))] * 3
    t_feats = [(Q_EXP, BF16), (H_IDX * D_IDX, BF16), (Q_EXP, BF16), (KV_A * HEAD_DIM, F32), (KV_A * HEAD_DIM, F32),
               (D_IDX, F32), (WKV, F32), (WKV, F32), (H_IDX, F32)]
    out_shape = [jax.ShapeDtypeStruct((bsz, f, t), dt) for f, dt in t_feats]
    out_specs = [pl.BlockSpec((1, f, tm), bt) for f, _ in t_feats]
    r_feats = [(KV_A * HEAD_DIM, BF16), (D_IDX, BF16), (WKV, BF16)]
    out_shape += [jax.ShapeDtypeStruct((n, f), dt) for f, dt in r_feats]
    out_specs += [pl.BlockSpec((tm, f), row) for f, _ in r_feats]
    nblk = tm // MOBA_BLOCK
    out_shape += [jax.ShapeDtypeStruct((n // tm, KV_A * HEAD_DIM, tm), BF16),
                  jax.ShapeDtypeStruct((n // MOBA_BLOCK, WKV, MOBA_BLOCK), BF16),
                  jax.ShapeDtypeStruct((n // tm, nblk, WKV), F32),
                  jax.ShapeDtypeStruct((n, d), F32), jax.ShapeDtypeStruct((n, d), F32)]
    out_specs += [pl.BlockSpec((1, KV_A * HEAD_DIM, tm), chunk), pl.BlockSpec((nblk, WKV, MOBA_BLOCK), chunk),
                  pl.BlockSpec((1, nblk, WKV), chunk), pl.BlockSpec((tm, d), row), pl.BlockSpec((tm, d), row)]
    return pl.pallas_call(
        _proj_prompt_kernel, grid=(n // tm,), in_specs=in_specs, out_specs=out_specs, out_shape=out_shape,
        compiler_params=_params("parallel"), name="proj_prompt",
    )(x2d, g.reshape(1, d), w_t, w_r, *tabs_t, *tabs)


def _proj_sample_kernel(x_ref, g_ref, w_ref, c_ref, s1_ref, s2_ref,
                        qa_ref, qi_ref, qb_ref, ka_ref, va_ref, kiwi_ref, kb_ref, vb_ref, ga_ref, gb_ref):
    tm, d_model = x_ref.shape
    x = x_ref[...]
    h = ((x * _rms_scale(x)) * g_ref[...]).astype(BF16)
    c, s1, s2 = c_ref[...], s1_ref[...], s2_ref[...]
    off = [0]

    def proj(width):
        y = _dot(h, w_ref[:, off[0]:off[0] + width])
        off[0] += width
        return y

    def rope_slabs(y):
        return jnp.concatenate([_rope(y[:, j * LANE:(j + 1) * LANE], c, s1, s2) for j in range(y.shape[1] // LANE)],
                               axis=1)

    for q_ref in (qa_ref, qi_ref, qb_ref):
        q_ref[...] = rope_slabs(proj(Q_EXP)).astype(BF16)
    y = proj(3 * LANE)
    ka_ref[...] = _rope(y[:, :LANE], c, s1, s2)
    va_ref[...] = y[:, LANE:2 * LANE]
    is_ki = lax.broadcasted_iota(I32, (tm, LANE), 1) < D_IDX
    kiwi_ref[...] = _rope(y[:, 2 * LANE:], jnp.where(is_ki, c, 1.0), jnp.where(is_ki, s1, 0.0),
                          jnp.where(is_ki, s2, 0.0))
    kb_ref[...] = rope_slabs(proj(WKV))
    vb_ref[...] = proj(WKV)
    ga_ref[...] = proj(d_model)
    gb_ref[...] = proj(d_model)


def _project_sample(x2d, g, w_all, tabs):
    n, d = x2d.shape
    const = lambda i: (0, 0)
    widths = [(Q_EXP, BF16)] * 3 + [(LANE, F32)] * 3 + [(WKV, F32)] * 2 + [(d, F32)] * 2
    return pl.pallas_call(
        _proj_sample_kernel, grid=(1,),
        in_specs=[pl.BlockSpec((n, d), const), pl.BlockSpec((1, d), const),
                  pl.BlockSpec(w_all.shape, const, pipeline_mode=pl.Buffered(1))] + [pl.BlockSpec((n, LANE), const)] * 3,
        out_specs=[pl.BlockSpec((n, w), const) for w, _ in widths],
        out_shape=[jax.ShapeDtypeStruct((n, w), dt) for w, dt in widths],
        compiler_params=_params("arbitrary"), name="proj_sample",
    )(x2d, g.reshape(1, d), w_all, *tabs)


def _key_to_float(key):
    return pltpu.bitcast(key ^ ((key >> 31) & 0x7FFFFFFF), F32)


def _select_bias(score_ref, bias_ref, nc, k_top, n_keys_log2, key_axis):
    _, d0, d1 = score_ref.shape
    kc = (d0, d1)[key_axis]
    kidx0 = lax.broadcasted_iota(I32, (d0, d1), key_axis)
    k_top = float(k_top)
    qshape = (1, d1) if key_axis == 0 else (d0, 1)

    def fold(f):
        if key_axis == 0:
            return jnp.sum(f.reshape(SUBLANE, d0 // SUBLANE, d1), axis=0)
        part = f[:, :LANE]
        for j in range(1, d1 // LANE):
            part = part + f[:, j * LANE:(j + 1) * LANE]
        return part

    def count(pred):
        def body(c, cnt):
            return cnt + fold(jnp.where(pred(score_ref[c], c), 1.0, 0.0))
        zero = jnp.zeros((d0 // SUBLANE, d1) if key_axis == 0 else (d0, LANE), F32)
        return jnp.sum(lax.fori_loop(0, nc, body, zero), axis=key_axis, keepdims=True)

    nonneg = count(lambda s, c: s >= 0.0)
    v0 = jnp.where(nonneg >= k_top, 0, INT_MIN).astype(I32)

    def bit_body(b, carry):
        v, at_least = carry
        cand = v | jnp.left_shift(jnp.int32(1), 30 - b)
        cf = _key_to_float(cand)
        cnt = count(lambda s, c: s >= cf)
        ok = cnt >= k_top
        return jnp.where(ok, cand, v), jnp.where(ok, cnt, at_least)

    v, at_least = lax.fori_loop(0, 31, bit_body, (v0, nonneg))
    thr = _key_to_float(jnp.maximum(v, KEY_NEG_INF))

    tied = (at_least > k_top) & (thr > -jnp.inf)
    all_keys = jnp.full(qshape, 2 ** n_keys_log2 - 1, I32)

    def tie_search():
        need = k_top - count(lambda s, c: s > thr)

        def tie_body(b, u):
            cand = u | jnp.left_shift(jnp.int32(1), n_keys_log2 - 1 - b)
            below = count(lambda s, c: (s == thr) & (c * kc + kidx0 < cand))
            return jnp.where(below < need, cand, u)

        return lax.fori_loop(0, n_keys_log2, tie_body, jnp.zeros(qshape, I32))

    u = lax.cond(jnp.max(jnp.where(tied, 1.0, 0.0)) > 0.0, tie_search, lambda: all_keys)

    def bias_body(c, carry):
        s = score_ref[c]
        sel = (s > -jnp.inf) & ((s > thr) | ((s == thr) & (c * kc + kidx0 <= u)))
        bias_ref[c] = jnp.where(sel, 0.0, NEG)
        return carry

    lax.fori_loop(0, nc, bias_body, 0)


def _dsa_prompt_kernel(qa_t_ref, qi_t_ref, wi_t_ref, kib_ref, kab_ref, va_tc_ref, o_ref, score_ref, bias_ref, *, k_top):
    nchunks, kc, q = score_ref.shape
    i = pl.program_id(1)
    nc = (i * q + q - 1) // kc + 1
    key0 = lax.broadcasted_iota(I32, (kc, q), 0)
    qpos = i * q + lax.broadcasted_iota(I32, (kc, q), 1)
    w = wi_t_ref[0] * (H_IDX ** -0.5 * D_IDX ** -0.5)
    qi_all = jnp.concatenate([qi_t_ref[0, h * D_IDX:(h + 1) * D_IDX, :] for h in range(H_IDX)], axis=1)

    def score_body(c, carry):
        d = _dot(kib_ref[pl.ds(pl.multiple_of(c * kc, kc), kc), :], qi_all)
        acc = jnp.zeros((kc, q), F32)
        for h in range(H_IDX):
            acc = acc + jnp.maximum(d[:, h * q:(h + 1) * q], 0.0) * w[h:h + 1, :]
        score_ref[c] = jnp.where(c * kc + key0 <= qpos, acc, -jnp.inf)
        return carry

    lax.fori_loop(0, nc, score_body, 0)
    _select_bias(score_ref, bias_ref, nc, k_top, (nchunks * kc - 1).bit_length(), key_axis=0)

    qa_all = jnp.concatenate([qa_t_ref[0, h * LANE:(h + 1) * LANE, :] for h in range(H_A)], axis=1)
    qa_all = qa_all * SOFTMAX_SCALE

    hs = H_A // DSA_HEAD_SPLITS
    q_parts = [qa_all[:, g * hs * q:(g + 1) * hs * q] for g in range(DSA_HEAD_SPLITS)]

    dv = KV_A * HEAD_DIM
    ones_rows = jnp.ones((BF16_ROWS, kc), BF16)

    def att_body(c, carry):
        k = kab_ref[pl.ds(pl.multiple_of(c * kc, kc), kc), :]
        v = jnp.concatenate([va_tc_ref[c], ones_rows], axis=0)
        b = bias_ref[c]
        splits = range(DSA_HEAD_SPLITS)
        qk = [_dot(k, q_parts[g]) for g in splits]
        s = [jnp.concatenate([qk[g][:, h * q:(h + 1) * q] + b for h in range(hs)], axis=1) for g in splits]
        m_new = [jnp.maximum(carry[g][0], jnp.max(s[g], axis=0, keepdims=True)) for g in splits]
        p = [jnp.exp((s[g] - m_new[g]).astype(BF16)) for g in splits]
        pv = [_dot(v, p[g]) for g in splits]
        out = []
        for g in splits:
            m, l, acc = carry[g]
            alpha = jnp.exp(m - m_new[g])
            out.append((m_new[g], alpha * l + pv[g][dv:dv + 1], alpha * acc + pv[g][:dv]))
        return tuple(out)

    init = tuple((jnp.full((1, hs * q), NEG, F32), jnp.zeros((1, hs * q), F32),
                  jnp.zeros((KV_A * HEAD_DIM, hs * q), F32)) for _ in range(DSA_HEAD_SPLITS))
    final = lax.fori_loop(0, nc, att_body, init)
    o = jnp.concatenate([acc / l for _, l, acc in final], axis=1)
    heads_per_group = H_A // KV_A
    for j in range(H_A // 2):
        g0 = ((2 * j) // heads_per_group) * HEAD_DIM
        pair = jnp.concatenate([o[g0:g0 + HEAD_DIM, (2 * j) * q:(2 * j + 1) * q],
                                o[g0:g0 + HEAD_DIM, (2 * j + 1) * q:(2 * j + 2) * q]], axis=0)
        o_ref[:, j * LANE:(j + 1) * LANE] = pair.T.astype(BF16)


def _dsa_prompt(qa_t, qi_t, wi_t, kib, kab, va_tc, bsz, t):
    q, kc = DSA_Q, DSA_KC
    nq, nchunks = t // q, t // kc
    tile_t = lambda b, i: (b, 0, i)
    return pl.pallas_call(
        functools.partial(_dsa_prompt_kernel, k_top=min(TOPK_IDX, t // 4)),
        grid=(bsz, nq),
        in_specs=[pl.BlockSpec((1, Q_EXP, q), tile_t), pl.BlockSpec((1, H_IDX * D_IDX, q), tile_t),
                  pl.BlockSpec((1, H_IDX, q), tile_t),
                  pl.BlockSpec((t, D_IDX), lambda b, i: (b, 0)), pl.BlockSpec((t, KV_A * HEAD_DIM), lambda b, i: (b, 0)),
                  pl.BlockSpec((nchunks, KV_A * HEAD_DIM, kc), lambda b, i: (b, 0, 0))],
        out_specs=pl.BlockSpec((q, H_A * HEAD_DIM), lambda b, i: (b * nq + i, 0)),
        out_shape=jax.ShapeDtypeStruct((bsz * t, H_A * HEAD_DIM), BF16),
        scratch_shapes=[pltpu.VMEM((nchunks, kc, q), F32), pltpu.VMEM((nchunks, kc, q), F32)],
        compiler_params=_params("parallel", "arbitrary"), name="dsa_prompt",
    )(qa_t, qi_t, wi_t, kib, kab, va_tc)


def _top_blocks(gate, valid, idxf, n_sel, axis):
    gate = jnp.where(valid, gate, -jnp.inf)
    sel = jnp.zeros(gate.shape, jnp.bool_)
    for _ in range(n_sel):
        mx = jnp.max(gate, axis=axis, keepdims=True)
        first = jnp.min(jnp.where(gate == mx, idxf, 1e9), axis=axis, keepdims=True)
        pick = idxf == first
        sel = sel | (pick & valid)
        gate = jnp.where(pick, -jnp.inf, gate)
    return sel


def _moba_prompt_kernel(qb_t_ref, km_ref, kb_ref, vb_tc_ref, o_ref, *, n_sel):
    qb = o_ref.shape[0]
    nb = km_ref.shape[0]
    j = pl.program_id(1)
    r = 2 * qb
    nrow = lax.broadcasted_iota(I32, (nb, r), 0)
    nrowf = nrow.astype(F32)
    qc = lax.broadcasted_iota(I32, (qb, r), 1)
    own_mask = lax.broadcasted_iota(I32, (qb, r), 0) <= jnp.where(qc >= qb, qc - qb, qc)
    own = pl.multiple_of(j * qb, qb)

    pairs = range(H_B // 2)
    sls = [slice(p * LANE, (p + 1) * LANE) for p in pairs]
    ones_rows = jnp.ones((BF16_ROWS, MOBA_BLOCK), BF16)

    def pv_and_sum(n, p, pr):
        return _dot(jnp.concatenate([vb_tc_ref[n, sls[p], :], ones_rows], axis=0), pr)

    q_raw = [jnp.concatenate([qb_t_ref[0, (2 * p) * LANE:(2 * p + 1) * LANE, :],
                              qb_t_ref[0, (2 * p + 1) * LANE:(2 * p + 2) * LANE, :]], axis=1) for p in pairs]
    gates = [_dot(km_ref[:, sls[p]], q_raw[p].astype(F32), precision=_HI) for p in pairs]
    selbiases = [jnp.where(_top_blocks(gates[p], nrow < j, nrowf, n_sel, axis=0), 0.0, NEG) for p in pairs]
    qes = [q * SOFTMAX_SCALE for q in q_raw]
    s_own = [jnp.where(own_mask, _dot(kb_ref[pl.ds(own, qb), sls[p]], qes[p]), NEG) for p in pairs]
    m_own = [jnp.max(s, axis=0, keepdims=True) for s in s_own]
    pv_own = [pv_and_sum(j, p, jnp.exp((s_own[p] - m_own[p]).astype(BF16))) for p in pairs]
    init = [(m_own[p], pv_own[p][LANE:LANE + 1], pv_own[p][:LANE]) for p in pairs]

    half = lax.broadcasted_iota(I32, (2 * MOBA_BLOCK, r), 0) < MOBA_BLOCK

    def body(n2, carry):
        start = pl.multiple_of(n2 * (2 * MOBA_BLOCK), 2 * MOBA_BLOCK)
        qk = [_dot(kb_ref[pl.ds(start, 2 * MOBA_BLOCK), sls[p]], qes[p]) for p in pairs]
        rowb = [[jnp.sum(jnp.where(nrow == 2 * n2 + i, selbiases[p], 0.0), axis=0, keepdims=True) for i in (0, 1)]
                for p in pairs]
        s = [qk[p] + jnp.where(half, rowb[p][0], rowb[p][1]) for p in pairs]
        m_new = [jnp.maximum(carry[p][0], jnp.max(s[p], axis=0, keepdims=True)) for p in pairs]
        pr = [jnp.exp((s[p] - m_new[p]).astype(BF16)) for p in pairs]
        pv = [pv_and_sum(2 * n2, p, pr[p][:MOBA_BLOCK]) + pv_and_sum(2 * n2 + 1, p, pr[p][MOBA_BLOCK:]) for p in pairs]
        out = []
        for p in pairs:
            m, l, acc = carry[p]
            alpha = jnp.exp(m - m_new[p])
            out.append((m_new[p], alpha * l + pv[p][LANE:LANE + 1], alpha * acc + pv[p][:LANE]))
        return tuple(out)

    final = lax.fori_loop(0, (j + 1) // 2, body, tuple(init))
    for p in pairs:
        _, l, acc = final[p]
        o = acc / l
        pair = jnp.concatenate([o[:HEAD_DIM, :qb], o[HEAD_DIM:, qb:]], axis=0)
        o_ref[:, sls[p]] = pair.T.astype(BF16)


def _moba_prompt(qb_t, kmean, kbb, vb_tc, bsz, t):
    nb = t // MOBA_BLOCK
    return pl.pallas_call(
        functools.partial(_moba_prompt_kernel, n_sel=min(TOPK_BLK, nb - 1)),
        grid=(bsz, nb),
        in_specs=[pl.BlockSpec((1, Q_EXP, MOBA_BLOCK), lambda b, i: (b, 0, i)), pl.BlockSpec((nb, WKV), lambda b, i: (b, 0)),
                  pl.BlockSpec((t, WKV), lambda b, i: (b, 0)),
                  pl.BlockSpec((nb, WKV, MOBA_BLOCK), lambda b, i: (b, 0, 0))],
        out_specs=pl.BlockSpec((MOBA_BLOCK, WKV), lambda b, i: (b * nb + i, 0)),
        out_shape=jax.ShapeDtypeStruct((bsz * t, WKV), BF16),
        compiler_params=_params("parallel", "arbitrary"), name="moba_prompt",
    )(qb_t, kmean, kbb, vb_tc)


GSEL_LANE = LANE - 1


def _route(rl, group=None):
    lanef = lax.broadcasted_iota(I32, rl.shape, 1).astype(F32)

    def masked_softmax(mask):
        z = jnp.where(mask, rl, -jnp.inf)
        e = jnp.exp(z - jnp.max(z, axis=1, keepdims=True))
        return e / jnp.sum(e, axis=1, keepdims=True)

    def first_lane(cond):
        return jnp.min(jnp.where(cond, lanef, 1e9), axis=1, keepdims=True)

    gmask = lanef < N_GROUPS
    gp = masked_softmax(gmask)
    if group is None:
        gprob = jnp.max(gp, axis=1, keepdims=True)
        gsel = first_lane((gp == gprob) & gmask)
    else:
        gsel = jnp.full((rl.shape[0], 1), group, I32).astype(F32)
        gprob = jnp.sum(jnp.where(lanef == gsel, gp, 0.0), axis=1, keepdims=True)
    lo = N_GROUPS + EXPERTS_PER_GROUP * gsel
    emask = (lanef >= lo) & (lanef < lo + EXPERTS_PER_GROUP)
    ep = jnp.where(emask, masked_softmax(emask), -1.0)
    p1 = jnp.max(ep, axis=1, keepdims=True)
    i1 = first_lane(ep == p1)
    ep = jnp.where(lanef == i1, -1.0, ep)
    p2 = jnp.max(ep, axis=1, keepdims=True)
    i2 = first_lane(ep == p2)
    den = p1 + p2
    cw = jnp.where(lanef == i1, gprob * p1 / den, 0.0) + jnp.where(lanef == i2, gprob * p2 / den, 0.0)
    return jnp.where(lanef == GSEL_LANE, gsel, cw)


def _merge_kernel(oa_ref, ob_ref, ga_ref, gb_ref, x_ref, wa_ref, wb_ref, wo_ref, gf_ref, wr_ref, br_ref,
                  x1_ref, h2_ref, cw_ref):
    tm = x_ref.shape[0]
    u = jax.nn.sigmoid(ga_ref[...]) * _dot(oa_ref[...], wa_ref[...]) \
        + jax.nn.sigmoid(gb_ref[...]) * _dot(ob_ref[...], wb_ref[...])
    x1 = x_ref[...] + _dot(u.astype(BF16), wo_ref[...])
    x1_ref[...] = x1
    h2 = (x1 * _rms_scale(x1)) * gf_ref[...]
    h2_ref[...] = h2.astype(BF16)

    cw_ref[...] = _route(_dot(h2, wr_ref[...], precision=_HI) + br_ref[...])


def _merge(oa, ob, ga, gb, x2d, wa, wb, wo, g_ffn, w_router, b_router, tm):
    n, d = x2d.shape
    row = lambda i: (i, 0)
    const = lambda i: (0, 0)
    ha = oa.shape[1]
    return pl.pallas_call(
        _merge_kernel, grid=(n // tm,),
        in_specs=[pl.BlockSpec((tm, ha), row), pl.BlockSpec((tm, ha), row), pl.BlockSpec((tm, d), row),
                  pl.BlockSpec((tm, d), row), pl.BlockSpec((tm, d), row), pl.BlockSpec((ha, d), const),
                  pl.BlockSpec((ha, d), const), pl.BlockSpec((d, d), const), pl.BlockSpec((1, d), const),
                  pl.BlockSpec((d, LANE), const), pl.BlockSpec((1, LANE), const)],
        out_specs=[pl.BlockSpec((tm, d), row), pl.BlockSpec((tm, d), row), pl.BlockSpec((tm, LANE), row)],
        out_shape=[jax.ShapeDtypeStruct((n, d), F32), jax.ShapeDtypeStruct((n, d), BF16),
                   jax.ShapeDtypeStruct((n, LANE), F32)],
        compiler_params=_params("parallel"), name="merge",
    )(oa, ob, ga, gb, x2d, wa, wb, wo, g_ffn.reshape(1, d), w_router, b_router)


def _moe_kernel(h_ref, cw_ref, x_ref, wg_ref, wu_ref, wd_ref, gfin_ref, y_ref, acc_ref):
    tm = h_ref.shape[0]
    e = pl.program_id(1)

    @pl.when(e == 0)
    def _():
        acc_ref[...] = jnp.zeros_like(acc_ref)

    h = h_ref[...]
    a = _dot(h, wg_ref[0])
    hdn = (a * jax.nn.sigmoid(a)) * _dot(h, wu_ref[0])
    y = _dot(hdn.astype(BF16), wd_ref[0])
    lane = lax.broadcasted_iota(I32, (tm, LANE), 1)
    col = jnp.sum(jnp.where(lane == e + N_GROUPS, cw_ref[...], 0.0), axis=1, keepdims=True)
    acc_ref[...] += y * col

    @pl.when(e == pl.num_programs(1) - 1)
    def _():
        out = x_ref[...] + acc_ref[...]
        y_ref[...] = (out * _rms_scale(out)) * gfin_ref[...]


def _moe(h2, cw, x1, wg, wu, wd, g_final, tm):
    n, d = x1.shape
    ne, _, de = wg.shape
    row = lambda i, e: (i, 0)
    return pl.pallas_call(
        _moe_kernel, grid=(n // tm, ne),
        in_specs=[pl.BlockSpec((tm, d), row), pl.BlockSpec((tm, LANE), row), pl.BlockSpec((tm, d), row),
                  pl.BlockSpec((1, d, de), lambda i, e: (e, 0, 0)), pl.BlockSpec((1, d, de), lambda i, e: (e, 0, 0)),
                  pl.BlockSpec((1, de, d), lambda i, e: (e, 0, 0)), pl.BlockSpec((1, d), lambda i, e: (0, 0))],
        out_specs=pl.BlockSpec((tm, d), row),
        out_shape=jax.ShapeDtypeStruct((n, d), F32),
        scratch_shapes=[pltpu.VMEM((tm, d), F32)],
        compiler_params=_params("parallel", "arbitrary"), name="moe",
    )(h2, cw, x1, wg, wu, wd, g_final.reshape(1, d))


def _scatter_rows_kernel(dest_ref, x_ref, zeros_ref, xs_ref, sem):
    del zeros_ref
    tm = x_ref.shape[0]
    base = pl.program_id(0) * tm

    def row_copy(r, dst):
        return pltpu.make_async_copy(x_ref.at[pl.ds(r, 1)], xs_ref.at[pl.ds(dst, 1)], sem)

    def start(r, carry):
        row_copy(r, dest_ref[base + r]).start()
        return carry

    def wait(r, carry):
        row_copy(r, 0).wait()
        return carry

    lax.fori_loop(0, tm, start, 0, unroll=ROW_DMA_UNROLL)
    lax.fori_loop(0, tm, wait, 0, unroll=ROW_DMA_UNROLL)


def _scatter_rows(dest, x, n_rows, tm):
    n, d = x.shape
    grid_spec = pltpu.PrefetchScalarGridSpec(
        num_scalar_prefetch=1, grid=(n // tm,),
        in_specs=[pl.BlockSpec((tm, d), lambda i, dest: (i, 0)), pl.BlockSpec(memory_space=pl.ANY)],
        out_specs=pl.BlockSpec(memory_space=pl.ANY),
        scratch_shapes=[pltpu.SemaphoreType.DMA(())])
    return pl.pallas_call(
        _scatter_rows_kernel, grid_spec=grid_spec, out_shape=jax.ShapeDtypeStruct((n_rows, d), x.dtype),
        input_output_aliases={2: 0}, compiler_params=_params("arbitrary"), name="moe_scatter_rows",
    )(dest, x, jnp.zeros((n_rows, d), x.dtype))


def _gather_rows_kernel(dest_ref, ys_ref, y_ref, sem):
    tm = y_ref.shape[0]
    base = pl.program_id(0) * tm

    def row_copy(r, src):
        return pltpu.make_async_copy(ys_ref.at[pl.ds(src, 1)], y_ref.at[pl.ds(r, 1)], sem)

    def start(r, carry):
        row_copy(r, dest_ref[base + r]).start()
        return carry

    def wait(r, carry):
        row_copy(r, 0).wait()
        return carry

    lax.fori_loop(0, tm, start, 0, unroll=ROW_DMA_UNROLL)
    lax.fori_loop(0, tm, wait, 0, unroll=ROW_DMA_UNROLL)


def _gather_rows(dest, ys, n, tm):
    d = ys.shape[1]
    grid_spec = pltpu.PrefetchScalarGridSpec(
        num_scalar_prefetch=1, grid=(n // tm,),
        in_specs=[pl.BlockSpec(memory_space=pl.ANY)],
        out_specs=pl.BlockSpec((tm, d), lambda i, dest: (i, 0)),
        scratch_shapes=[pltpu.SemaphoreType.DMA(())])
    return pl.pallas_call(
        _gather_rows_kernel, grid_spec=grid_spec, out_shape=jax.ShapeDtypeStruct((n, d), ys.dtype),
        compiler_params=_params("arbitrary"), name="moe_gather_rows",
    )(dest, ys)


def _moe_grouped_kernel(tg_ref, used_ref, x_ref, gf_ref, wr_ref, br_ref, wg_ref, wu_ref, wd_ref, gfin_ref, y_ref,
                        h_ref, cw_ref, acc_ref):
    tm = x_ref.shape[0]
    i, e = pl.program_id(0), pl.program_id(1)
    group = tg_ref[i]
    live = i < used_ref[0]
    last = e == pl.num_programs(1) - 1

    @pl.when(live & (e == 0))
    def _():
        x1 = x_ref[...]
        h2 = (x1 * _rms_scale(x1)) * gf_ref[...]
        h_ref[...] = h2.astype(BF16)
        cw_ref[...] = _route(_dot(h2, wr_ref[...], precision=_HI) + br_ref[...], group)
        acc_ref[...] = jnp.zeros_like(acc_ref)

    @pl.when(live)
    def _():
        h = h_ref[...]
        a = _dot(h, wg_ref[0])
        hdn = (a * jax.nn.sigmoid(a)) * _dot(h, wu_ref[0])
        y = _dot(hdn.astype(BF16), wd_ref[0])
        lane = lax.broadcasted_iota(I32, (tm, LANE), 1)
        expert_lane = N_GROUPS + group * EXPERTS_PER_GROUP + e
        col = jnp.sum(jnp.where(lane == expert_lane, cw_ref[...], 0.0), axis=1, keepdims=True)
        acc_ref[...] += y * col

    @pl.when(live & last)
    def _():
        out = x_ref[...] + acc_ref[...]
        y_ref[...] = (out * _rms_scale(out)) * gfin_ref[...]

    @pl.when(jnp.logical_not(live) & last)
    def _():
        y_ref[...] = jnp.zeros_like(y_ref)


def _moe_grouped(tile_group, tiles_used, xs, g_ffn, w_router, b_router, wg, wu, wd, g_final, tm):
    p, d = xs.shape
    de = wg.shape[2]
    row = lambda i, e, tg, used: (i, 0)
    const = lambda i, e, tg, used: (0, 0)
    expert = lambda i, e, tg, used: (tg[i] * EXPERTS_PER_GROUP + e, 0, 0)
    grid_spec = pltpu.PrefetchScalarGridSpec(
        num_scalar_prefetch=2, grid=(p // tm, EXPERTS_PER_GROUP),
        in_specs=[pl.BlockSpec((tm, d), row), pl.BlockSpec((1, d), const), pl.BlockSpec((d, LANE), const),
                  pl.BlockSpec((1, LANE), const), pl.BlockSpec((1, d, de), expert), pl.BlockSpec((1, d, de), expert),
                  pl.BlockSpec((1, de, d), expert), pl.BlockSpec((1, d), const)],
        out_specs=pl.BlockSpec((tm, d), row),
        scratch_shapes=[pltpu.VMEM((tm, d), BF16), pltpu.VMEM((tm, LANE), F32), pltpu.VMEM((tm, d), F32)])
    return pl.pallas_call(
        _moe_grouped_kernel, grid_spec=grid_spec, out_shape=jax.ShapeDtypeStruct((p, d), F32),
        compiler_params=_params("parallel", "arbitrary"), name="moe_grouped",
    )(tile_group, tiles_used, xs, g_ffn.reshape(1, d), w_router, b_router, wg, wu, wd, g_final.reshape(1, d))


def _group_dispatch(group, tm):
    n = group.shape[0]
    onehot = (group[:, None] == jnp.arange(N_GROUPS, dtype=I32)[None, :]).astype(I32)
    rank = jnp.cumsum(onehot, axis=0) - onehot
    counts = jnp.sum(onehot, axis=0)
    padded = (counts + tm - 1) // tm * tm
    ends = jnp.cumsum(padded)
    dest = jnp.sum(onehot * ((ends - padded)[None, :] + rank), axis=1).astype(I32)
    rows = -(-n // tm) * tm + N_GROUPS * tm
    tiles = rows // tm
    tile_start = jnp.arange(tiles, dtype=I32) * tm
    tile_group = jnp.minimum(jnp.sum((tile_start[:, None] >= ends[None, :]).astype(I32), axis=1), N_GROUPS - 1)
    return dest, tile_group, (ends[-1:] // tm).astype(I32), rows


def _page_specs(shape_tail, pages_per_step):
    zeros = (0,) * len(shape_tail)
    return [pl.BlockSpec((None,) + shape_tail,
                         functools.partial(lambda b, j, pt, t: (pt[b, j * pages_per_step + t],) + zeros, t=t))
            for t in range(pages_per_step)]


def _stream_pages(pt_ref, pools_and_bufs, sem_ref, pages):
    b, j = pl.program_id(0), pl.program_id(1)
    nj = pl.num_programs(1)
    total = pl.num_programs(0) * nj
    step = b * nj + j
    slot = step % 2

    def copy(pool_ref, buf_ref, page, sl, t):
        return pltpu.make_async_copy(pool_ref.at[page], buf_ref.at[sl, t], sem_ref.at[sl])

    def start(bb, jj, sl):
        for t in range(pages):
            page = pt_ref[bb, jj * pages + t]
            for pool_ref, buf_ref in pools_and_bufs:
                copy(pool_ref, buf_ref, page, sl, t).start()

    @pl.when(step == 0)
    def _():
        start(b, j, slot)

    @pl.when(step + 1 < total)
    def _():
        nxt = step + 1
        start(nxt // nj, nxt % nj, 1 - slot)

    for t in range(pages):
        for pool_ref, buf_ref in pools_and_bufs:
            copy(pool_ref, buf_ref, 0, slot, t).wait()
    return slot


def _dsa_sample_score_kernel(pt_ref, qi_ref, w_ref, kin_ref, pool_ref, s_ref, snew_ref, kbuf_ref, sem_ref, *, pages):
    slot = _stream_pages(pt_ref, [(pool_ref, kbuf_ref)], sem_ref, pages)
    j = pl.program_id(1)
    rows = qi_ref.shape[1]
    s_len = rows // H_IDX
    q = qi_ref[0][:, :D_IDX]
    w = w_ref[0] * (H_IDX ** -0.5 * D_IDX ** -0.5)

    def scores(d):
        d = jnp.maximum(d, 0.0) * w
        return jnp.sum(d.reshape(s_len, H_IDX, d.shape[1]), axis=1)

    k_t = jnp.concatenate([kbuf_ref[slot, t].astype(BF16) for t in range(pages)], axis=1)
    s_ref[0] = scores(_dot(q, k_t))

    @pl.when(j == pl.num_programs(1) - 1)
    def _():
        sn = scores(_dot_nt(q, kin_ref[0][:, :D_IDX].astype(BF16)))
        si = lax.broadcasted_iota(I32, (s_len, PAGE_SIZE), 0)
        ki = lax.broadcasted_iota(I32, (s_len, PAGE_SIZE), 1)
        snew_ref[0] = jnp.where(ki <= si, sn, -jnp.inf)


def _dsa_sample_scores(page_table, qi_s, wi_s, kin_pad, pool_kidx, pages):
    bd, n_pages = page_table.shape
    rows = qi_s.shape[1]
    s_len = rows // H_IDX
    per_b = lambda b, j, pt: (b, 0, 0)
    grid_spec = pltpu.PrefetchScalarGridSpec(
        num_scalar_prefetch=1, grid=(bd, n_pages // pages),
        in_specs=[pl.BlockSpec((1, rows, LANE), per_b), pl.BlockSpec((1, rows, 1), per_b),
                  pl.BlockSpec((1, PAGE_SIZE, LANE), per_b), pl.BlockSpec(memory_space=pl.ANY)],
        out_specs=[pl.BlockSpec((1, s_len, pages * PAGE_SIZE), lambda b, j, pt: (b, 0, j)),
                   pl.BlockSpec((1, s_len, PAGE_SIZE), per_b)],
        scratch_shapes=[pltpu.VMEM((2, pages, D_IDX, PAGE_SIZE), F32), pltpu.SemaphoreType.DMA((2,))])
    return pl.pallas_call(
        functools.partial(_dsa_sample_score_kernel, pages=pages), grid_spec=grid_spec,
        out_shape=[jax.ShapeDtypeStruct((bd, s_len, n_pages * PAGE_SIZE), F32),
                   jax.ShapeDtypeStruct((bd, s_len, PAGE_SIZE), F32)],
        compiler_params=_params("arbitrary", "arbitrary"), name="dsa_sample_scores",
    )(page_table, qi_s, wi_s, kin_pad, pool_kidx)


def _dsa_sample_select_kernel(sp_ref, sn_ref, bp_ref, bn_ref, score_ref, bias_ref, *, k_top):
    nch, q, kc = score_ref.shape
    npast = nch - 1

    def load(c, carry):
        score_ref[c] = sp_ref[:, pl.ds(pl.multiple_of(c * kc, kc), kc)]
        return carry

    lax.fori_loop(0, npast, load, 0)
    score_ref[npast] = jnp.concatenate([sn_ref[...], jnp.full((q, kc - LANE), -jnp.inf, F32)], axis=1)
    _select_bias(score_ref, bias_ref, nch, k_top, (nch * kc - 1).bit_length(), key_axis=1)

    def store(c, carry):
        bp_ref[:, pl.ds(pl.multiple_of(c * kc, kc), kc)] = bias_ref[c]
        return carry

    lax.fori_loop(0, npast, store, 0)
    bn_ref[...] = bias_ref[npast][:, :LANE]


def _dsa_sample_select(s_past, s_new, k_top):
    q, past = s_past.shape
    kc = 512
    nch = past // kc + 1
    return pl.pallas_call(
        functools.partial(_dsa_sample_select_kernel, k_top=k_top),
        out_shape=[jax.ShapeDtypeStruct((q, past), F32), jax.ShapeDtypeStruct((q, LANE), F32)],
        scratch_shapes=[pltpu.VMEM((nch, q, kc), F32), pltpu.VMEM((nch, q, kc), F32)],
        compiler_params=pltpu.CompilerParams(vmem_limit_bytes=VMEM_LIMIT), name="dsa_sample_select",
    )(s_past, s_new)


def _dsa_sample_attn_kernel(pt_ref, qa_ref, bp_ref, bn_ref, kn_ref, vn_ref, kpool_ref, vpool_ref, o_ref,
                            m_ref, l_ref, acc_ref, kbuf_ref, vbuf_ref, sem_ref, *, pages):
    slot = _stream_pages(pt_ref, [(kpool_ref, kbuf_ref), (vpool_ref, vbuf_ref)], sem_ref, pages)
    j = pl.program_id(1)
    rows = qa_ref.shape[1]
    s_len = rows // H_A
    q = qa_ref[0]
    scale = HEAD_DIM ** -0.5

    @pl.when(j == 0)
    def _():
        m_ref[...] = jnp.full_like(m_ref, NEG)
        l_ref[...] = jnp.zeros_like(l_ref)
        acc_ref[...] = jnp.zeros_like(acc_ref)

    def update(s, bias, v, v_dot):
        n = s.shape[1]
        s = (s.reshape(s_len, H_A, n) * scale + bias[:, None, :]).reshape(rows, n)
        m = m_ref[...]
        m_new = jnp.maximum(m, jnp.max(s, axis=1, keepdims=True))
        alpha = jnp.exp(m - m_new)
        p = jnp.exp(s - m_new)
        l_ref[...] = alpha * l_ref[...] + jnp.sum(p, axis=1, keepdims=True)
        acc_ref[...] = alpha * acc_ref[...] + v_dot(p.astype(BF16), v)
        m_ref[...] = m_new

    k_t = jnp.concatenate([kbuf_ref[slot, t].astype(BF16) for t in range(pages)], axis=1)
    v_t = jnp.concatenate([vbuf_ref[slot, t].astype(BF16) for t in range(pages)], axis=1)
    update(_dot(q, k_t), bp_ref[0], v_t, _dot_nt)

    @pl.when(j == pl.num_programs(1) - 1)
    def _():
        update(_dot_nt(q, kn_ref[0].astype(BF16)), bn_ref[0], vn_ref[0].astype(BF16), _dot)
        o_ref[0] = acc_ref[...] / l_ref[...]


def _dsa_sample_attn(page_table, qa_s, bias_past, bias_new, kn_pad, vn_pad, pool_k, pool_v, pages):
    bd, n_pages = page_table.shape
    rows = qa_s.shape[1]
    s_len = rows // H_A
    per_b = lambda b, j, pt: (b, 0, 0)
    grid_spec = pltpu.PrefetchScalarGridSpec(
        num_scalar_prefetch=1, grid=(bd, n_pages // pages),
        in_specs=[pl.BlockSpec((1, rows, LANE), per_b),
                  pl.BlockSpec((1, s_len, pages * PAGE_SIZE), lambda b, j, pt: (b, 0, j)),
                  pl.BlockSpec((1, s_len, LANE), per_b),
                  pl.BlockSpec((1, PAGE_SIZE, LANE), per_b), pl.BlockSpec((1, PAGE_SIZE, LANE), per_b),
                  pl.BlockSpec(memory_space=pl.ANY), pl.BlockSpec(memory_space=pl.ANY)],
        out_specs=pl.BlockSpec((1, rows, LANE), per_b),
        scratch_shapes=[pltpu.VMEM((rows, 1), F32), pltpu.VMEM((rows, 1), F32), pltpu.VMEM((rows, LANE), F32),
                        pltpu.VMEM((2, pages, KV_A * HEAD_DIM, PAGE_SIZE), F32),
                        pltpu.VMEM((2, pages, KV_A * HEAD_DIM, PAGE_SIZE), F32), pltpu.SemaphoreType.DMA((2,))])
    return pl.pallas_call(
        functools.partial(_dsa_sample_attn_kernel, pages=pages), grid_spec=grid_spec,
        out_shape=jax.ShapeDtypeStruct((bd, rows, LANE), F32),
        compiler_params=_params("arbitrary", "arbitrary"), name="dsa_sample_attn",
    )(page_table, qa_s, bias_past, bias_new, kn_pad, vn_pad, pool_k, pool_v)


def _moba_sample_kernel(pt_ref, qb_ref, kn_ref, vn_ref, *refs, pages, n_sel):
    k_refs, v_refs = refs[:pages], refs[pages:2 * pages]
    o_ref, gate_ref, m_ref, l_ref, acc_ref = refs[2 * pages:]
    del pt_ref
    j = pl.program_id(1)
    rows = qb_ref.shape[1]
    s_len = rows // H_B
    nbp = acc_ref.shape[0]
    ppb = MOBA_BLOCK // PAGE_SIZE
    scale = HEAD_DIM ** -0.5

    qs = qb_ref[0]
    pair = (lax.broadcasted_iota(I32, (rows, LANE), 0) % H_B) // 2
    qe = jnp.concatenate([jnp.where(pair == p, qs, jnp.zeros_like(qs)) for p in range(H_B // 2)], axis=1)
    lane = lax.broadcasted_iota(I32, (rows, LANE), 1)

    @pl.when(j == 0)
    def _():
        gate_ref[...] = jnp.full_like(gate_ref, -jnp.inf)
        m_ref[...] = jnp.full_like(m_ref, NEG)
        l_ref[...] = jnp.zeros_like(l_ref)

    nblk = pages // ppb
    blocks = [slice(t * MOBA_BLOCK, (t + 1) * MOBA_BLOCK) for t in range(nblk)]
    k_t = jnp.concatenate([k_refs[t][...].astype(BF16) for t in range(pages)], axis=1)
    qk = _dot(qe, k_t)
    s = qk * scale
    gate_t = [jnp.sum(qk[:, blk], axis=1, keepdims=True) * (1.0 / MOBA_BLOCK) for blk in blocks]
    m_t = [jnp.max(s[:, blk], axis=1, keepdims=True) for blk in blocks]
    p = jnp.exp(s - jnp.concatenate([jnp.broadcast_to(m, (rows, MOBA_BLOCK)) for m in m_t], axis=1))
    l_t = [jnp.sum(p[:, blk], axis=1, keepdims=True) for blk in blocks]
    pb = p.astype(BF16)
    gate, m_blk, l_blk = gate_ref[...], m_ref[...], l_ref[...]
    for t in range(nblk):
        n = j * nblk + t
        v_t = jnp.concatenate([v_refs[ppb * t + i][...].astype(BF16) for i in range(ppb)], axis=1)
        acc_ref[n] = _dot_nt(pb[:, blocks[t]], v_t)
        here = lane == n
        gate = jnp.where(here, gate_t[t], gate)
        m_blk = jnp.where(here, m_t[t], m_blk)
        l_blk = jnp.where(here, l_t[t], l_blk)
    gate_ref[...] = gate
    m_ref[...] = m_blk
    l_ref[...] = l_blk

    @pl.when(j == pl.num_programs(1) - 1)
    def _():
        sel = _top_blocks(gate_ref[...], lane < nbp, lane.astype(F32), n_sel, axis=1)
        s_own = _dot_nt(qe, kn_ref[0].astype(BF16)) * scale
        srow = lax.broadcasted_iota(I32, (rows, LANE), 0) // H_B
        s_own = jnp.where(lane <= srow, s_own, NEG)
        m_blk = jnp.where(sel, m_ref[...], NEG)
        m_all = jnp.maximum(jnp.max(m_blk, axis=1, keepdims=True), jnp.max(s_own, axis=1, keepdims=True))
        wgt = jnp.where(sel, jnp.exp(m_blk - m_all), 0.0)
        p_own = jnp.exp(s_own - m_all)
        l_all = jnp.sum(wgt * l_ref[...], axis=1, keepdims=True) + jnp.sum(p_own, axis=1, keepdims=True)
        o = _dot(p_own.astype(BF16), vn_ref[0].astype(BF16))
        for n in range(nbp):
            o = o + wgt[:, n:n + 1] * acc_ref[n]
        o = o / l_all
        head_lane = lax.broadcasted_iota(I32, (rows, WKV), 1) // HEAD_DIM
        head_row = lax.broadcasted_iota(I32, (rows, WKV), 0) % H_B
        o = jnp.where(head_lane == head_row, o, 0.0)
        o_ref[0] = jnp.sum(o.reshape(s_len, H_B, WKV), axis=1)


def _moba_sample(page_table, qb_s, kn_pad, vn_pad, pool_k, pool_v, pages):
    bd, n_pages = page_table.shape
    rows = qb_s.shape[1]
    s_len = rows // H_B
    nbp = n_pages * PAGE_SIZE // MOBA_BLOCK
    per_b = lambda b, j, pt: (b, 0, 0)
    grid_spec = pltpu.PrefetchScalarGridSpec(
        num_scalar_prefetch=1, grid=(bd, n_pages // pages),
        in_specs=[pl.BlockSpec((1, rows, LANE), per_b), pl.BlockSpec((1, PAGE_SIZE, WKV), per_b),
                  pl.BlockSpec((1, PAGE_SIZE, WKV), per_b)] + _page_specs((WKV, PAGE_SIZE), pages) * 2,
        out_specs=pl.BlockSpec((1, s_len, WKV), per_b),
        scratch_shapes=[pltpu.VMEM((rows, LANE), F32), pltpu.VMEM((rows, LANE), F32), pltpu.VMEM((rows, LANE), F32),
                        pltpu.VMEM((nbp, rows, WKV), F32)])
    return pl.pallas_call(
        functools.partial(_moba_sample_kernel, pages=pages, n_sel=min(TOPK_BLK, nbp)), grid_spec=grid_spec,
        out_shape=jax.ShapeDtypeStruct((bd, s_len, WKV), F32),
        compiler_params=_params("parallel", "arbitrary"), name="moba_sample",
    )(page_table, qb_s, kn_pad, vn_pad, *([pool_k] * pages), *([pool_v] * pages))


def _split_w_in(w_in):
    d = w_in.shape[0]
    sizes = (H_A * HEAD_DIM, KV_A * HEAD_DIM, KV_A * HEAD_DIM, H_IDX * D_IDX, D_IDX, H_IDX, WKV, WKV, WKV, d, d)
    parts, o = [], 0
    for s in sizes:
        parts.append(w_in[:, o:o + s])
        o += s
    return parts


def _expand_heads(w, lane_offsets):
    z = jnp.zeros((w.shape[0], HEAD_DIM), w.dtype)
    parts = []
    for h, off in enumerate(lane_offsets):
        blk = w[:, h * HEAD_DIM:(h + 1) * HEAD_DIM]
        parts += [blk, z] if off == 0 else [z, blk]
    return jnp.concatenate(parts, axis=1)


def _layout_w_in(w_in):
    d = w_in.shape[0]
    qa, ka, va, qi, ki, wi, qb, kb, vb, ga, gb = _split_w_in(w_in)
    heads_per_group = H_A // KV_A
    qa_e = _expand_heads(qa, [(h // heads_per_group) * HEAD_DIM for h in range(H_A)])
    qb_e = _expand_heads(qb, [(h % 2) * HEAD_DIM for h in range(H_B)])
    kiwi = jnp.concatenate([ki, wi, jnp.zeros((d, LANE - D_IDX - H_IDX), w_in.dtype)], axis=1)
    ki_pad = jnp.concatenate([ki, jnp.zeros((d, LANE - D_IDX), w_in.dtype)], axis=1)
    w_sample = jnp.concatenate([qa_e, _expand_heads(qi, [0] * H_IDX), qb_e, ka, va, kiwi, kb, vb, ga, gb], axis=1)
    w_t = jnp.concatenate([qa_e, qi, qb_e, ka, va, ki, kb, vb, wi, jnp.zeros_like(wi)], axis=1).T
    w_r = jnp.concatenate([ka, ki_pad, kb, ga, gb], axis=1)
    return w_sample.astype(BF16), w_t.astype(BF16), w_r.astype(BF16)


def _rope_angles(pos):
    half = ROT_DIM // 2
    inv = ROPE_THETA ** (-jnp.arange(half, dtype=F32) / half)
    ang = pos.astype(F32)[:, None] * inv[None, :]
    return jnp.cos(ang), jnp.sin(ang)


def _rope_tables(pos):
    cos, sin = _rope_angles(pos)
    n = pos.shape[0]
    pad = jnp.zeros((n, HEAD_DIM - ROT_DIM), F32)
    zero = jnp.zeros((n, ROT_DIM // 2), F32)
    c = jnp.concatenate([cos, cos, pad + 1.0], axis=1)
    s1 = jnp.concatenate([-sin, zero, pad], axis=1)
    s2 = jnp.concatenate([zero, sin, pad], axis=1)
    return tuple(jnp.tile(t, (1, LANE // HEAD_DIM)) for t in (c, s1, s2))


def _pages_transposed(pool):
    n_pool, page = pool.shape[:2]
    nd = pool.ndim
    return jnp.transpose(pool, (0,) + tuple(range(2, nd)) + (1,)).reshape(n_pool, -1, page)


def _pad_rows(x, rows):
    b, s, w = x.shape
    return jnp.concatenate([x, jnp.zeros((b, rows - s, w), x.dtype)], axis=1)


def _ffn(oa, ob, ga, gb, x2d, lw, g_final, tm_merge, tm_moe, grouped):
    x1, h2, cw = _merge(oa, ob, ga, gb, x2d, lw["wa"], lw["wb"], lw["wo"], lw["g_ffn"], lw["w_router"],
                        lw["b_router"], tm_merge)
    if not grouped:
        return _moe(h2, cw, x1, lw["wg"], lw["wu"], lw["wd"], g_final, tm_moe)
    dest, tile_group, tiles_used, rows = _group_dispatch(cw[:, GSEL_LANE].astype(I32), tm_moe)
    xs = _scatter_rows(dest, x1, rows, TM_ROWS)
    ys = _moe_grouped(tile_group, tiles_used, xs, lw["g_ffn"], lw["w_router"], lw["b_router"], lw["wg"], lw["wu"], lw["wd"],
                      g_final, tm_moe)
    return _gather_rows(dest, ys, x1.shape[0], TM_ROWS)


@jax.jit
def kernel(x_prompt, x_sample, cache_k_a, cache_v_a, cache_k_idx, cache_k_b, cache_v_b, page_table, w_in, w_br_a,
           w_br_b, w_out, g_mix, g_ffn, w_grp, b_grp, w_exp, b_exp, w_e_gate, w_e_up, w_e_down, g_final):
    bsz, t, d = x_prompt.shape
    bd, s_len, _ = x_sample.shape
    depth = w_in.shape[0]
    n_pages = page_table.shape[1]
    past = n_pages * PAGE_SIZE
    assert depth == 1, "the final RMSNorm is fused into the last layer's MoE kernel"
    assert past % MOBA_BLOCK == 0 and s_len <= PAGE_SIZE and t % TM_PROJ == 0

    xp = x_prompt.reshape(bsz * t, d)
    xs = x_sample.reshape(bd * s_len, d)
    ns = bd * s_len

    l = 0
    zpad = jnp.zeros((d, LANE - N_GROUPS - N_EXPERTS), F32)
    lw = dict(
        wa=w_br_a[l].astype(BF16), wb=w_br_b[l].astype(BF16), wo=w_out[l].astype(BF16), g_ffn=g_ffn[l],
        w_router=jnp.concatenate([w_grp[l], w_exp[l], zpad], axis=1),
        b_router=jnp.concatenate([b_grp[l], b_exp[l], zpad[0]]).reshape(1, LANE),
        wg=w_e_gate[l].astype(BF16), wu=w_e_up[l].astype(BF16), wd=w_e_down[l].astype(BF16))
    w_sample, w_t, w_r = _layout_w_in(w_in[l])

    pos_p = jnp.arange(t)
    cos_p, sin_p = _rope_angles(pos_p)
    (qa_t, qi_t, qb_t, ka_t, va_t, ki_t, kb_t, vb_t, wi_t, kab, kib, kbb, va_tc, vb_tc, kmean, ga, gb) = _project_prompt(
        xp, g_mix[l], w_t, w_r, (cos_p.T, sin_p.T), _rope_tables(pos_p), bsz, t)
    o_a = _dsa_prompt(qa_t, qi_t, wi_t, kib, kab, va_tc, bsz, t)
    o_b = _moba_prompt(qb_t, kmean.reshape(bsz * t // MOBA_BLOCK, WKV), kbb, vb_tc, bsz, t)
    y_prompt = _ffn(o_a, o_b, ga, gb, xp, lw, g_final, TM_MERGE, TM_MOE, True).reshape(bsz, t, d)

    def heads_last(x_t, heads):
        return jnp.transpose(x_t.reshape(1, bsz, heads, HEAD_DIM, t), (0, 1, 4, 2, 3))

    rows_p = (heads_last(ka_t, KV_A), heads_last(va_t, KV_A), jnp.transpose(ki_t, (0, 2, 1))[None],
              heads_last(kb_t, H_B), heads_last(vb_t, H_B))

    tabs_s = tuple(jnp.tile(tb, (bd, 1)) for tb in _rope_tables(past + jnp.arange(s_len)))
    qa, qi, qb, ka, va, kiwi, kb, vb, ga, gb = _project_sample(xs, g_mix[l], w_sample, tabs_s)
    rows = s_len * N_HEAD_SLABS
    qa_s = qa.reshape(bd, rows, LANE)
    qi_s = qi.reshape(bd, rows, LANE)
    qb_s = qb.reshape(bd, rows, LANE)
    wi_s = kiwi[:, D_IDX:D_IDX + H_IDX].reshape(bd, rows, 1)
    s_past, s_new = _dsa_sample_scores(page_table, qi_s, wi_s, _pad_rows(kiwi.reshape(bd, s_len, LANE), PAGE_SIZE),
                                       _pages_transposed(cache_k_idx[l]), min(PAGES_SCORES, n_pages))
    bias_past, bias_new = _dsa_sample_select(s_past.reshape(ns, past), s_new.reshape(ns, PAGE_SIZE),
                                             min(TOPK_IDX, (past + s_len) // 4))
    o_a = _dsa_sample_attn(page_table, qa_s, bias_past.reshape(bd, s_len, past),
                           bias_new.reshape(bd, s_len, PAGE_SIZE),
                           _pad_rows(ka.reshape(bd, s_len, LANE), PAGE_SIZE),
                           _pad_rows(va.reshape(bd, s_len, LANE), PAGE_SIZE),
                           _pages_transposed(cache_k_a[l]), _pages_transposed(cache_v_a[l]), min(PAGES_ATTN, n_pages))
    o_a = o_a.reshape(bd, s_len, H_A, KV_A, HEAD_DIM)
    hpg = H_A // KV_A
    o_a = jnp.concatenate([o_a[:, :, g * hpg:(g + 1) * hpg, g] for g in range(KV_A)], axis=2)
    o_a = o_a.reshape(ns, H_A * HEAD_DIM).astype(BF16)
    o_b = _moba_sample(page_table, qb_s, _pad_rows(kb.reshape(bd, s_len, WKV), PAGE_SIZE),
                       _pad_rows(vb.reshape(bd, s_len, WKV), PAGE_SIZE),
                       _pages_transposed(cache_k_b[l]), _pages_transposed(cache_v_b[l]), min(PAGES_MOBA, n_pages))
    o_b = o_b.reshape(ns, WKV).astype(BF16)
    y_sample = _ffn(o_a, o_b, ga, gb, xs, lw, g_final, ns, ns, False).reshape(bd, s_len, d)
    rows_s = (ka.reshape(1, bd, s_len, KV_A, HEAD_DIM), va.reshape(1, bd, s_len, KV_A, HEAD_DIM),
              kiwi[:, :D_IDX].reshape(1, bd, s_len, D_IDX), kb.reshape(1, bd, s_len, H_B, HEAD_DIM),
              vb.reshape(1, bd, s_len, H_B, HEAD_DIM))
    return (y_prompt, y_sample) + rows_p + rows_s
```

```python
import functools

import jax
import jax.numpy as jnp
from jax import lax
from jax.experimental import pallas as pl
from jax.experimental.pallas import tpu as pltpu

F32, BF16, I32 = jnp.float32, jnp.bfloat16, jnp.int32

HEAD_DIM = 64
ROT_DIM = HEAD_DIM // 4
ROPE_THETA = 500000.0
H_A = 8
KV_A = 2
H_IDX = 8
D_IDX = 64
TOPK_IDX = 256
H_B = 8
MOBA_BLOCK = 256
TOPK_BLK = 3
N_GROUPS = 4
EXPERTS_PER_GROUP = 8
N_EXPERTS = N_GROUPS * EXPERTS_PER_GROUP
D_EXPERT = 256
RMS_EPS = 1e-6
PAGE_SIZE = 128

LANE = 128
SUBLANE = 8
BF16_ROWS = 16
VMEM_LIMIT = 56 * 1024 * 1024

TM_PROJ = 512
DSA_Q = 256
DSA_KC = TM_PROJ
DSA_HEAD_SPLITS = 1
TM_MERGE = 512
TM_MOE = 1024
TM_ROWS = 512
ROW_DMA_UNROLL = 8
PAGES_SCORES = 128
PAGES_ATTN = 128
PAGES_MOBA = 16

NEG = -1e30
INT_MIN = -(2 ** 31)
KEY_NEG_INF = -2139095041
N_HEAD_SLABS = 8
Q_EXP = N_HEAD_SLABS * LANE
WKV = H_B * HEAD_DIM
SOFTMAX_SCALE = HEAD_DIM ** -0.5
assert SOFTMAX_SCALE == 0.125

_NT = (((1,), (1,)), ((), ()))
_HI = lax.Precision.HIGHEST


def _dot(a, b, precision=None):
    return jnp.dot(a, b, preferred_element_type=F32, precision=precision)


def _dot_nt(a, b, precision=None):
    return lax.dot_general(a, b, _NT, preferred_element_type=F32, precision=precision)


def _params(*sem):
    return pltpu.CompilerParams(dimension_semantics=sem, vmem_limit_bytes=VMEM_LIMIT)


def _rms_scale(x):
    return lax.rsqrt(jnp.mean(x * x, axis=-1, keepdims=True) + RMS_EPS)


def _rope(y, c, s1, s2):
    half = ROT_DIM // 2
    return y * c + pltpu.roll(y, LANE - half, 1) * s1 + pltpu.roll(y, half, 1) * s2


def _rope_t(y, cos_t, sin_t):
    half = ROT_DIM // 2
    parts = []
    for r in range(0, y.shape[0], HEAD_DIM):
        x1, x2 = y[r:r + half], y[r + half:r + ROT_DIM]
        parts += [x1 * cos_t - x2 * sin_t, x2 * cos_t + x1 * sin_t, y[r + ROT_DIM:r + HEAD_DIM]]
    return jnp.concatenate(parts, axis=0)


def _proj_prompt_kernel(x_ref, g_ref, wt_ref, wr_ref, cos_t_ref, sin_t_ref, c_ref, s1_ref, s2_ref,
                        qa_t_ref, qi_t_ref, qb_t_ref, ka_t_ref, va_t_ref, ki_t_ref, kb_t_ref, vb_t_ref, wi_t_ref,
                        kab_ref, kib_ref, kbb_ref, va_tc_ref, vb_tc_ref, kmean_ref, ga_ref, gb_ref):
    tm, d_model = x_ref.shape
    x = x_ref[...]
    h = ((x * _rms_scale(x)) * g_ref[...]).astype(BF16)
    cos_t, sin_t = cos_t_ref[...], sin_t_ref[...]
    c, s1, s2 = c_ref[...], s1_ref[...], s2_ref[...]

    off = [0]

    def proj_t(rows):
        y = _dot_nt(wt_ref[off[0]:off[0] + rows, :], h)
        off[0] += rows
        return y

    half_q = Q_EXP // 2
    for r0 in (0, half_q):
        qa_t_ref[0, r0:r0 + half_q, :] = _rope_t(proj_t(half_q), cos_t, sin_t).astype(BF16)
    qi_t_ref[0] = _rope_t(proj_t(H_IDX * D_IDX), cos_t, sin_t).astype(BF16)
    for r0 in (0, half_q):
        qb_t_ref[0, r0:r0 + half_q, :] = _rope_t(proj_t(half_q), cos_t, sin_t).astype(BF16)
    ka_t_ref[0] = _rope_t(proj_t(KV_A * HEAD_DIM), cos_t, sin_t)
    va_t = proj_t(KV_A * HEAD_DIM)
    va_t_ref[0] = va_t
    va_tc_ref[0] = va_t.astype(BF16)
    ki_t_ref[0] = _rope_t(proj_t(D_IDX), cos_t, sin_t)
    kb_t_ref[0] = _rope_t(proj_t(WKV), cos_t, sin_t)
    vb_t = proj_t(WKV)
    vb_t_ref[0] = vb_t
    for n in range(tm // MOBA_BLOCK):
        vb_tc_ref[n] = vb_t[:, n * MOBA_BLOCK:(n + 1) * MOBA_BLOCK].astype(BF16)
    wi_t_ref[0] = proj_t(2 * H_IDX)[:H_IDX]

    col = [0]

    def proj(width):
        y = _dot(h, wr_ref[:, col[0]:col[0] + width])
        col[0] += width
        return y

    y = proj(2 * LANE)
    kab_ref[...] = _rope(y[:, :LANE], c, s1, s2).astype(BF16)
    kib_ref[...] = _rope(y[:, LANE:], c, s1, s2)[:, :D_IDX].astype(BF16)
    y = proj(WKV)
    kb = jnp.concatenate([_rope(y[:, j * LANE:(j + 1) * LANE], c, s1, s2) for j in range(WKV // LANE)], axis=1)
    kbb_ref[...] = kb.astype(BF16)
    nblk = tm // MOBA_BLOCK
    kmean_ref[0] = jnp.sum(kb.reshape(nblk, MOBA_BLOCK, WKV), axis=1) * (1.0 / MOBA_BLOCK)
    ga_ref[...] = proj(d_model)
    gb_ref[...] = proj(d_model)


def _project_prompt(x2d, g, w_t, w_r, tabs_t, tabs, bsz, t):
    n, d = x2d.shape
    tm = TM_PROJ
    tpb = t // tm
    row = lambda i: (i, 0)
    const = lambda i: (0, 0)
    bt = lambda i: (i // tpb, 0, i % tpb)
    chunk = lambda i: (i, 0, 0)
    once = dict(pipeline_mode=pl.Buffered(1))
    in_specs = [pl.BlockSpec((tm, d), row), pl.BlockSpec((1, d), const),
                pl.BlockSpec(w_t.shape, const, **once), pl.BlockSpec(w_r.shape, const, **once)]
    in_specs += [pl.BlockSpec((ROT_DIM // 2, tm), lambda i: (0, i % tpb))] * 2
    in_specs += [pl.BlockSpec((tm, LANE), lambda i: (i % tpb, 0))] * 3
    t_feats = [(Q_EXP, BF16), (H_IDX * D_IDX, BF16), (Q_EXP, BF16), (KV_A * HEAD_DIM, F32), (KV_A * HEAD_DIM, F32),
               (D_IDX, F32), (WKV, F32), (WKV, F32), (H_IDX, F32)]
    out_shape = [jax.ShapeDtypeStruct((bsz, f, t), dt) for f, dt in t_feats]
    out_specs = [pl.BlockSpec((1, f, tm), bt) for f, _ in t_feats]
    r_feats = [(KV_A * HEAD_DIM, BF16), (D_IDX, BF16), (WKV, BF16)]
    out_shape += [jax.ShapeDtypeStruct((n, f), dt) for f, dt in r_feats]
    out_specs += [pl.BlockSpec((tm, f), row) for f, _ in r_feats]
    nblk = tm // MOBA_BLOCK
    out_shape += [jax.ShapeDtypeStruct((n // tm, KV_A * HEAD_DIM, tm), BF16),
                  jax.ShapeDtypeStruct((n // MOBA_BLOCK, WKV, MOBA_BLOCK), BF16),
                  jax.ShapeDtypeStruct((n // tm, nblk, WKV), F32),
                  jax.ShapeDtypeStruct((n, d), F32), jax.ShapeDtypeStruct((n, d), F32)]
    out_specs += [pl.BlockSpec((1, KV_A * HEAD_DIM, tm), chunk), pl.BlockSpec((nblk, WKV, MOBA_BLOCK), chunk),
                  pl.BlockSpec((1, nblk, WKV), chunk), pl.BlockSpec((tm, d), row), pl.BlockSpec((tm, d), row)]
    return pl.pallas_call(
        _proj_prompt_kernel, grid=(n // tm,), in_specs=in_specs, out_specs=out_specs, out_shape=out_shape,
        compiler_params=_params("parallel"), name="proj_prompt",
    )(x2d, g.reshape(1, d), w_t, w_r, *tabs_t, *tabs)


def _proj_sample_kernel(x_ref, g_ref, w_ref, c_ref, s1_ref, s2_ref,
                        qa_ref, qi_ref, qb_ref, ka_ref, va_ref, kiwi_ref, kb_ref, vb_ref, ga_ref, gb_ref):
    tm, d_model = x_ref.shape
    x = x_ref[...]
    h = ((x * _rms_scale(x)) * g_ref[...]).astype(BF16)
    c, s1, s2 = c_ref[...], s1_ref[...], s2_ref[...]
    off = [0]

    def proj(width):
        y = _dot(h, w_ref[:, off[0]:off[0] + width])
        off[0] += width
        return y

    def rope_slabs(y):
        return jnp.concatenate([_rope(y[:, j * LANE:(j + 1) * LANE], c, s1, s2) for j in range(y.shape[1] // LANE)],
                               axis=1)

    for q_ref in (qa_ref, qi_ref, qb_ref):
        q_ref[...] = rope_slabs(proj(Q_EXP)).astype(BF16)
    y = proj(3 * LANE)
    ka_ref[...] = _rope(y[:, :LANE], c, s1, s2)
    va_ref[...] = y[:, LANE:2 * LANE]
    is_ki = lax.broadcasted_iota(I32, (tm, LANE), 1) < D_IDX
    kiwi_ref[...] = _rope(y[:, 2 * LANE:], jnp.where(is_ki, c, 1.0), jnp.where(is_ki, s1, 0.0),
                          jnp.where(is_ki, s2, 0.0))
    kb_ref[...] = rope_slabs(proj(WKV))
    vb_ref[...] = proj(WKV)
    ga_ref[...] = proj(d_model)
    gb_ref[...] = proj(d_model)


def _project_sample(x2d, g, w_all, tabs):
    n, d = x2d.shape
    const = lambda i: (0, 0)
    widths = [(Q_EXP, BF16)] * 3 + [(LANE, F32)] * 3 + [(WKV, F32)] * 2 + [(d, F32)] * 2
    return pl.pallas_call(
        _proj_sample_kernel, grid=(1,),
        in_specs=[pl.BlockSpec((n, d), const), pl.BlockSpec((1, d), const),
                  pl.BlockSpec(w_all.shape, const, pipeline_mode=pl.Buffered(1))] + [pl.BlockSpec((n, LANE), const)] * 3,
        out_specs=[pl.BlockSpec((n, w), const) for w, _ in widths],
        out_shape=[jax.ShapeDtypeStruct((n, w), dt) for w, dt in widths],
        compiler_params=_params("arbitrary"), name="proj_sample",
    )(x2d, g.reshape(1, d), w_all, *tabs)


def _key_to_float(key):
    return pltpu.bitcast(key ^ ((key >> 31) & 0x7FFFFFFF), F32)


def _select_bias(score_ref, bias_ref, nc, k_top, n_keys_log2, key_axis):
    _, d0, d1 = score_ref.shape
    kc = (d0, d1)[key_axis]
    kidx0 = lax.broadcasted_iota(I32, (d0, d1), key_axis)
    k_top = float(k_top)
    qshape = (1, d1) if key_axis == 0 else (d0, 1)

    def fold(f):
        if key_axis == 0:
            return jnp.sum(f.reshape(SUBLANE, d0 // SUBLANE, d1), axis=0)
        part = f[:, :LANE]
        for j in range(1, d1 // LANE):
            part = part + f[:, j * LANE:(j + 1) * LANE]
        return part

    def count(pred):
        def body(c, cnt):
            return cnt + fold(jnp.where(pred(score_ref[c], c), 1.0, 0.0))
        zero = jnp.zeros((d0 // SUBLANE, d1) if key_axis == 0 else (d0, LANE), F32)
        return jnp.sum(lax.fori_loop(0, nc, body, zero), axis=key_axis, keepdims=True)

    nonneg = count(lambda s, c: s >= 0.0)
    v0 = jnp.where(nonneg >= k_top, 0, INT_MIN).astype(I32)

    def bit_body(b, carry):
        v, at_least = carry
        cand = v | jnp.left_shift(jnp.int32(1), 30 - b)
        cf = _key_to_float(cand)
        cnt = count(lambda s, c: s >= cf)
        ok = cnt >= k_top
        return jnp.where(ok, cand, v), jnp.where(ok, cnt, at_least)

    v, at_least = lax.fori_loop(0, 31, bit_body, (v0, nonneg))
    thr = _key_to_float(jnp.maximum(v, KEY_NEG_INF))

    tied = (at_least > k_top) & (thr > -jnp.inf)
    all_keys = jnp.full(qshape, 2 ** n_keys_log2 - 1, I32)

    def tie_search():
        need = k_top - count(lambda s, c: s > thr)

        def tie_body(b, u):
            cand = u | jnp.left_shift(jnp.int32(1), n_keys_log2 - 1 - b)
            below = count(lambda s, c: (s == thr) & (c * kc + kidx0 < cand))
            return jnp.where(below < need, cand, u)

        return lax.fori_loop(0, n_keys_log2, tie_body, jnp.zeros(qshape, I32))

    u = lax.cond(jnp.max(jnp.where(tied, 1.0, 0.0)) > 0.0, tie_search, lambda: all_keys)

    def bias_body(c, carry):
        s = score_ref[c]
        sel = (s > -jnp.inf) & ((s > thr) | ((s == thr) & (c * kc + kidx0 <= u)))
        bias_ref[c] = jnp.where(sel, 0.0, NEG)
        return carry

    lax.fori_loop(0, nc, bias_body, 0)


def _dsa_prompt_kernel(qa_t_ref, qi_t_ref, wi_t_ref, kib_ref, kab_ref, va_tc_ref, o_ref, score_ref, bias_ref, *, k_top):
    nchunks, kc, q = score_ref.shape
    i = pl.program_id(1)
    nc = (i * q + q - 1) // kc + 1
    key0 = lax.broadcasted_iota(I32, (kc, q), 0)
    qpos = i * q + lax.broadcasted_iota(I32, (kc, q), 1)
    w = wi_t_ref[0] * (H_IDX ** -0.5 * D_IDX ** -0.5)
    qi_all = jnp.concatenate([qi_t_ref[0, h * D_IDX:(h + 1) * D_IDX, :] for h in range(H_IDX)], axis=1)

    def score_body(c, carry):
        d = _dot(kib_ref[pl.ds(pl.multiple_of(c * kc, kc), kc), :], qi_all)
        acc = jnp.zeros((kc, q), F32)
        for h in range(H_IDX):
            acc = acc + jnp.maximum(d[:, h * q:(h + 1) * q], 0.0) * w[h:h + 1, :]
        score_ref[c] = jnp.where(c * kc + key0 <= qpos, acc, -jnp.inf)
        return carry

    lax.fori_loop(0, nc, score_body, 0)
    _select_bias(score_ref, bias_ref, nc, k_top, (nchunks * kc - 1).bit_length(), key_axis=0)

    qa_all = jnp.concatenate([qa_t_ref[0, h * LANE:(h + 1) * LANE, :] for h in range(H_A)], axis=1)
    qa_all = qa_all * SOFTMAX_SCALE

    hs = H_A // DSA_HEAD_SPLITS
    q_parts = [qa_all[:, g * hs * q:(g + 1) * hs * q] for g in range(DSA_HEAD_SPLITS)]

    dv = KV_A * HEAD_DIM
    ones_rows = jnp.ones((BF16_ROWS, kc), BF16)

    def att_body(c, carry):
        k = kab_ref[pl.ds(pl.multiple_of(c * kc, kc), kc), :]
        v = jnp.concatenate([va_tc_ref[c], ones_rows], axis=0)
        b = bias_ref[c]
        splits = range(DSA_HEAD_SPLITS)
        qk = [_dot(k, q_parts[g]) for g in splits]
        s = [jnp.concatenate([qk[g][:, h * q:(h + 1) * q] + b for h in range(hs)], axis=1) for g in splits]
        m_new = [jnp.maximum(carry[g][0], jnp.max(s[g], axis=0, keepdims=True)) for g in splits]
        p = [jnp.exp((s[g] - m_new[g]).astype(BF16)) for g in splits]
        pv = [_dot(v, p[g]) for g in splits]
        out = []
        for g in splits:
            m, l, acc = carry[g]
            alpha = jnp.exp(m - m_new[g])
            out.append((m_new[g], alpha * l + pv[g][dv:dv + 1], alpha * acc + pv[g][:dv]))
        return tuple(out)

    init = tuple((jnp.full((1, hs * q), NEG, F32), jnp.zeros((1, hs * q), F32),
                  jnp.zeros((KV_A * HEAD_DIM, hs * q), F32)) for _ in range(DSA_HEAD_SPLITS))
    final = lax.fori_loop(0, nc, att_body, init)
    o = jnp.concatenate([acc / l for _, l, acc in final], axis=1)
    heads_per_group = H_A // KV_A
    for j in range(H_A // 2):
        g0 = ((2 * j) // heads_per_group) * HEAD_DIM
        pair = jnp.concatenate([o[g0:g0 + HEAD_DIM, (2 * j) * q:(2 * j + 1) * q],
                                o[g0:g0 + HEAD_DIM, (2 * j + 1) * q:(2 * j + 2) * q]], axis=0)
        o_ref[:, j * LANE:(j + 1) * LANE] = pair.T.astype(BF16)


def _dsa_prompt(qa_t, qi_t, wi_t, kib, kab, va_tc, bsz, t):
    q, kc = DSA_Q, DSA_KC
    nq, nchunks = t // q, t // kc
    tile_t = lambda b, i: (b, 0, i)
    return pl.pallas_call(
        functools.partial(_dsa_prompt_kernel, k_top=min(TOPK_IDX, t // 4)),
        grid=(bsz, nq),
        in_specs=[pl.BlockSpec((1, Q_EXP, q), tile_t), pl.BlockSpec((1, H_IDX * D_IDX, q), tile_t),
                  pl.BlockSpec((1, H_IDX, q), tile_t),
                  pl.BlockSpec((t, D_IDX), lambda b, i: (b, 0)), pl.BlockSpec((t, KV_A * HEAD_DIM), lambda b, i: (b, 0)),
                  pl.BlockSpec((nchunks, KV_A * HEAD_DIM, kc), lambda b, i: (b, 0, 0))],
        out_specs=pl.BlockSpec((q, H_A * HEAD_DIM), lambda b, i: (b * nq + i, 0)),
        out_shape=jax.ShapeDtypeStruct((bsz * t, H_A * HEAD_DIM), BF16),
        scratch_shapes=[pltpu.VMEM((nchunks, kc, q), F32), pltpu.VMEM((nchunks, kc, q), F32)],
        compiler_params=_params("parallel", "arbitrary"), name="dsa_prompt",
    )(qa_t, qi_t, wi_t, kib, kab, va_tc)


def _top_blocks(gate, valid, idxf, n_sel, axis):
    gate = jnp.where(valid, gate, -jnp.inf)
    sel = jnp.zeros(gate.shape, jnp.bool_)
    for _ in range(n_sel):
        mx = jnp.max(gate, axis=axis, keepdims=True)
        first = jnp.min(jnp.where(gate == mx, idxf, 1e9), axis=axis, keepdims=True)
        pick = idxf == first
        sel = sel | (pick & valid)
        gate = jnp.where(pick, -jnp.inf, gate)
    return sel


def _moba_prompt_kernel(qb_t_ref, km_ref, kb_ref, vb_tc_ref, o_ref, *, n_sel):
    qb = o_ref.shape[0]
    nb = km_ref.shape[0]
    j = pl.program_id(1)
    r = 2 * qb
    nrow = lax.broadcasted_iota(I32, (nb, r), 0)
    nrowf = nrow.astype(F32)
    qc = lax.broadcasted_iota(I32, (qb, r), 1)
    own_mask = lax.broadcasted_iota(I32, (qb, r), 0) <= jnp.where(qc >= qb, qc - qb, qc)
    own = pl.multiple_of(j * qb, qb)

    pairs = range(H_B // 2)
    sls = [slice(p * LANE, (p + 1) * LANE) for p in pairs]
    ones_rows = jnp.ones((BF16_ROWS, MOBA_BLOCK), BF16)

    def pv_and_sum(n, p, pr):
        return _dot(jnp.concatenate([vb_tc_ref[n, sls[p], :], ones_rows], axis=0), pr)

    q_raw = [jnp.concatenate([qb_t_ref[0, (2 * p) * LANE:(2 * p + 1) * LANE, :],
                              qb_t_ref[0, (2 * p + 1) * LANE:(2 * p + 2) * LANE, :]], axis=1) for p in pairs]
    gates = [_dot(km_ref[:, sls[p]], q_raw[p].astype(F32), precision=_HI) for p in pairs]
    selbiases = [jnp.where(_top_blocks(gates[p], nrow < j, nrowf, n_sel, axis=0), 0.0, NEG) for p in pairs]
    qes = [q * SOFTMAX_SCALE for q in q_raw]
    s_own = [jnp.where(own_mask, _dot(kb_ref[pl.ds(own, qb), sls[p]], qes[p]), NEG) for p in pairs]
    m_own = [jnp.max(s, axis=0, keepdims=True) for s in s_own]
    pv_own = [pv_and_sum(j, p, jnp.exp((s_own[p] - m_own[p]).astype(BF16))) for p in pairs]
    init = [(m_own[p], pv_own[p][LANE:LANE + 1], pv_own[p][:LANE]) for p in pairs]

    half = lax.broadcasted_iota(I32, (2 * MOBA_BLOCK, r), 0) < MOBA_BLOCK

    def body(n2, carry):
        start = pl.multiple_of(n2 * (2 * MOBA_BLOCK), 2 * MOBA_BLOCK)
        qk = [_dot(kb_ref[pl.ds(start, 2 * MOBA_BLOCK), sls[p]], qes[p]) for p in pairs]
        rowb = [[jnp.sum(jnp.where(nrow == 2 * n2 + i, selbiases[p], 0.0), axis=0, keepdims=True) for i in (0, 1)]
                for p in pairs]
        s = [qk[p] + jnp.where(half, rowb[p][0], rowb[p][1]) for p in pairs]
        m_new = [jnp.maximum(carry[p][0], jnp.max(s[p], axis=0, keepdims=True)) for p in pairs]
        pr = [jnp.exp((s[p] - m_new[p]).astype(BF16)) for p in pairs]
        pv = [pv_and_sum(2 * n2, p, pr[p][:MOBA_BLOCK]) + pv_and_sum(2 * n2 + 1, p, pr[p][MOBA_BLOCK:]) for p in pairs]
        out = []
        for p in pairs:
            m, l, acc = carry[p]
            alpha = jnp.exp(m - m_new[p])
            out.append((m_new[p], alpha * l + pv[p][LANE:LANE + 1], alpha * acc + pv[p][:LANE]))
        return tuple(out)

    final = lax.fori_loop(0, (j + 1) // 2, body, tuple(init))
    for p in pairs:
        _, l, acc = final[p]
        o = acc / l
        pair = jnp.concatenate([o[:HEAD_DIM, :qb], o[HEAD_DIM:, qb:]], axis=0)
        o_ref[:, sls[p]] = pair.T.astype(BF16)


def _moba_prompt(qb_t, kmean, kbb, vb_tc, bsz, t):
    nb = t // MOBA_BLOCK
    return pl.pallas_call(
        functools.partial(_moba_prompt_kernel, n_sel=min(TOPK_BLK, nb - 1)),
        grid=(bsz, nb),
        in_specs=[pl.BlockSpec((1, Q_EXP, MOBA_BLOCK), lambda b, i: (b, 0, i)), pl.BlockSpec((nb, WKV), lambda b, i: (b, 0)),
                  pl.BlockSpec((t, WKV), lambda b, i: (b, 0)),
                  pl.BlockSpec((nb, WKV, MOBA_BLOCK), lambda b, i: (b, 0, 0))],
        out_specs=pl.BlockSpec((MOBA_BLOCK, WKV), lambda b, i: (b * nb + i, 0)),
        out_shape=jax.ShapeDtypeStruct((bsz * t, WKV), BF16),
        compiler_params=_params("parallel", "arbitrary"), name="moba_prompt",
    )(qb_t, kmean, kbb, vb_tc)


GSEL_LANE = LANE - 1


def _route(rl, group=None):
    lanef = lax.broadcasted_iota(I32, rl.shape, 1).astype(F32)

    def masked_softmax(mask):
        z = jnp.where(mask, rl, -jnp.inf)
        e = jnp.exp(z - jnp.max(z, axis=1, keepdims=True))
        return e / jnp.sum(e, axis=1, keepdims=True)

    def first_lane(cond):
        return jnp.min(jnp.where(cond, lanef, 1e9), axis=1, keepdims=True)

    gmask = lanef < N_GROUPS
    gp = masked_softmax(gmask)
    if group is None:
        gprob = jnp.max(gp, axis=1, keepdims=True)
        gsel = first_lane((gp == gprob) & gmask)
    else:
        gsel = jnp.full((rl.shape[0], 1), group, I32).astype(F32)
        gprob = jnp.sum(jnp.where(lanef == gsel, gp, 0.0), axis=1, keepdims=True)
    lo = N_GROUPS + EXPERTS_PER_GROUP * gsel
    emask = (lanef >= lo) & (lanef < lo + EXPERTS_PER_GROUP)
    ep = jnp.where(emask, masked_softmax(emask), -1.0)
    p1 = jnp.max(ep, axis=1, keepdims=True)
    i1 = first_lane(ep == p1)
    ep = jnp.where(lanef == i1, -1.0, ep)
    p2 = jnp.max(ep, axis=1, keepdims=True)
    i2 = first_lane(ep == p2)
    den = p1 + p2
    cw = jnp.where(lanef == i1, gprob * p1 / den, 0.0) + jnp.where(lanef == i2, gprob * p2 / den, 0.0)
    return jnp.where(lanef == GSEL_LANE, gsel, cw)


def _merge_kernel(oa_ref, ob_ref, ga_ref, gb_ref, x_ref, wa_ref, wb_ref, wo_ref, gf_ref, wr_ref, br_ref,
                  x1_ref, h2_ref, cw_ref):
    tm = x_ref.shape[0]
    u = jax.nn.sigmoid(ga_ref[...]) * _dot(oa_ref[...], wa_ref[...]) \
        + jax.nn.sigmoid(gb_ref[...]) * _dot(ob_ref[...], wb_ref[...])
    x1 = x_ref[...] + _dot(u.astype(BF16), wo_ref[...])
    x1_ref[...] = x1
    h2 = (x1 * _rms_scale(x1)) * gf_ref[...]
    h2_ref[...] = h2.astype(BF16)

    cw_ref[...] = _route(_dot(h2, wr_ref[...], precision=_HI) + br_ref[...])


def _merge(oa, ob, ga, gb, x2d, wa, wb, wo, g_ffn, w_router, b_router, tm):
    n, d = x2d.shape
    row = lambda i: (i, 0)
    const = lambda i: (0, 0)
    ha = oa.shape[1]
    return pl.pallas_call(
        _merge_kernel, grid=(n // tm,),
        in_specs=[pl.BlockSpec((tm, ha), row), pl.BlockSpec((tm, ha), row), pl.BlockSpec((tm, d), row),
                  pl.BlockSpec((tm, d), row), pl.BlockSpec((tm, d), row), pl.BlockSpec((ha, d), const),
                  pl.BlockSpec((ha, d), const), pl.BlockSpec((d, d), const), pl.BlockSpec((1, d), const),
                  pl.BlockSpec((d, LANE), const), pl.BlockSpec((1, LANE), const)],
        out_specs=[pl.BlockSpec((tm, d), row), pl.BlockSpec((tm, d), row), pl.BlockSpec((tm, LANE), row)],
        out_shape=[jax.ShapeDtypeStruct((n, d), F32), jax.ShapeDtypeStruct((n, d), BF16),
                   jax.ShapeDtypeStruct((n, LANE), F32)],
        compiler_params=_params("parallel"), name="merge",
    )(oa, ob, ga, gb, x2d, wa, wb, wo, g_ffn.reshape(1, d), w_router, b_router)


def _moe_kernel(h_ref, cw_ref, x_ref, wg_ref, wu_ref, wd_ref, gfin_ref, y_ref, acc_ref):
    tm = h_ref.shape[0]
    e = pl.program_id(1)

    @pl.when(e == 0)
    def _():
        acc_ref[...] = jnp.zeros_like(acc_ref)

    h = h_ref[...]
    a = _dot(h, wg_ref[0])
    hdn = (a * jax.nn.sigmoid(a)) * _dot(h, wu_ref[0])
    y = _dot(hdn.astype(BF16), wd_ref[0])
    lane = lax.broadcasted_iota(I32, (tm, LANE), 1)
    col = jnp.sum(jnp.where(lane == e + N_GROUPS, cw_ref[...], 0.0), axis=1, keepdims=True)
    acc_ref[...] += y * col

    @pl.when(e == pl.num_programs(1) - 1)
    def _():
        out = x_ref[...] + acc_ref[...]
        y_ref[...] = (out * _rms_scale(out)) * gfin_ref[...]


def _moe(h2, cw, x1, wg, wu, wd, g_final, tm):
    n, d = x1.shape
    ne, _, de = wg.shape
    row = lambda i, e: (i, 0)
    return pl.pallas_call(
        _moe_kernel, grid=(n // tm, ne),
        in_specs=[pl.BlockSpec((tm, d), row), pl.BlockSpec((tm, LANE), row), pl.BlockSpec((tm, d), row),
                  pl.BlockSpec((1, d, de), lambda i, e: (e, 0, 0)), pl.BlockSpec((1, d, de), lambda i, e: (e, 0, 0)),
                  pl.BlockSpec((1, de, d), lambda i, e: (e, 0, 0)), pl.BlockSpec((1, d), lambda i, e: (0, 0))],
        out_specs=pl.BlockSpec((tm, d), row),
        out_shape=jax.ShapeDtypeStruct((n, d), F32),
        scratch_shapes=[pltpu.VMEM((tm, d), F32)],
        compiler_params=_params("parallel", "arbitrary"), name="moe",
    )(h2, cw, x1, wg, wu, wd, g_final.reshape(1, d))


def _scatter_rows_kernel(dest_ref, x_ref, zeros_ref, xs_ref, sem):
    del zeros_ref
    tm = x_ref.shape[0]
    base = pl.program_id(0) * tm

    def row_copy(r, dst):
        return pltpu.make_async_copy(x_ref.at[pl.ds(r, 1)], xs_ref.at[pl.ds(dst, 1)], sem)

    _row_dma_burst(tm, lambda r: row_copy(r, dest_ref[base + r]), lambda r: row_copy(r, 0))


def _row_dma_burst(rows, start_copy, wait_copy):
    def start(g, carry):
        for u in range(ROW_DMA_UNROLL):
            start_copy(g * ROW_DMA_UNROLL + u).start(priority=u % 2)
        return carry

    def wait(g, carry):
        for u in range(ROW_DMA_UNROLL):
            wait_copy(g * ROW_DMA_UNROLL + u).wait()
        return carry

    lax.fori_loop(0, rows // ROW_DMA_UNROLL, start, 0)
    lax.fori_loop(0, rows // ROW_DMA_UNROLL, wait, 0)


def _scatter_rows(dest, x, n_rows, tm):
    n, d = x.shape
    grid_spec = pltpu.PrefetchScalarGridSpec(
        num_scalar_prefetch=1, grid=(n // tm,),
        in_specs=[pl.BlockSpec((tm, d), lambda i, dest: (i, 0)), pl.BlockSpec(memory_space=pl.ANY)],
        out_specs=pl.BlockSpec(memory_space=pl.ANY),
        scratch_shapes=[pltpu.SemaphoreType.DMA(())])
    return pl.pallas_call(
        _scatter_rows_kernel, grid_spec=grid_spec, out_shape=jax.ShapeDtypeStruct((n_rows, d), x.dtype),
        input_output_aliases={2: 0}, compiler_params=_params("arbitrary"), name="moe_scatter_rows",
    )(dest, x, jnp.zeros((n_rows, d), x.dtype))


def _gather_rows_kernel(dest_ref, ys_ref, y_ref, sem):
    tm = y_ref.shape[0]
    base = pl.program_id(0) * tm

    def row_copy(r, src):
        return pltpu.make_async_copy(ys_ref.at[pl.ds(src, 1)], y_ref.at[pl.ds(r, 1)], sem)

    _row_dma_burst(tm, lambda r: row_copy(r, dest_ref[base + r]), lambda r: row_copy(r, 0))


def _gather_rows(dest, ys, n, tm):
    d = ys.shape[1]
    grid_spec = pltpu.PrefetchScalarGridSpec(
        num_scalar_prefetch=1, grid=(n // tm,),
        in_specs=[pl.BlockSpec(memory_space=pl.ANY)],
        out_specs=pl.BlockSpec((tm, d), lambda i, dest: (i, 0)),
        scratch_shapes=[pltpu.SemaphoreType.DMA(())])
    return pl.pallas_call(
        _gather_rows_kernel, grid_spec=grid_spec, out_shape=jax.ShapeDtypeStruct((n, d), ys.dtype),
        compiler_params=_params("arbitrary"), name="moe_gather_rows",
    )(dest, ys)


def _moe_grouped_kernel(tg_ref, used_ref, x_ref, gf_ref, wr_ref, br_ref, wg_ref, wu_ref, wd_ref, gfin_ref, y_ref,
                        h_ref, cw_ref, acc_ref):
    tm = x_ref.shape[0]
    i, e = pl.program_id(0), pl.program_id(1)
    group = tg_ref[i]
    live = i < used_ref[0]
    last = e == pl.num_programs(1) - 1

    @pl.when(live & (e == 0))
    def _():
        x1 = x_ref[...]
        h2 = (x1 * _rms_scale(x1)) * gf_ref[...]
        h_ref[...] = h2.astype(BF16)
        cw_ref[...] = _route(_dot(h2, wr_ref[...], precision=_HI) + br_ref[...], group)
        acc_ref[...] = jnp.zeros_like(acc_ref)

    @pl.when(live)
    def _():
        h = h_ref[...]
        a = _dot(h, wg_ref[0])
        hdn = (a * jax.nn.sigmoid(a)) * _dot(h, wu_ref[0])
        y = _dot(hdn.astype(BF16), wd_ref[0])
        lane = lax.broadcasted_iota(I32, (tm, LANE), 1)
        expert_lane = N_GROUPS + group * EXPERTS_PER_GROUP + e
        col = jnp.sum(jnp.where(lane == expert_lane, cw_ref[...], 0.0), axis=1, keepdims=True)
        acc_ref[...] += y * col

    @pl.when(live & last)
    def _():
        out = x_ref[...] + acc_ref[...]
        y_ref[...] = (out * _rms_scale(out)) * gfin_ref[...]

    @pl.when(jnp.logical_not(live) & last)
    def _():
        y_ref[...] = jnp.zeros_like(y_ref)


def _moe_grouped(tile_group, tiles_used, xs, g_ffn, w_router, b_router, wg, wu, wd, g_final, tm):
    p, d = xs.shape
    de = wg.shape[2]
    row = lambda i, e, tg, used: (i, 0)
    const = lambda i, e, tg, used: (0, 0)
    expert = lambda i, e, tg, used: (tg[i] * EXPERTS_PER_GROUP + e, 0, 0)
    grid_spec = pltpu.PrefetchScalarGridSpec(
        num_scalar_prefetch=2, grid=(p // tm, EXPERTS_PER_GROUP),
        in_specs=[pl.BlockSpec((tm, d), row), pl.BlockSpec((1, d), const), pl.BlockSpec((d, LANE), const),
                  pl.BlockSpec((1, LANE), const), pl.BlockSpec((1, d, de), expert), pl.BlockSpec((1, d, de), expert),
                  pl.BlockSpec((1, de, d), expert), pl.BlockSpec((1, d), const)],
        out_specs=pl.BlockSpec((tm, d), row),
        scratch_shapes=[pltpu.VMEM((tm, d), BF16), pltpu.VMEM((tm, LANE), F32), pltpu.VMEM((tm, d), F32)])
    return pl.pallas_call(
        _moe_grouped_kernel, grid_spec=grid_spec, out_shape=jax.ShapeDtypeStruct((p, d), F32),
        compiler_params=_params("parallel", "arbitrary"), name="moe_grouped",
    )(tile_group, tiles_used, xs, g_ffn.reshape(1, d), w_router, b_router, wg, wu, wd, g_final.reshape(1, d))


def _group_dispatch(group, tm):
    n = group.shape[0]
    onehot = (group[:, None] == jnp.arange(N_GROUPS, dtype=I32)[None, :]).astype(I32)
    rank = jnp.cumsum(onehot, axis=0) - onehot
    counts = jnp.sum(onehot, axis=0)
    padded = (counts + tm - 1) // tm * tm
    ends = jnp.cumsum(padded)
    dest = jnp.sum(onehot * ((ends - padded)[None, :] + rank), axis=1).astype(I32)
    rows = -(-n // tm) * tm + N_GROUPS * tm
    tiles = rows // tm
    tile_start = jnp.arange(tiles, dtype=I32) * tm
    tile_group = jnp.minimum(jnp.sum((tile_start[:, None] >= ends[None, :]).astype(I32), axis=1), N_GROUPS - 1)
    return dest, tile_group, (ends[-1:] // tm).astype(I32), rows


def _page_specs(shape_tail, pages_per_step):
    zeros = (0,) * len(shape_tail)
    return [pl.BlockSpec((None,) + shape_tail,
                         functools.partial(lambda b, j, pt, t: (pt[b, j * pages_per_step + t],) + zeros, t=t))
            for t in range(pages_per_step)]


def _stream_pages(pt_ref, pools_and_bufs, sem_ref, pages):
    b, j = pl.program_id(0), pl.program_id(1)
    nj = pl.num_programs(1)
    total = pl.num_programs(0) * nj
    step = b * nj + j
    slot = step % 2

    def copy(pool_ref, buf_ref, page, sl, t):
        return pltpu.make_async_copy(pool_ref.at[page], buf_ref.at[sl, t], sem_ref.at[sl])

    def start(bb, jj, sl):
        for t in range(pages):
            page = pt_ref[bb, jj * pages + t]
            for pool_ref, buf_ref in pools_and_bufs:
                copy(pool_ref, buf_ref, page, sl, t).start()

    @pl.when(step == 0)
    def _():
        start(b, j, slot)

    @pl.when(step + 1 < total)
    def _():
        nxt = step + 1
        start(nxt // nj, nxt % nj, 1 - slot)

    for t in range(pages):
        for pool_ref, buf_ref in pools_and_bufs:
            copy(pool_ref, buf_ref, 0, slot, t).wait()
    return slot


def _dsa_sample_score_kernel(pt_ref, qi_ref, w_ref, kin_ref, pool_ref, s_ref, snew_ref, kbuf_ref, sem_ref, *, pages):
    slot = _stream_pages(pt_ref, [(pool_ref, kbuf_ref)], sem_ref, pages)
    j = pl.program_id(1)
    rows = qi_ref.shape[1]
    s_len = rows // H_IDX
    q = qi_ref[0][:, :D_IDX]
    w = w_ref[0] * (H_IDX ** -0.5 * D_IDX ** -0.5)

    def scores(d):
        d = jnp.maximum(d, 0.0) * w
        return jnp.sum(d.reshape(s_len, H_IDX, d.shape[1]), axis=1)

    k_t = jnp.concatenate([kbuf_ref[slot, t].astype(BF16) for t in range(pages)], axis=1)
    s_ref[0] = scores(_dot(q, k_t))

    @pl.when(j == pl.num_programs(1) - 1)
    def _():
        sn = scores(_dot_nt(q, kin_ref[0][:, :D_IDX].astype(BF16)))
        si = lax.broadcasted_iota(I32, (s_len, PAGE_SIZE), 0)
        ki = lax.broadcasted_iota(I32, (s_len, PAGE_SIZE), 1)
        snew_ref[0] = jnp.where(ki <= si, sn, -jnp.inf)


def _dsa_sample_scores(page_table, qi_s, wi_s, kin_pad, pool_kidx, pages):
    bd, n_pages = page_table.shape
    rows = qi_s.shape[1]
    s_len = rows // H_IDX
    per_b = lambda b, j, pt: (b, 0, 0)
    grid_spec = pltpu.PrefetchScalarGridSpec(
        num_scalar_prefetch=1, grid=(bd, n_pages // pages),
        in_specs=[pl.BlockSpec((1, rows, LANE), per_b), pl.BlockSpec((1, rows, 1), per_b),
                  pl.BlockSpec((1, PAGE_SIZE, LANE), per_b), pl.BlockSpec(memory_space=pl.ANY)],
        out_specs=[pl.BlockSpec((1, s_len, pages * PAGE_SIZE), lambda b, j, pt: (b, 0, j)),
                   pl.BlockSpec((1, s_len, PAGE_SIZE), per_b)],
        scratch_shapes=[pltpu.VMEM((2, pages, D_IDX, PAGE_SIZE), F32), pltpu.SemaphoreType.DMA((2,))])
    return pl.pallas_call(
        functools.partial(_dsa_sample_score_kernel, pages=pages), grid_spec=grid_spec,
        out_shape=[jax.ShapeDtypeStruct((bd, s_len, n_pages * PAGE_SIZE), F32),
                   jax.ShapeDtypeStruct((bd, s_len, PAGE_SIZE), F32)],
        compiler_params=_params("arbitrary", "arbitrary"), name="dsa_sample_scores",
    )(page_table, qi_s, wi_s, kin_pad, pool_kidx)


def _dsa_sample_select_kernel(sp_ref, sn_ref, bp_ref, bn_ref, score_ref, bias_ref, *, k_top):
    nch, q, kc = score_ref.shape
    npast = nch - 1

    def load(c, carry):
        score_ref[c] = sp_ref[:, pl.ds(pl.multiple_of(c * kc, kc), kc)]
        return carry

    lax.fori_loop(0, npast, load, 0)
    score_ref[npast] = jnp.concatenate([sn_ref[...], jnp.full((q, kc - LANE), -jnp.inf, F32)], axis=1)
    _select_bias(score_ref, bias_ref, nch, k_top, (nch * kc - 1).bit_length(), key_axis=1)

    def store(c, carry):
        bp_ref[:, pl.ds(pl.multiple_of(c * kc, kc), kc)] = bias_ref[c]
        return carry

    lax.fori_loop(0, npast, store, 0)
    bn_ref[...] = bias_ref[npast][:, :LANE]


def _dsa_sample_select(s_past, s_new, k_top):
    q, past = s_past.shape
    kc = 512
    nch = past // kc + 1
    return pl.pallas_call(
        functools.partial(_dsa_sample_select_kernel, k_top=k_top),
        out_shape=[jax.ShapeDtypeStruct((q, past), F32), jax.ShapeDtypeStruct((q, LANE), F32)],
        scratch_shapes=[pltpu.VMEM((nch, q, kc), F32), pltpu.VMEM((nch, q, kc), F32)],
        compiler_params=pltpu.CompilerParams(vmem_limit_bytes=VMEM_LIMIT), name="dsa_sample_select",
    )(s_past, s_new)


def _dsa_sample_attn_kernel(pt_ref, qa_ref, bp_ref, bn_ref, kn_ref, vn_ref, kpool_ref, vpool_ref, o_ref,
                            m_ref, l_ref, acc_ref, kbuf_ref, vbuf_ref, sem_ref, *, pages):
    slot = _stream_pages(pt_ref, [(kpool_ref, kbuf_ref), (vpool_ref, vbuf_ref)], sem_ref, pages)
    j = pl.program_id(1)
    rows = qa_ref.shape[1]
    s_len = rows // H_A
    q = qa_ref[0]
    scale = HEAD_DIM ** -0.5

    @pl.when(j == 0)
    def _():
        m_ref[...] = jnp.full_like(m_ref, NEG)
        l_ref[...] = jnp.zeros_like(l_ref)
        acc_ref[...] = jnp.zeros_like(acc_ref)

    def update(s, bias, v, v_dot):
        n = s.shape[1]
        s = (s.reshape(s_len, H_A, n) * scale + bias[:, None, :]).reshape(rows, n)
        m = m_ref[...]
        m_new = jnp.maximum(m, jnp.max(s, axis=1, keepdims=True))
        alpha = jnp.exp(m - m_new)
        p = jnp.exp(s - m_new)
        l_ref[...] = alpha * l_ref[...] + jnp.sum(p, axis=1, keepdims=True)
        acc_ref[...] = alpha * acc_ref[...] + v_dot(p.astype(BF16), v)
        m_ref[...] = m_new

    k_t = jnp.concatenate([kbuf_ref[slot, t].astype(BF16) for t in range(pages)], axis=1)
    v_t = jnp.concatenate([vbuf_ref[slot, t].astype(BF16) for t in range(pages)], axis=1)
    update(_dot(q, k_t), bp_ref[0], v_t, _dot_nt)

    @pl.when(j == pl.num_programs(1) - 1)
    def _():
        update(_dot_nt(q, kn_ref[0].astype(BF16)), bn_ref[0], vn_ref[0].astype(BF16), _dot)
        o_ref[0] = acc_ref[...] / l_ref[...]


def _dsa_sample_attn(page_table, qa_s, bias_past, bias_new, kn_pad, vn_pad, pool_k, pool_v, pages):
    bd, n_pages = page_table.shape
    rows = qa_s.shape[1]
    s_len = rows // H_A
    per_b = lambda b, j, pt: (b, 0, 0)
    grid_spec = pltpu.PrefetchScalarGridSpec(
        num_scalar_prefetch=1, grid=(bd, n_pages // pages),
        in_specs=[pl.BlockSpec((1, rows, LANE), per_b),
                  pl.BlockSpec((1, s_len, pages * PAGE_SIZE), lambda b, j, pt: (b, 0, j)),
                  pl.BlockSpec((1, s_len, LANE), per_b),
                  pl.BlockSpec((1, PAGE_SIZE, LANE), per_b), pl.BlockSpec((1, PAGE_SIZE, LANE), per_b),
                  pl.BlockSpec(memory_space=pl.ANY), pl.BlockSpec(memory_space=pl.ANY)],
        out_specs=pl.BlockSpec((1, rows, LANE), per_b),
        scratch_shapes=[pltpu.VMEM((rows, 1), F32), pltpu.VMEM((rows, 1), F32), pltpu.VMEM((rows, LANE), F32),
                        pltpu.VMEM((2, pages, KV_A * HEAD_DIM, PAGE_SIZE), F32),
                        pltpu.VMEM((2, pages, KV_A * HEAD_DIM, PAGE_SIZE), F32), pltpu.SemaphoreType.DMA((2,))])
    return pl.pallas_call(
        functools.partial(_dsa_sample_attn_kernel, pages=pages), grid_spec=grid_spec,
        out_shape=jax.ShapeDtypeStruct((bd, rows, LANE), F32),
        compiler_params=_params("arbitrary", "arbitrary"), name="dsa_sample_attn",
    )(page_table, qa_s, bias_past, bias_new, kn_pad, vn_pad, pool_k, pool_v)


def _moba_sample_kernel(pt_ref, qb_ref, kn_ref, vn_ref, *refs, pages, n_sel):
    k_refs, v_refs = refs[:pages], refs[pages:2 * pages]
    o_ref, gate_ref, m_ref, l_ref, acc_ref = refs[2 * pages:]
    del pt_ref
    j = pl.program_id(1)
    rows = qb_ref.shape[1]
    s_len = rows // H_B
    nbp = acc_ref.shape[0]
    ppb = MOBA_BLOCK // PAGE_SIZE
    scale = HEAD_DIM ** -0.5

    qs = qb_ref[0]
    pair = (lax.broadcasted_iota(I32, (rows, LANE), 0) % H_B) // 2
    qe = jnp.concatenate([jnp.where(pair == p, qs, jnp.zeros_like(qs)) for p in range(H_B // 2)], axis=1)
    lane = lax.broadcasted_iota(I32, (rows, LANE), 1)

    @pl.when(j == 0)
    def _():
        gate_ref[...] = jnp.full_like(gate_ref, -jnp.inf)
        m_ref[...] = jnp.full_like(m_ref, NEG)
        l_ref[...] = jnp.zeros_like(l_ref)

    nblk = pages // ppb
    blocks = [slice(t * MOBA_BLOCK, (t + 1) * MOBA_BLOCK) for t in range(nblk)]
    k_t = jnp.concatenate([k_refs[t][...].astype(BF16) for t in range(pages)], axis=1)
    qk = _dot(qe, k_t)
    s = qk * scale
    gate_t = [jnp.sum(qk[:, blk], axis=1, keepdims=True) * (1.0 / MOBA_BLOCK) for blk in blocks]
    m_t = [jnp.max(s[:, blk], axis=1, keepdims=True) for blk in blocks]
    p = jnp.exp(s - jnp.concatenate([jnp.broadcast_to(m, (rows, MOBA_BLOCK)) for m in m_t], axis=1))
    l_t = [jnp.sum(p[:, blk], axis=1, keepdims=True) for blk in blocks]
    pb = p.astype(BF16)
    gate, m_blk, l_blk = gate_ref[...], m_ref[...], l_ref[...]
    for t in range(nblk):
        n = j * nblk + t
        v_t = jnp.concatenate([v_refs[ppb * t + i][...].astype(BF16) for i in range(ppb)], axis=1)
        acc_ref[n] = _dot_nt(pb[:, blocks[t]], v_t)
        here = lane == n
        gate = jnp.where(here, gate_t[t], gate)
        m_blk = jnp.where(here, m_t[t], m_blk)
        l_blk = jnp.where(here, l_t[t], l_blk)
    gate_ref[...] = gate
    m_ref[...] = m_blk
    l_ref[...] = l_blk

    @pl.when(j == pl.num_programs(1) - 1)
    def _():
        sel = _top_blocks(gate_ref[...], lane < nbp, lane.astype(F32), n_sel, axis=1)
        s_own = _dot_nt(qe, kn_ref[0].astype(BF16)) * scale
        srow = lax.broadcasted_iota(I32, (rows, LANE), 0) // H_B
        s_own = jnp.where(lane <= srow, s_own, NEG)
        m_blk = jnp.where(sel, m_ref[...], NEG)
        m_all = jnp.maximum(jnp.max(m_blk, axis=1, keepdims=True), jnp.max(s_own, axis=1, keepdims=True))
        wgt = jnp.where(sel, jnp.exp(m_blk - m_all), 0.0)
        p_own = jnp.exp(s_own - m_all)
        l_all = jnp.sum(wgt * l_ref[...], axis=1, keepdims=True) + jnp.sum(p_own, axis=1, keepdims=True)
        o = _dot(p_own.astype(BF16), vn_ref[0].astype(BF16))
        for n in range(nbp):
            o = o + wgt[:, n:n + 1] * acc_ref[n]
        o = o / l_all
        head_lane = lax.broadcasted_iota(I32, (rows, WKV), 1) // HEAD_DIM
        head_row = lax.broadcasted_iota(I32, (rows, WKV), 0) % H_B
        o = jnp.where(head_lane == head_row, o, 0.0)
        o_ref[0] = jnp.sum(o.reshape(s_len, H_B, WKV), axis=1)


def _moba_sample(page_table, qb_s, kn_pad, vn_pad, pool_k, pool_v, pages):
    bd, n_pages = page_table.shape
    rows = qb_s.shape[1]
    s_len = rows // H_B
    nbp = n_pages * PAGE_SIZE // MOBA_BLOCK
    per_b = lambda b, j, pt: (b, 0, 0)
    grid_spec = pltpu.PrefetchScalarGridSpec(
        num_scalar_prefetch=1, grid=(bd, n_pages // pages),
        in_specs=[pl.BlockSpec((1, rows, LANE), per_b), pl.BlockSpec((1, PAGE_SIZE, WKV), per_b),
                  pl.BlockSpec((1, PAGE_SIZE, WKV), per_b)] + _page_specs((WKV, PAGE_SIZE), pages) * 2,
        out_specs=pl.BlockSpec((1, s_len, WKV), per_b),
        scratch_shapes=[pltpu.VMEM((rows, LANE), F32), pltpu.VMEM((rows, LANE), F32), pltpu.VMEM((rows, LANE), F32),
                        pltpu.VMEM((nbp, rows, WKV), F32)])
    return pl.pallas_call(
        functools.partial(_moba_sample_kernel, pages=pages, n_sel=min(TOPK_BLK, nbp)), grid_spec=grid_spec,
        out_shape=jax.ShapeDtypeStruct((bd, s_len, WKV), F32),
        compiler_params=_params("parallel", "arbitrary"), name="moba_sample",
    )(page_table, qb_s, kn_pad, vn_pad, *([pool_k] * pages), *([pool_v] * pages))


def _split_w_in(w_in):
    d = w_in.shape[0]
    sizes = (H_A * HEAD_DIM, KV_A * HEAD_DIM, KV_A * HEAD_DIM, H_IDX * D_IDX, D_IDX, H_IDX, WKV, WKV, WKV, d, d)
    parts, o = [], 0
    for s in sizes:
        parts.append(w_in[:, o:o + s])
        o += s
    return parts


def _expand_heads(w, lane_offsets):
    z = jnp.zeros((w.shape[0], HEAD_DIM), w.dtype)
    parts = []
    for h, off in enumerate(lane_offsets):
        blk = w[:, h * HEAD_DIM:(h + 1) * HEAD_DIM]
        parts += [blk, z] if off == 0 else [z, blk]
    return jnp.concatenate(parts, axis=1)


def _layout_w_in(w_in):
    d = w_in.shape[0]
    qa, ka, va, qi, ki, wi, qb, kb, vb, ga, gb = _split_w_in(w_in)
    heads_per_group = H_A // KV_A
    qa_e = _expand_heads(qa, [(h // heads_per_group) * HEAD_DIM for h in range(H_A)])
    qb_e = _expand_heads(qb, [(h % 2) * HEAD_DIM for h in range(H_B)])
    kiwi = jnp.concatenate([ki, wi, jnp.zeros((d, LANE - D_IDX - H_IDX), w_in.dtype)], axis=1)
    ki_pad = jnp.concatenate([ki, jnp.zeros((d, LANE - D_IDX), w_in.dtype)], axis=1)
    w_sample = jnp.concatenate([qa_e, _expand_heads(qi, [0] * H_IDX), qb_e, ka, va, kiwi, kb, vb, ga, gb], axis=1)
    w_t = jnp.concatenate([qa_e, qi, qb_e, ka, va, ki, kb, vb, wi, jnp.zeros_like(wi)], axis=1).T
    w_r = jnp.concatenate([ka, ki_pad, kb, ga, gb], axis=1)
    return w_sample.astype(BF16), w_t.astype(BF16), w_r.astype(BF16)


def _rope_angles(pos):
    half = ROT_DIM // 2
    inv = ROPE_THETA ** (-jnp.arange(half, dtype=F32) / half)
    ang = pos.astype(F32)[:, None] * inv[None, :]
    return jnp.cos(ang), jnp.sin(ang)


def _rope_tables(pos):
    cos, sin = _rope_angles(pos)
    n = pos.shape[0]
    pad = jnp.zeros((n, HEAD_DIM - ROT_DIM), F32)
    zero = jnp.zeros((n, ROT_DIM // 2), F32)
    c = jnp.concatenate([cos, cos, pad + 1.0], axis=1)
    s1 = jnp.concatenate([-sin, zero, pad], axis=1)
    s2 = jnp.concatenate([zero, sin, pad], axis=1)
    return tuple(jnp.tile(t, (1, LANE // HEAD_DIM)) for t in (c, s1, s2))


def _pages_transposed(pool):
    n_pool, page = pool.shape[:2]
    nd = pool.ndim
    return jnp.transpose(pool, (0,) + tuple(range(2, nd)) + (1,)).reshape(n_pool, -1, page)


def _pad_rows(x, rows):
    b, s, w = x.shape
    return jnp.concatenate([x, jnp.zeros((b, rows - s, w), x.dtype)], axis=1)


def _ffn(oa, ob, ga, gb, x2d, lw, g_final, tm_merge, tm_moe, grouped):
    x1, h2, cw = _merge(oa, ob, ga, gb, x2d, lw["wa"], lw["wb"], lw["wo"], lw["g_ffn"], lw["w_router"],
                        lw["b_router"], tm_merge)
    if not grouped:
        return _moe(h2, cw, x1, lw["wg"], lw["wu"], lw["wd"], g_final, tm_moe)
    dest, tile_group, tiles_used, rows = _group_dispatch(cw[:, GSEL_LANE].astype(I32), tm_moe)
    xs = _scatter_rows(dest, x1, rows, TM_ROWS)
    ys = _moe_grouped(tile_group, tiles_used, xs, lw["g_ffn"], lw["w_router"], lw["b_router"], lw["wg"], lw["wu"], lw["wd"],
                      g_final, tm_moe)
    return _gather_rows(dest, ys, x1.shape[0], TM_ROWS)


@jax.jit
def kernel(x_prompt, x_sample, cache_k_a, cache_v_a, cache_k_idx, cache_k_b, cache_v_b, page_table, w_in, w_br_a,
           w_br_b, w_out, g_mix, g_ffn, w_grp, b_grp, w_exp, b_exp, w_e_gate, w_e_up, w_e_down, g_final):
    bsz, t, d = x_prompt.shape
    bd, s_len, _ = x_sample.shape
    depth = w_in.shape[0]
    n_pages = page_table.shape[1]
    past = n_pages * PAGE_SIZE
    assert depth == 1, "the final RMSNorm is fused into the last layer's MoE kernel"
    assert past % MOBA_BLOCK == 0 and s_len <= PAGE_SIZE and t % TM_PROJ == 0

    xp = x_prompt.reshape(bsz * t, d)
    xs = x_sample.reshape(bd * s_len, d)
    ns = bd * s_len

    l = 0
    zpad = jnp.zeros((d, LANE - N_GROUPS - N_EXPERTS), F32)
    lw = dict(
        wa=w_br_a[l].astype(BF16), wb=w_br_b[l].astype(BF16), wo=w_out[l].astype(BF16), g_ffn=g_ffn[l],
        w_router=jnp.concatenate([w_grp[l], w_exp[l], zpad], axis=1),
        b_router=jnp.concatenate([b_grp[l], b_exp[l], zpad[0]]).reshape(1, LANE),
        wg=w_e_gate[l].astype(BF16), wu=w_e_up[l].astype(BF16), wd=w_e_down[l].astype(BF16))
    w_sample, w_t, w_r = _layout_w_in(w_in[l])

    pos_p = jnp.arange(t)
    cos_p, sin_p = _rope_angles(pos_p)
    (qa_t, qi_t, qb_t, ka_t, va_t, ki_t, kb_t, vb_t, wi_t, kab, kib, kbb, va_tc, vb_tc, kmean, ga, gb) = _project_prompt(
        xp, g_mix[l], w_t, w_r, (cos_p.T, sin_p.T), _rope_tables(pos_p), bsz, t)
    o_a = _dsa_prompt(qa_t, qi_t, wi_t, kib, kab, va_tc, bsz, t)
    o_b = _moba_prompt(qb_t, kmean.reshape(bsz * t // MOBA_BLOCK, WKV), kbb, vb_tc, bsz, t)
    y_prompt = _ffn(o_a, o_b, ga, gb, xp, lw, g_final, TM_MERGE, TM_MOE, True).reshape(bsz, t, d)

    def heads_last(x_t, heads):
        return jnp.transpose(x_t.reshape(1, bsz, heads, HEAD_DIM, t), (0, 1, 4, 2, 3))

    rows_p = (heads_last(ka_t, KV_A), heads_last(va_t, KV_A), jnp.transpose(ki_t, (0, 2, 1))[None],
              heads_last(kb_t, H_B), heads_last(vb_t, H_B))

    tabs_s = tuple(jnp.tile(tb, (bd, 1)) for tb in _rope_tables(past + jnp.arange(s_len)))
    qa, qi, qb, ka, va, kiwi, kb, vb, ga, gb = _project_sample(xs, g_mix[l], w_sample, tabs_s)
    rows = s_len * N_HEAD_SLABS
    qa_s = qa.reshape(bd, rows, LANE)
    qi_s = qi.reshape(bd, rows, LANE)
    qb_s = qb.reshape(bd, rows, LANE)
    wi_s = kiwi[:, D_IDX:D_IDX + H_IDX].reshape(bd, rows, 1)
    s_past, s_new = _dsa_sample_scores(page_table, qi_s, wi_s, _pad_rows(kiwi.reshape(bd, s_len, LANE), PAGE_SIZE),
                                       _pages_transposed(cache_k_idx[l]), min(PAGES_SCORES, n_pages))
    bias_past, bias_new = _dsa_sample_select(s_past.reshape(ns, past), s_new.reshape(ns, PAGE_SIZE),
                                             min(TOPK_IDX, (past + s_len) // 4))
    o_a = _dsa_sample_attn(page_table, qa_s, bias_past.reshape(bd, s_len, past),
                           bias_new.reshape(bd, s_len, PAGE_SIZE),
                           _pad_rows(ka.reshape(bd, s_len, LANE), PAGE_SIZE),
                           _pad_rows(va.reshape(bd, s_len, LANE), PAGE_SIZE),
                           _pages_transposed(cache_k_a[l]), _pages_transposed(cache_v_a[l]), min(PAGES_ATTN, n_pages))
    o_a = o_a.reshape(bd, s_len, H_A, KV_A, HEAD_DIM)
    hpg = H_A // KV_A
    o_a = jnp.concatenate([o_a[:, :, g * hpg:(g + 1) * hpg, g] for g in range(KV_A)], axis=2)
    o_a = o_a.reshape(ns, H_A * HEAD_DIM).astype(BF16)
    o_b = _moba_sample(page_table, qb_s, _pad_rows(kb.reshape(bd, s_len, WKV), PAGE_SIZE),
                       _pad_rows(vb.reshape(bd, s_len, WKV), PAGE_SIZE),
                       _pages_transposed(cache_k_b[l]), _pages_transposed(cache_v_b[l]), min(PAGES_MOBA, n_pages))
    o_b = o_b.reshape(ns, WKV).astype(BF16)
    y_sample = _ffn(o_a, o_b, ga, gb, xs, lw, g_final, ns, ns, False).reshape(bd, s_len, d)
    rows_s = (ka.reshape(1, bd, s_len, KV_A, HEAD_DIM), va.reshape(1, bd, s_len, KV_A, HEAD_DIM),
              kiwi[:, :D_IDX].reshape(1, bd, s_len, D_IDX), kb.reshape(1, bd, s_len, H_B, HEAD_DIM),
              vb.reshape(1, bd, s_len, H_B, HEAD_DIM))
    return (y_prompt, y_sample) + rows_p + rows_s
```
